```python
import math
import jax, jax.numpy as jnp
from jax import lax
import numpy as np

D_MODEL = 1024
BATCH = 2
SEQ = 16384
DEPTH = 1
DEC_BATCH = 16
DEC_SEQ = 32
PAST_LEN = 1024

CHUNK = 64
Q_BLOCK = 128
HA = 4
DH = D_MODEL // 8
D_A = HA * DH
HB = 4
DK = D_MODEL // 16
DV = 2 * DK
D_QB = HB * 2 * DK
D_VB = HB * DV
CONV_W = 4
FF = 4 * D_MODEL
NUM_BUCKETS = 32
MAX_DISTANCE = 128
EPS = 1e-6
N_IN = 4 * D_A + 2 * HA + 2 * D_QB + D_VB + 2 * D_MODEL

kernel_name = "hybrid_mlstm_diffattn_streaming_step"


def rmsnorm(x, g):
    xf = x.astype(jnp.float32)
    y = xf * lax.rsqrt(jnp.mean(xf * xf, axis=-1, keepdims=True) + EPS)
    return (y * g.astype(jnp.float32)).astype(x.dtype)


def rel_bucket(rel_pos):
    half = NUM_BUCKETS // 2
    max_exact = half // 2
    n = jnp.abs(rel_pos)
    large = max_exact + (jnp.log(jnp.maximum(n, 1).astype(jnp.float32) / max_exact)
                         / math.log(MAX_DISTANCE / max_exact) * (half - max_exact)).astype(jnp.int32)
    large = jnp.minimum(large, half - 1)
    return jnp.where(rel_pos > 0, half, 0) + jnp.where(n < max_exact, n, large)


def causal_conv(x, buf, w, b):
    T = x.shape[1]
    xp = jnp.concatenate([buf.astype(x.dtype), x], axis=1)
    out = b + xp[:, 0:T] * w[0]
    for j in range(1, CONV_W):
        out = out + xp[:, j:j + T] * w[j]
    return out, xp[:, T:]


def mlstm_chunk(carry, xs):
    C, n, m = carry
    q, k, v, ig, lf = xs
    L = q.shape[2]
    b = jnp.cumsum(lf, axis=-1)
    a = b + m[..., None]
    causal = jnp.tril(jnp.ones((L, L), dtype=bool))
    dlog = jnp.where(causal, b[..., :, None] - b[..., None, :] + ig[..., None, :], -jnp.inf)
    m_t = jnp.maximum(a, jnp.max(dlog, axis=-1))
    w_inter = jnp.exp(a - m_t)
    s = jnp.exp(dlog - m_t[..., None]) * jnp.einsum('bhtd,bhsd->bhts', q, k)
    num = w_inter[..., None] * jnp.einsum('bhvd,bhtd->bhtv', C, q) + jnp.einsum('bhts,bhsv->bhtv', s, v)
    den = w_inter * jnp.einsum('bhd,bhtd->bht', n, q) + jnp.sum(s, axis=-1)
    h = num / jnp.maximum(jnp.abs(den), jnp.exp(-m_t))[..., None]
    g = b[..., -1]
    ls = g[..., None] - b + ig
    m_new = jnp.maximum(g + m, jnp.max(ls, axis=-1))
    decay = jnp.exp(g + m - m_new)
    ws = jnp.exp(ls - m_new[..., None])
    C_new = decay[..., None, None] * C + jnp.einsum('bhs,bhsv,bhsd->bhvd', ws, v, k)
    n_new = decay[..., None] * n + jnp.einsum('bhs,bhsd->bhd', ws, k)
    return (C_new, n_new, m_new), h


def mlstm(q, k, v, ig, lf, C, n, m):
    B, H, T, D = q.shape
    L = min(T, CHUNK)
    nc = T // L

    def chunks(t):
        return jnp.moveaxis(t.reshape(t.shape[:2] + (nc, L) + t.shape[3:]), 2, 0)

    (C, n, m), hs = lax.scan(mlstm_chunk, (C, n, m), (chunks(q), chunks(k), chunks(v), chunks(ig), chunks(lf)))
    h = jnp.moveaxis(hs, 0, 2).reshape(B, H, T, D)
    return h, C, n, m


def diff_attention(q, k, v, q_pos, k_pos, rel_bias, lam):
    B, T = q.shape[:2]
    qb = min(T, Q_BLOCK)
    nb = T // qb
    q_blocks = jnp.moveaxis(q.reshape((B, nb, qb) + q.shape[2:]), 1, 0)
    p_blocks = q_pos.reshape(nb, qb)
    scale = DK ** -0.5
    k_chunk = k_pos // CHUNK

    def block(args):
        qi, pi = args
        s = jnp.einsum('bqhcd,bshcd->bhcqs', qi, k).astype(jnp.float32) * scale
        bias = rel_bias[rel_bucket(k_pos[None, :] - pi[:, None])].astype(jnp.float32)
        s = s + jnp.transpose(bias, (2, 0, 1))[None, :, None]
        allowed = k_chunk[None, :] <= (pi // CHUNK)[:, None]
        p = jax.nn.softmax(jnp.where(allowed, s, -jnp.inf), axis=-1)
        a = p[:, :, 0] - lam * p[:, :, 1]
        return jnp.einsum('bhqs,bshd->bqhd', a.astype(v.dtype), v)

    out = lax.map(block, (q_blocks, p_blocks))
    return jnp.moveaxis(out, 0, 1).reshape((B, T) + v.shape[2:])


def trunk_layer(x, k_past, v_past, C0, n0, m0, conv0, rel_bias, g_pre_mix, g_post_mix, g_pre_ffn,
                g_post_ffn, w_in, b_in, conv_w, conv_b, g_head_a, w_pa, lambda_q1, lambda_k1,
                lambda_q2, lambda_k2, g_head_b, w_pb, w_out, w_ff1, w_ff2, lambda_init):
    f32 = jnp.float32
    B, T, _ = x.shape
    P = k_past.shape[1]
    u = rmsnorm(x, g_pre_mix)
    proj = u @ w_in + b_in
    sizes = (2 * D_A, D_A, D_A, HA, HA, D_QB, D_QB, D_VB, D_MODEL)
    points = []
    acc = 0
    for sz in sizes:
        acc += sz
        points.append(acc)
    qk_a, v_a, o_a, i_a, f_a, q_b, k_b, v_b, gate_a, gate_b = jnp.split(proj, points, axis=-1)

    qk_c, conv_new = causal_conv(qk_a, conv0, conv_w, conv_b)
    qk_c = jax.nn.silu(qk_c)

    def heads(t):
        return jnp.transpose(t.reshape(B, T, HA, DH), (0, 2, 1, 3)).astype(f32)

    q_m = heads(qk_c[..., :D_A])
    k_m = heads(qk_c[..., D_A:]) * (DH ** -0.5)
    v_m = heads(v_a)
    ig = jnp.transpose(i_a, (0, 2, 1)).astype(f32)
    lf = jax.nn.log_sigmoid(jnp.transpose(f_a, (0, 2, 1)).astype(f32))
    h, C_new, n_new, m_new = mlstm(q_m, k_m, v_m, ig, lf, C0.astype(f32), n0.astype(f32), m0.astype(f32))
    h = rmsnorm(jnp.transpose(h, (0, 2, 1, 3)), g_head_a)
    h_a = (h.reshape(B, T, D_A) * jax.nn.sigmoid(o_a.astype(f32))).astype(x.dtype)
    y_a = h_a @ w_pa

    qd = q_b.reshape(B, T, HB, 2, DK)
    kd = k_b.reshape(B, T, HB, 2, DK)
    vd = v_b.reshape(B, T, HB, DV)
    k_all = jnp.concatenate([k_past.astype(kd.dtype).reshape(B, P, HB, 2, DK), kd], axis=1)
    v_all = jnp.concatenate([v_past.astype(vd.dtype), vd], axis=1)
    lam = (jnp.exp(jnp.sum(lambda_q1.astype(f32) * lambda_k1.astype(f32)))
           - jnp.exp(jnp.sum(lambda_q2.astype(f32) * lambda_k2.astype(f32))) + lambda_init)
    q_pos = P + jnp.arange(T, dtype=jnp.int32)
    k_pos = jnp.arange(P + T, dtype=jnp.int32)
    o_b = diff_attention(qd, k_all, v_all, q_pos, k_pos, rel_bias, lam)
    o_b = rmsnorm(o_b, g_head_b) * (1.0 - lambda_init)
    y_b = o_b.reshape(B, T, D_VB) @ w_pb

    mix = (jax.nn.sigmoid(gate_a) * y_a + jax.nn.sigmoid(gate_b) * y_b) @ w_out
    x = x + rmsnorm(mix, g_post_mix)

    hf = jnp.square(jax.nn.relu(rmsnorm(x, g_pre_ffn) @ w_ff1)) @ w_ff2
    x = x + rmsnorm(hf, g_post_ffn)
    return x, kd.reshape(B, T, HB, 2 * DK), vd, C_new, n_new, m_new, conv_new


def setup_inputs(seed: int = 0) -> dict:
    key = jax.random.key(seed)
    ks = jax.random.split(key, 32)
    f32 = jnp.float32

    def nrm(k, shape, s):
        return jax.random.normal(k, shape, f32) * s

    def gain(k, shape):
        return 1.0 + 0.05 * jax.random.normal(k, shape, f32)

    off_f = 4 * D_A + HA
    b_in = nrm(ks[14], (DEPTH, N_IN), 0.02)
    b_in = b_in.at[:, off_f:off_f + HA].add(jnp.linspace(3.0, 6.0, HA, dtype=f32))
    return {
        "x_prompt": nrm(ks[0], (BATCH, SEQ, D_MODEL), 1.0),
        "x_sample": nrm(ks[1], (DEC_BATCH, DEC_SEQ, D_MODEL), 1.0),
        "cache_k": nrm(ks[2], (DEPTH, DEC_BATCH, PAST_LEN, HB, 2 * DK), 1.0),
        "cache_v": nrm(ks[3], (DEPTH, DEC_BATCH, PAST_LEN, HB, DV), 1.0),
        "state_C": nrm(ks[4], (DEPTH, DEC_BATCH, HA, DH, DH), 0.1),
        "state_n": nrm(ks[5], (DEPTH, DEC_BATCH, HA, DH), 0.1),
        "state_m": nrm(ks[6], (DEPTH, DEC_BATCH, HA), 1.0),
        "state_conv": nrm(ks[7], (DEPTH, DEC_BATCH, CONV_W - 1, 2 * D_A), 1.0),
        "rel_bias": nrm(ks[8], (NUM_BUCKETS, HB), 0.5),
        "g_pre_mix": gain(ks[9], (DEPTH, D_MODEL)),
        "g_post_mix": gain(ks[10], (DEPTH, D_MODEL)),
        "g_pre_ffn": gain(ks[11], (DEPTH, D_MODEL)),
        "g_post_ffn": gain(ks[12], (DEPTH, D_MODEL)),
        "w_in": nrm(ks[13], (DEPTH, D_MODEL, N_IN), D_MODEL ** -0.5),
        "b_in": b_in,
        "conv_w": nrm(ks[15], (DEPTH, CONV_W, 2 * D_A), CONV_W ** -0.5),
        "conv_b": nrm(ks[16], (DEPTH, 2 * D_A), 0.02),
        "g_head_a": gain(ks[17], (DEPTH, HA, DH)),
        "w_pa": nrm(ks[18], (DEPTH, D_A, D_MODEL), D_A ** -0.5),
        "lambda_q1": nrm(ks[19], (DEPTH, DK), 0.1),
        "lambda_k1": nrm(ks[20], (DEPTH, DK), 0.1),
        "lambda_q2": nrm(ks[21], (DEPTH, DK), 0.1),
        "lambda_k2": nrm(ks[22], (DEPTH, DK), 0.1),
        "g_head_b": gain(ks[23], (DEPTH, DV)),
        "w_pb": nrm(ks[24], (DEPTH, D_VB, D_MODEL), D_VB ** -0.5),
        "w_out": nrm(ks[25], (DEPTH, D_MODEL, D_MODEL), D_MODEL ** -0.5),
        "w_ff1": nrm(ks[26], (DEPTH, D_MODEL, FF), D_MODEL ** -0.5),
        "w_ff2": nrm(ks[27], (DEPTH, FF, D_MODEL), FF ** -0.5),
    }


def reference(x_prompt, x_sample, cache_k, cache_v, state_C, state_n, state_m, state_conv, rel_bias,
              g_pre_mix, g_post_mix, g_pre_ffn, g_post_ffn, w_in, b_in, conv_w, conv_b, g_head_a, w_pa,
              lambda_q1, lambda_k1, lambda_q2, lambda_k2, g_head_b, w_pb, w_out, w_ff1, w_ff2):
    f32 = jnp.float32
    yp, ys = x_prompt, x_sample
    kp_l, vp_l, Cp_l, np_l, mp_l, cp_l = [], [], [], [], [], []
    ks_l, vs_l, Cs_l, ns_l, ms_l, cs_l = [], [], [], [], [], []
    for l in range(DEPTH):
        lw = (rel_bias, g_pre_mix[l], g_post_mix[l], g_pre_ffn[l], g_post_ffn[l], w_in[l], b_in[l],
              conv_w[l], conv_b[l], g_head_a[l], w_pa[l], lambda_q1[l], lambda_k1[l], lambda_q2[l],
              lambda_k2[l], g_head_b[l], w_pb[l], w_out[l], w_ff1[l], w_ff2[l],
              0.8 - 0.6 * math.exp(-0.3 * l))
        B = yp.shape[0]
        yp, kp, vp, Cp, n_p, mp, cp = trunk_layer(
            yp, jnp.zeros((B, 0, HB, 2 * DK), yp.dtype), jnp.zeros((B, 0, HB, DV), yp.dtype),
            jnp.zeros((B, HA, DH, DH), f32), jnp.zeros((B, HA, DH), f32), jnp.zeros((B, HA), f32),
            jnp.zeros((B, CONV_W - 1, 2 * D_A), yp.dtype), *lw)
        ys, k_s, v_s, C_s, n_s, m_s, c_s = trunk_layer(
            ys, cache_k[l], cache_v[l], state_C[l], state_n[l], state_m[l], state_conv[l], *lw)
        kp_l.append(kp); vp_l.append(vp); Cp_l.append(Cp); np_l.append(n_p); mp_l.append(mp); cp_l.append(cp)
        ks_l.append(k_s); vs_l.append(v_s); Cs_l.append(C_s); ns_l.append(n_s); ms_l.append(m_s); cs_l.append(c_s)
    return (yp, ys,
            jnp.stack(kp_l), jnp.stack(vp_l), jnp.stack(Cp_l), jnp.stack(np_l), jnp.stack(mp_l), jnp.stack(cp_l),
            jnp.stack(ks_l), jnp.stack(vs_l), jnp.stack(Cs_l), jnp.stack(ns_l), jnp.stack(ms_l), jnp.stack(cs_l))
```

```python
import functools
import math

import numpy as np
import jax
import jax.numpy as jnp
from jax import lax
from jax.experimental import pallas as pl
from jax.experimental.pallas import tpu as pltpu

F32 = jnp.float32
BF16 = jnp.bfloat16

D_MODEL = 1024
HA = 4
DH = 128
D_A = HA * DH
HB = 4
DK = 64
DV = 2 * DK
D_QB = HB * 2 * DK
D_VB = HB * DV
CONV_W = 4
FF = 4 * D_MODEL
NUM_BUCKETS = 32
CHUNK = 64
EPS = 1e-6
LAMBDA_INIT = 0.8 - 0.6 * math.exp(-0.3 * 0)
NEG = -1e30
LOG2E = math.log2(math.e)
QSCALE = DK ** -0.5 * LOG2E
N_GATE_PAD = 128
ROW_TILE = 512
OUT_ROW_TILE = 256
FF_CHUNK = 1024
VMEM_LIMIT_BYTES = 56 * 1024 * 1024

NT_DIMS = (((1,), (1,)), ((), ()))
TN_DIMS = (((0,), (0,)), ((), ()))


def _const_spec(shape):
    zeros = (0,) * len(shape)
    return pl.BlockSpec(shape, lambda *_: zeros, pipeline_mode=pl.Buffered(1))


def _rms(x, g):
    return x * lax.rsqrt(jnp.mean(x * x, axis=-1, keepdims=True) + EPS) * g


def _log_sigmoid(x):
    return -(jnp.maximum(-x, 0.0) + jnp.log(1.0 + jnp.exp(-jnp.abs(x))))


def _split3(x):
    x1 = x.astype(BF16)
    r = x - x1.astype(F32)
    x2 = r.astype(BF16)
    r = r - x2.astype(F32)
    return x1, x2, r.astype(BF16)


def _mm(a, b):
    return jnp.dot(a, b, preferred_element_type=F32)


def _proj_kernel(x_ref, g_ref, wqk_ref, bqk_ref, wva_ref, bva_ref, wg_ref, bg_ref, wk_ref, bk_ref,
                 wv_ref, bv_ref, wqt_ref, bqt_ref, wvt_ref, bvt_ref,
                 qk_ref, va_ref, gate_ref, k_ref, k16_ref, v_ref, qt_ref, vt_ref):
    u = _rms(x_ref[...], g_ref[...]).astype(BF16)
    qk_ref[...] = _mm(u, wqk_ref[...]) + bqk_ref[...]
    va_ref[...] = _mm(u, wva_ref[...]) + bva_ref[...]
    gate_ref[...] = (_mm(u, wg_ref[...]) + bg_ref[...])[:, :2 * HA]
    k = _mm(u, wk_ref[...]) + bk_ref[...]
    k_ref[...] = k
    k16_ref[...] = k.astype(BF16)
    v_ref[...] = _mm(u, wv_ref[...]) + bv_ref[...]
    qt = lax.dot_general(wqt_ref[...], u, NT_DIMS, preferred_element_type=F32) + bqt_ref[...]
    qt_ref[...] = (qt * QSCALE).astype(BF16)
    vt = lax.dot_general(wvt_ref[...], u, NT_DIMS, preferred_element_type=F32) + bvt_ref[...]
    vt = vt.astype(BF16)
    for h in range(HB):
        vt_ref[h] = vt[h * DV:(h + 1) * DV, :]


def _proj(x2d, g_pre, w_in, b_in):
    rows = x2d.shape[0]
    tm = min(rows, ROW_TILE)
    nblk = rows // tm
    o_qk, o_va, o_oa, o_i, o_q, o_k, o_v, o_ga = 0, 2 * D_A, 3 * D_A, 4 * D_A, 4 * D_A + 2 * HA, \
        4 * D_A + 2 * HA + D_QB, 4 * D_A + 2 * HA + 2 * D_QB, 4 * D_A + 2 * HA + 2 * D_QB + D_VB
    del o_oa, o_ga

    def cols(lo, n):
        return w_in[:, lo:lo + n].astype(BF16), b_in[lo:lo + n].reshape(1, n)

    wqk, bqk = cols(o_qk, 2 * D_A)
    wva, bva = cols(o_va, D_A)
    wg = jnp.pad(w_in[:, o_i:o_i + 2 * HA], ((0, 0), (0, N_GATE_PAD - 2 * HA))).astype(BF16)
    bg = jnp.pad(b_in[o_i:o_i + 2 * HA], (0, N_GATE_PAD - 2 * HA)).reshape(1, N_GATE_PAD)
    wk, bk = cols(o_k, D_QB)
    wv, bv = cols(o_v, D_VB)
    wqt = w_in[:, o_q:o_q + D_QB].T.astype(BF16)
    bqt = b_in[o_q:o_q + D_QB].reshape(D_QB, 1)
    wvt = w_in[:, o_v:o_v + D_VB].T.astype(BF16)
    bvt = b_in[o_v:o_v + D_VB].reshape(D_VB, 1)
    consts = (g_pre.reshape(1, D_MODEL), wqk, bqk, wva, bva, wg, bg, wk, bk, wv, bv, wqt, bqt, wvt, bvt)

    def row_spec(n):
        return pl.BlockSpec((tm, n), lambda i: (i, 0))

    return pl.pallas_call(
        _proj_kernel,
        grid=(nblk,),
        in_specs=[row_spec(D_MODEL)] + [_const_spec(c.shape) for c in consts],
        out_specs=[row_spec(2 * D_A), row_spec(D_A), row_spec(2 * HA), row_spec(D_QB), row_spec(D_QB),
                   row_spec(D_VB),
                   pl.BlockSpec((None, D_QB, tm), lambda i: (i, 0, 0)),
                   pl.BlockSpec((HB, None, DV, tm), lambda i: (0, i, 0, 0))],
        out_shape=[jax.ShapeDtypeStruct((rows, 2 * D_A), F32), jax.ShapeDtypeStruct((rows, D_A), F32),
                   jax.ShapeDtypeStruct((rows, 2 * HA), F32), jax.ShapeDtypeStruct((rows, D_QB), F32),
                   jax.ShapeDtypeStruct((rows, D_QB), BF16), jax.ShapeDtypeStruct((rows, D_VB), F32),
                   jax.ShapeDtypeStruct((nblk, D_QB, tm), BF16),
                   jax.ShapeDtypeStruct((HB, nblk, DV, tm), BF16)],
        compiler_params=pltpu.CompilerParams(dimension_semantics=("parallel",),
                                             vmem_limit_bytes=VMEM_LIMIT_BYTES),
        name="proj",
    )(x2d, *consts)


def _mlstm_kernel(qk_ref, va_ref, gcol_ref, grow_ref, conv0_ref, c0_ref, n0_ref, m0_ref, cw_ref, cb_ref,
                  gha_ref, hn_ref, c_ref, n_ref, m_ref, xp_sc, q_sc, k_sc, st_sc, m_sc, *, chunk, n_chunks):
    j = pl.program_id(1)
    tb = chunk * n_chunks

    @pl.when(j == 0)
    def _():
        xp_sc[0:8, :] = conv0_ref[...]
        for h in range(HA):
            st_sc[h, :, 0:DH] = c0_ref[h].T
            st_sc[h, :, DH:2 * DH] = jnp.broadcast_to(n0_ref[h:h + 1, :], (DH, DH)).T
        m_sc[...] = m0_ref[...]

    @pl.when(j > 0)
    def _():
        xp_sc[0:8, :] = xp_sc[tb:tb + 8, :]

    xp_sc[8:8 + tb, :] = qk_ref[...]
    cw = cw_ref[...]
    conv = cb_ref[...]
    for t in range(CONV_W):
        conv = conv + xp_sc[8 - (CONV_W - 1) + t:8 - (CONV_W - 1) + t + tb, :] * cw[t:t + 1, :]
    act = conv * jax.nn.sigmoid(conv)
    q_sc[...] = act[:, :D_A].astype(BF16)
    k_sc[...] = (act[:, D_A:] * DH ** -0.5).astype(BF16)

    rr = lax.broadcasted_iota(jnp.int32, (chunk, chunk), 0)
    cc = lax.broadcasted_iota(jnp.int32, (chunk, chunk), 1)
    tril = cc <= rr
    ltri = tril.astype(BF16)
    utri = (rr <= cc).astype(BF16)
    ones = jnp.ones((chunk, DH), F32)

    for c in range(n_chunks):
        sl = slice(c * chunk, (c + 1) * chunk)
        gc = gcol_ref[sl, :]
        ig_c = gc[:, 0:HA]
        b_c = sum(_mm(ltri, p) for p in _split3(_log_sigmoid(gc[:, HA:2 * HA])))
        g = b_c[chunk - 1:chunk, :]
        ls_c = g - b_c + ig_c
        m_old = m_sc[0:1, 0:HA]
        m_new = jnp.maximum(g + m_old, jnp.max(ls_c, axis=0, keepdims=True))
        ws_c = jnp.exp(ls_c - m_new)
        decay = jnp.exp(g + m_old - m_new)
        a_c = b_c + m_old
        m_sc[0:1, 0:HA] = m_new
        gr = grow_ref[c]
        b_r = sum(_mm(p, utri) for p in _split3(_log_sigmoid(gr[HA:2 * HA, :])))
        c_r = gr[0:HA, :] - b_r
        for h in range(HA):
            hs = slice(h * DH, (h + 1) * DH)
            dlog = jnp.where(tril, b_c[:, h:h + 1] + c_r[h:h + 1, :], NEG)
            a_h = a_c[:, h:h + 1]
            m_t = jnp.maximum(a_h, jnp.max(dlog, axis=1, keepdims=True))
            p = jnp.exp(dlog - m_t)
            w_int = jnp.exp(a_h - m_t)
            q_h = q_sc[sl, hs]
            k_h = k_sc[sl, hs]
            vaug = jnp.concatenate([va_ref[sl, hs], ones], axis=1)
            s = (p * lax.dot_general(q_h, k_h, NT_DIMS, preferred_element_type=F32)).astype(BF16)
            st = st_sc[h]
            acc = w_int * _mm(q_h, st.astype(BF16)) + _mm(s, vaug.astype(BF16))
            hh = acc[:, :DH] / jnp.maximum(jnp.abs(acc[:, DH:]), jnp.exp(-m_t))
            hn_ref[sl, hs] = _rms(hh, gha_ref[h:h + 1, :])
            wv = (ws_c[:, h:h + 1] * vaug).astype(BF16)
            st_sc[h] = decay[:, h:h + 1] * st + lax.dot_general(k_h, wv, TN_DIMS, preferred_element_type=F32)

    @pl.when(j == pl.num_programs(1) - 1)
    def _():
        for h in range(HA):
            st = st_sc[h]
            c_ref[h] = st[:, :DH].T
            n_ref[h:h + 1, :] = st[:, DH:].T[0:1, :]
        m_ref[...] = m_sc[...]


def _mlstm(qk_a, v_a, gates, conv0, c0, n0, m0, conv_w, conv_b, g_head_a):
    bsz, t, _ = qk_a.shape
    chunk = min(t, CHUNK)
    n_chunks = min(t // chunk, ROW_TILE // chunk)
    tb = chunk * n_chunks
    nblk = t // tb
    grow = gates.reshape(bsz, t // chunk, chunk, 2 * HA).transpose(0, 1, 3, 2)
    conv0p = jnp.pad(conv0, ((0, 0), (8 - (CONV_W - 1), 0), (0, 0)))
    m0p = jnp.pad(m0, ((0, 0), (0, 128 - HA)))[:, None, :] * jnp.ones((1, 8, 1), F32)
    kern = functools.partial(_mlstm_kernel, chunk=chunk, n_chunks=n_chunks)
    return pl.pallas_call(
        kern,
        grid=(bsz, nblk),
        in_specs=[pl.BlockSpec((None, tb, 2 * D_A), lambda b, j: (b, j, 0)),
                  pl.BlockSpec((None, tb, D_A), lambda b, j: (b, j, 0)),
                  pl.BlockSpec((None, tb, 2 * HA), lambda b, j: (b, j, 0)),
                  pl.BlockSpec((None, n_chunks, 2 * HA, chunk), lambda b, j: (b, j, 0, 0)),
                  pl.BlockSpec((None, 8, 2 * D_A), lambda b, j: (b, 0, 0)),
                  pl.BlockSpec((None, HA, DH, DH), lambda b, j: (b, 0, 0, 0)),
                  pl.BlockSpec((None, HA, DH), lambda b, j: (b, 0, 0)),
                  pl.BlockSpec((None, 8, 128), lambda b, j: (b, 0, 0)),
                  _const_spec((CONV_W, 2 * D_A)), _const_spec((1, 2 * D_A)), _const_spec((HA, DH))],
        out_specs=[pl.BlockSpec((None, tb, D_A), lambda b, j: (b, j, 0)),
                   pl.BlockSpec((None, HA, DH, DH), lambda b, j: (b, 0, 0, 0)),
                   pl.BlockSpec((None, HA, DH), lambda b, j: (b, 0, 0)),
                   pl.BlockSpec((None, 8, 128), lambda b, j: (b, 0, 0))],
        out_shape=[jax.ShapeDtypeStruct((bsz, t, D_A), F32), jax.ShapeDtypeStruct((bsz, HA, DH, DH), F32),
                   jax.ShapeDtypeStruct((bsz, HA, DH), F32), jax.ShapeDtypeStruct((bsz, 8, 128), F32)],
        scratch_shapes=[pltpu.VMEM((tb + 8, 2 * D_A), F32), pltpu.VMEM((tb, D_A), BF16),
                        pltpu.VMEM((tb, D_A), BF16), pltpu.VMEM((HA, DH, 2 * DH), F32),
                        pltpu.VMEM((8, 128), F32)],
        compiler_params=pltpu.CompilerParams(dimension_semantics=("parallel", "arbitrary"),
                                             vmem_limit_bytes=VMEM_LIMIT_BYTES),
        name="mlstm",
    )(qk_a, v_a, gates, grow, conv0p, c0, n0, m0p, conv_w, conv_b.reshape(1, 2 * D_A), g_head_a)


def _attn_kernel(qt_ref, k_ref, vt_ref, bd_ref, bp_ref, ghb_ref, lam_ref, o_ref,
                 qpad_sc, m_sc, l_sc, acc_sc, *, tq, tk, n_qblk):
    i = pl.program_id(1)
    lv = lam_ref[...]
    lam = (jnp.exp(jnp.sum(lv[0:1, :] * lv[1:2, :], axis=1, keepdims=True))
           - jnp.exp(jnp.sum(lv[2:3, :] * lv[3:4, :], axis=1, keepdims=True)) + LAMBDA_INIT)

    first_map = lax.broadcasted_iota(jnp.int32, (2 * DK, tq), 0) < DK
    for h in range(HB):
        qh = qt_ref[h * 2 * DK:(h + 1) * 2 * DK, :]
        zero = jnp.zeros_like(qh)
        qpad_sc[h, :, 0:tq] = jnp.where(first_map, qh, zero)
        qpad_sc[h, :, tq:2 * tq] = jnp.where(first_map, zero, qh)
    m_sc[...] = jnp.full(m_sc.shape, NEG, F32)
    l_sc[...] = jnp.zeros(l_sc.shape, F32)
    acc_sc[...] = jnp.zeros(acc_sc.shape, F32)

    def block(jb, bias_ref):
        row0 = pl.multiple_of(jb * tk, tk)
        for h in range(HB):
            kh = k_ref[pl.ds(row0, tk), h * 2 * DK:(h + 1) * 2 * DK]
            s = _mm(kh, qpad_sc[h])
            if bias_ref is not None:
                bias = bias_ref[h]
                s = s + jnp.concatenate([bias, bias], axis=1)
            m_old = m_sc[h]
            m_new = jnp.maximum(m_old, jnp.max(s, axis=0, keepdims=True))
            alpha = jnp.exp2(m_old - m_new)
            p = jnp.exp2(s - m_new)
            l_sc[h] = alpha * l_sc[h] + jnp.sum(p, axis=0, keepdims=True)
            acc_sc[h] = alpha * acc_sc[h] + _mm(vt_ref[h, jb], p.astype(BF16))
            m_sc[h] = m_new

    if n_qblk > 1:
        def far(jb, carry):
            block(jb, None)
            return carry
        lax.fori_loop(0, jnp.maximum(i - 1, 0), far, 0)

        @pl.when(i >= 1)
        def _():
            block(i - 1, bp_ref)
    block(i, bd_ref)

    for h in range(HB):
        on = acc_sc[h] / l_sc[h]
        o = on[:, 0:tq] - lam * on[:, tq:2 * tq]
        o = o * lax.rsqrt(jnp.mean(o * o, axis=0, keepdims=True) + EPS) * (ghb_ref[...] * (1.0 - LAMBDA_INIT))
        o_ref[:, h * DV:(h + 1) * DV] = o.T.astype(o_ref.dtype)


def _bucket(rel):
    half, max_exact = NUM_BUCKETS // 2, NUM_BUCKETS // 4
    n = np.abs(rel).astype(np.int64)
    sq = np.maximum(n * n // (max_exact * max_exact), 1)
    large = max_exact + np.floor(np.log2(sq.astype(np.float64)) + 1e-9).astype(np.int64)
    large = np.minimum(large, half - 1)
    return np.where(rel > 0, half, 0) + np.where(n < max_exact, n, large)


def _bias_tile(rel_bias, rel, allowed):
    table = (rel_bias - rel_bias[NUM_BUCKETS // 2 - 1:NUM_BUCKETS // 2, :]) * LOG2E
    tile = jnp.transpose(table[_bucket(rel)], (2, 0, 1))
    return jnp.where(jnp.asarray(allowed)[None], tile, NEG)


def _attn(qt_blk, k16, vt_blk, bias_diag, bias_prev, g_head_b, lam_vecs, bsz, t, tq, tk):
    n_qblk = t // tq
    nk = k16.shape[1]
    n_kblk = nk // tk
    kern = functools.partial(_attn_kernel, tq=tq, tk=tk, n_qblk=n_qblk)
    return pl.pallas_call(
        kern,
        grid=(bsz, n_qblk),
        in_specs=[pl.BlockSpec((None, D_QB, tq), lambda b, i: (b * n_qblk + i, 0, 0)),
                  pl.BlockSpec((None, nk, D_QB), lambda b, i: (b, 0, 0), pipeline_mode=pl.Buffered(1)),
                  pl.BlockSpec((HB, n_kblk, DV, tk), lambda b, i: (0, b, 0, 0), pipeline_mode=pl.Buffered(1)),
                  _const_spec(bias_diag.shape), _const_spec(bias_prev.shape),
                  _const_spec((DV, 1)), _const_spec((4, DK))],
        out_specs=pl.BlockSpec((None, tq, D_VB), lambda b, i: (b, i, 0)),
        out_shape=jax.ShapeDtypeStruct((bsz, t, D_VB), BF16),
        scratch_shapes=[pltpu.VMEM((HB, 2 * DK, 2 * tq), BF16), pltpu.VMEM((HB, 1, 2 * tq), F32),
                        pltpu.VMEM((HB, 1, 2 * tq), F32), pltpu.VMEM((HB, DV, 2 * tq), F32)],
        compiler_params=pltpu.CompilerParams(dimension_semantics=("parallel", "arbitrary"),
                                             vmem_limit_bytes=VMEM_LIMIT_BYTES),
        name="attn",
    )(qt_blk, k16, vt_blk, bias_diag, bias_prev, g_head_b.reshape(DV, 1), lam_vecs)


def _out_kernel(x_ref, hn_ref, ob_ref, gpre_ref, gpost_ref, gpref_ref, gpostf_ref, woa_ref, boa_ref,
                wga_ref, bga_ref, wgb_ref, bgb_ref, wpa_ref, wpb_ref, wout_ref, wff1_ref, wff2_ref, o_ref):
    x = x_ref[...]
    u = _rms(x, gpre_ref[...]).astype(BF16)
    h_a = (hn_ref[...] * jax.nn.sigmoid(_mm(u, woa_ref[...]) + boa_ref[...])).astype(BF16)
    y_a = _mm(h_a, wpa_ref[...])
    y_b = _mm(ob_ref[...], wpb_ref[...])
    gate_a = jax.nn.sigmoid(_mm(u, wga_ref[...]) + bga_ref[...])
    gate_b = jax.nn.sigmoid(_mm(u, wgb_ref[...]) + bgb_ref[...])
    mix = _mm((gate_a * y_a + gate_b * y_b).astype(BF16), wout_ref[...])
    x1 = x + _rms(mix, gpost_ref[...])
    f = _rms(x1, gpref_ref[...]).astype(BF16)
    hf = jnp.zeros_like(x1)
    for c in range(FF // FF_CHUNK):
        mid = jnp.maximum(_mm(f, wff1_ref[:, c * FF_CHUNK:(c + 1) * FF_CHUNK]), 0.0)
        hf = hf + _mm((mid * mid).astype(BF16), wff2_ref[c * FF_CHUNK:(c + 1) * FF_CHUNK, :])
    o_ref[...] = x1 + _rms(hf, gpostf_ref[...])


def _out(x2d, hn2d, ob2d, g_pre_mix, g_post_mix, g_pre_ffn, g_post_ffn, w_in, b_in, w_pa, w_pb, w_out,
         w_ff1, w_ff2):
    rows = x2d.shape[0]
    tm = min(rows, OUT_ROW_TILE)
    o_oa = 3 * D_A
    o_ga = 4 * D_A + 2 * HA + 2 * D_QB + D_VB
    o_gb = o_ga + D_MODEL

    def cols(lo, n):
        return w_in[:, lo:lo + n].astype(BF16), b_in[lo:lo + n].reshape(1, n)

    woa, boa = cols(o_oa, D_A)
    wga, bga = cols(o_ga, D_MODEL)
    wgb, bgb = cols(o_gb, D_MODEL)
    consts = (g_pre_mix.reshape(1, -1), g_post_mix.reshape(1, -1), g_pre_ffn.reshape(1, -1),
              g_post_ffn.reshape(1, -1), woa, boa, wga, bga, wgb, bgb, w_pa.astype(BF16), w_pb.astype(BF16),
              w_out.astype(BF16), w_ff1.astype(BF16), w_ff2.astype(BF16))

    def row_spec(n):
        return pl.BlockSpec((tm, n), lambda i: (i, 0))

    return pl.pallas_call(
        _out_kernel,
        grid=(rows // tm,),
        in_specs=[row_spec(D_MODEL), row_spec(D_A), row_spec(D_VB)] + [_const_spec(c.shape) for c in consts],
        out_specs=row_spec(D_MODEL),
        out_shape=jax.ShapeDtypeStruct((rows, D_MODEL), F32),
        compiler_params=pltpu.CompilerParams(dimension_semantics=("parallel",),
                                             vmem_limit_bytes=VMEM_LIMIT_BYTES),
        name="out",
    )(x2d, hn2d, ob2d, *consts)


def _layer(x, k_past, v_past, c0, n0, m0, conv0, rel_bias, g_pre_mix, g_post_mix, g_pre_ffn, g_post_ffn,
           w_in, b_in, conv_w, conv_b, g_head_a, w_pa, lam_vecs, g_head_b, w_pb, w_out, w_ff1, w_ff2):
    bsz, t, _ = x.shape
    past = k_past.shape[1]
    x2d = x.reshape(bsz * t, D_MODEL)
    qk_a, v_a, gates, k_new, k16, v_new, qt_blk, vt_blk = _proj(x2d, g_pre_mix, w_in, b_in)

    hn, c_new, n_new, m_new = _mlstm(qk_a.reshape(bsz, t, 2 * D_A), v_a.reshape(bsz, t, D_A),
                                     gates.reshape(bsz, t, 2 * HA), conv0, c0, n0, m0, conv_w, conv_b, g_head_a)

    if past == 0:
        tq = tk = min(t, ROW_TILE)
        s_idx = np.arange(tk)[:, None]
        t_idx = np.arange(tq)[None, :]
        bias_diag = _bias_tile(rel_bias, s_idx - t_idx, (s_idx // CHUNK) <= (t_idx // CHUNK))
        bias_prev = _bias_tile(rel_bias, s_idx - t_idx - tk, np.ones((tk, tq), bool))
        k16 = k16.reshape(bsz, t, D_QB)
    else:
        tq = t
        nk = past + t
        tk = -(-nk // 128) * 128
        k16 = jnp.concatenate([k_past.reshape(bsz, past, D_QB).astype(BF16), k16.reshape(bsz, t, D_QB),
                               jnp.zeros((bsz, tk - nk, D_QB), BF16)], axis=1)
        vt_past = jnp.transpose(v_past.astype(BF16), (2, 0, 3, 1))
        vt_new = jnp.transpose(vt_blk.reshape(HB, DV, bsz, t), (0, 2, 1, 3))
        vt_blk = jnp.concatenate([vt_past, vt_new, jnp.zeros((HB, bsz, DV, tk - nk), BF16)], axis=3)
        qt_blk = jnp.transpose(qt_blk.reshape(D_QB, bsz, t), (1, 0, 2))
        s_idx = np.arange(tk)[:, None]
        t_idx = np.arange(tq)[None, :]
        allowed = (s_idx < nk) & ((s_idx // CHUNK) <= ((past + t_idx) // CHUNK))
        bias_diag = _bias_tile(rel_bias, s_idx - (past + t_idx), allowed)
        bias_prev = bias_diag
    o_b = _attn(qt_blk, k16, vt_blk, bias_diag, bias_prev, g_head_b, lam_vecs, bsz, t, tq, tk)

    y = _out(x2d, hn.reshape(bsz * t, D_A), o_b.reshape(bsz * t, D_VB), g_pre_mix, g_post_mix, g_pre_ffn,
             g_post_ffn, w_in, b_in, w_pa, w_pb, w_out, w_ff1, w_ff2)
    return (y.reshape(bsz, t, D_MODEL), k_new.reshape(bsz, t, HB, 2 * DK), v_new.reshape(bsz, t, HB, DV),
            c_new, n_new, m_new[:, 0, :HA], qk_a.reshape(bsz, t, 2 * D_A)[:, t - (CONV_W - 1):, :])


def kernel(x_prompt, x_sample, cache_k, cache_v, state_C, state_n, state_m, state_conv, rel_bias, g_pre_mix,
           g_post_mix, g_pre_ffn, g_post_ffn, w_in, b_in, conv_w, conv_b, g_head_a, w_pa, lambda_q1, lambda_k1,
           lambda_q2, lambda_k2, g_head_b, w_pb, w_out, w_ff1, w_ff2):
    lam_vecs = jnp.concatenate([lambda_q1, lambda_k1, lambda_q2, lambda_k2], axis=0)
    weights = (rel_bias, g_pre_mix[0], g_post_mix[0], g_pre_ffn[0], g_post_ffn[0], w_in[0], b_in[0], conv_w[0],
               conv_b[0], g_head_a[0], w_pa[0], lam_vecs, g_head_b[0], w_pb[0], w_out[0], w_ff1[0], w_ff2[0])
    bp = x_prompt.shape[0]
    prompt = _layer(x_prompt, jnp.zeros((bp, 0, HB, 2 * DK), F32), jnp.zeros((bp, 0, HB, DV), F32),
                    jnp.zeros((bp, HA, DH, DH), F32), jnp.zeros((bp, HA, DH), F32), jnp.zeros((bp, HA), F32),
                    jnp.zeros((bp, CONV_W - 1, 2 * D_A), F32), *weights)
    sample = _layer(x_sample, cache_k[0], cache_v[0], state_C[0], state_n[0], state_m[0], state_conv[0], *weights)
    yp, ys = prompt[0], sample[0]
    return (yp, ys) + tuple(a[None] for a in prompt[1:]) + tuple(a[None] for a in sample[1:])
```

```python
import functools
import math

import numpy as np
import jax
import jax.numpy as jnp
from jax import lax
from jax.experimental import pallas as pl
from jax.experimental.pallas import tpu as pltpu

F32 = jnp.float32
BF16 = jnp.bfloat16

D_MODEL = 1024
HA = 4
DH = 128
D_A = HA * DH
HB = 4
DK = 64
DV = 2 * DK
D_QB = HB * 2 * DK
D_VB = HB * DV
CONV_W = 4
FF = 4 * D_MODEL
NUM_BUCKETS = 32
CHUNK = 64
EPS = 1e-6
LAMBDA_INIT = 0.8 - 0.6 * math.exp(-0.3 * 0)
NEG = -1e30
LOG2E = math.log2(math.e)
QSCALE = DK ** -0.5 * LOG2E
N_GATE_PAD = 128
ROW_TILE = 512
KEY_TILE = 256
OUT_ROW_TILE = 256
FF_CHUNK = 1024
VMEM_LIMIT_BYTES = 56 * 1024 * 1024

NT_DIMS = (((1,), (1,)), ((), ()))
TN_DIMS = (((0,), (0,)), ((), ()))


def _const_spec(shape):
    zeros = (0,) * len(shape)
    return pl.BlockSpec(shape, lambda *_: zeros, pipeline_mode=pl.Buffered(1))


def _rms(x, g):
    return x * lax.rsqrt(jnp.mean(x * x, axis=-1, keepdims=True) + EPS) * g


def _log_sigmoid(x):
    return -(jnp.maximum(-x, 0.0) + jnp.log(1.0 + jnp.exp(-jnp.abs(x))))


def _split3(x):
    x1 = x.astype(BF16)
    r = x - x1.astype(F32)
    x2 = r.astype(BF16)
    r = r - x2.astype(F32)
    return x1, x2, r.astype(BF16)


def _mm(a, b):
    return jnp.dot(a, b, preferred_element_type=F32)


def _proj_kernel(x_ref, g_ref, wqk_ref, bqk_ref, wva_ref, bva_ref, wg_ref, bg_ref, wk_ref, bk_ref,
                 wv_ref, bv_ref, wqt_ref, bqt_ref, wvt_ref, bvt_ref,
                 qk_ref, va_ref, gate_ref, k_ref, k16_ref, v_ref, qt_ref, vt_ref):
    u = _rms(x_ref[...], g_ref[...]).astype(BF16)
    qk_ref[...] = _mm(u, wqk_ref[...]) + bqk_ref[...]
    va_ref[...] = _mm(u, wva_ref[...]) + bva_ref[...]
    gate_ref[...] = (_mm(u, wg_ref[...]) + bg_ref[...])[:, :2 * HA]
    k = _mm(u, wk_ref[...]) + bk_ref[...]
    k_ref[...] = k
    k16 = k.astype(BF16)
    for h in range(HB):
        k16_ref[h] = k16[:, h * 2 * DK:(h + 1) * 2 * DK]
    v_ref[...] = _mm(u, wv_ref[...]) + bv_ref[...]
    qt = lax.dot_general(wqt_ref[...], u, NT_DIMS, preferred_element_type=F32) + bqt_ref[...]
    qt_ref[...] = (qt * QSCALE).astype(BF16)
    vt = lax.dot_general(wvt_ref[...], u, NT_DIMS, preferred_element_type=F32) + bvt_ref[...]
    vt = vt.astype(BF16)
    tk = vt_ref.shape[-1]
    for h in range(HB):
        for c in range(vt_ref.shape[1]):
            vt_ref[h, c] = vt[h * DV:(h + 1) * DV, c * tk:(c + 1) * tk]


def _proj(x2d, g_pre, w_in, b_in):
    rows = x2d.shape[0]
    tm = min(rows, ROW_TILE)
    nblk = rows // tm
    tk = min(tm, KEY_TILE)
    o_qk, o_va, o_oa, o_i, o_q, o_k, o_v, o_ga = 0, 2 * D_A, 3 * D_A, 4 * D_A, 4 * D_A + 2 * HA, \
        4 * D_A + 2 * HA + D_QB, 4 * D_A + 2 * HA + 2 * D_QB, 4 * D_A + 2 * HA + 2 * D_QB + D_VB
    del o_oa, o_ga

    def cols(lo, n):
        return w_in[:, lo:lo + n].astype(BF16), b_in[lo:lo + n].reshape(1, n)

    wqk, bqk = cols(o_qk, 2 * D_A)
    wva, bva = cols(o_va, D_A)
    wg = jnp.pad(w_in[:, o_i:o_i + 2 * HA], ((0, 0), (0, N_GATE_PAD - 2 * HA))).astype(BF16)
    bg = jnp.pad(b_in[o_i:o_i + 2 * HA], (0, N_GATE_PAD - 2 * HA)).reshape(1, N_GATE_PAD)
    wk, bk = cols(o_k, D_QB)
    wv, bv = cols(o_v, D_VB)
    wqt = w_in[:, o_q:o_q + D_QB].T.astype(BF16)
    bqt = b_in[o_q:o_q + D_QB].reshape(D_QB, 1)
    wvt = w_in[:, o_v:o_v + D_VB].T.astype(BF16)
    bvt = b_in[o_v:o_v + D_VB].reshape(D_VB, 1)
    consts = (g_pre.reshape(1, D_MODEL), wqk, bqk, wva, bva, wg, bg, wk, bk, wv, bv, wqt, bqt, wvt, bvt)

    def row_spec(n):
        return pl.BlockSpec((tm, n), lambda i: (i, 0))

    return pl.pallas_call(
        _proj_kernel,
        grid=(nblk,),
        in_specs=[row_spec(D_MODEL)] + [_const_spec(c.shape) for c in consts],
        out_specs=[row_spec(2 * D_A), row_spec(D_A), row_spec(2 * HA), row_spec(D_QB),
                   pl.BlockSpec((HB, tm, 2 * DK), lambda i: (0, i, 0)), row_spec(D_VB),
                   pl.BlockSpec((None, D_QB, tm), lambda i: (i, 0, 0)),
                   pl.BlockSpec((HB, tm // tk, DV, tk), lambda i: (0, i, 0, 0))],
        out_shape=[jax.ShapeDtypeStruct((rows, 2 * D_A), F32), jax.ShapeDtypeStruct((rows, D_A), F32),
                   jax.ShapeDtypeStruct((rows, 2 * HA), F32), jax.ShapeDtypeStruct((rows, D_QB), F32),
                   jax.ShapeDtypeStruct((HB, rows, 2 * DK), BF16), jax.ShapeDtypeStruct((rows, D_VB), F32),
                   jax.ShapeDtypeStruct((nblk, D_QB, tm), BF16),
                   jax.ShapeDtypeStruct((HB, rows // tk, DV, tk), BF16)],
        compiler_params=pltpu.CompilerParams(dimension_semantics=("parallel",),
                                             vmem_limit_bytes=VMEM_LIMIT_BYTES),
        name="proj",
    )(x2d, *consts)


def _mlstm_kernel(qk_ref, va_ref, gcol_ref, grow_ref, conv0_ref, c0_ref, n0_ref, m0_ref, cw_ref, cb_ref,
                  gha_ref, hn_ref, c_ref, n_ref, m_ref, xp_sc, q_sc, k_sc, st_sc, m_sc, *, chunk, n_chunks):
    j = pl.program_id(1)
    tb = chunk * n_chunks

    @pl.when(j == 0)
    def _():
        xp_sc[0:8, :] = conv0_ref[...]
        for h in range(HA):
            st_sc[h, :, 0:DH] = c0_ref[h].T
            st_sc[h, :, DH:2 * DH] = jnp.broadcast_to(n0_ref[h:h + 1, :], (DH, DH)).T
        m_sc[...] = m0_ref[...]

    @pl.when(j > 0)
    def _():
        xp_sc[0:8, :] = xp_sc[tb:tb + 8, :]

    xp_sc[8:8 + tb, :] = qk_ref[...]
    cw = cw_ref[...]
    conv = cb_ref[...]
    for t in range(CONV_W):
        conv = conv + xp_sc[8 - (CONV_W - 1) + t:8 - (CONV_W - 1) + t + tb, :] * cw[t:t + 1, :]
    act = conv * jax.nn.sigmoid(conv)
    q_sc[...] = act[:, :D_A].astype(BF16)
    k_sc[...] = (act[:, D_A:] * DH ** -0.5).astype(BF16)

    rr = lax.broadcasted_iota(jnp.int32, (chunk, chunk), 0)
    cc = lax.broadcasted_iota(jnp.int32, (chunk, chunk), 1)
    tril = cc <= rr
    ltri = tril.astype(BF16)
    utri = (rr <= cc).astype(BF16)
    ones = jnp.ones((chunk, DH), F32)

    for c in range(n_chunks):
        sl = slice(c * chunk, (c + 1) * chunk)
        gc = gcol_ref[sl, :]
        ig_c = gc[:, 0:HA]
        b_c = sum(_mm(ltri, p) for p in _split3(_log_sigmoid(gc[:, HA:2 * HA])))
        g = b_c[chunk - 1:chunk, :]
        ls_c = g - b_c + ig_c
        m_old = m_sc[0:1, 0:HA]
        m_new = jnp.maximum(g + m_old, jnp.max(ls_c, axis=0, keepdims=True))
        ws_c = jnp.exp(ls_c - m_new)
        decay = jnp.exp(g + m_old - m_new)
        a_c = b_c + m_old
        m_sc[0:1, 0:HA] = m_new
        gr = grow_ref[c]
        b_r = sum(_mm(p, utri) for p in _split3(_log_sigmoid(gr[HA:2 * HA, :])))
        c_r = gr[0:HA, :] - b_r
        for h in range(HA):
            hs = slice(h * DH, (h + 1) * DH)
            dlog = jnp.where(tril, b_c[:, h:h + 1] + c_r[h:h + 1, :], NEG)
            a_h = a_c[:, h:h + 1]
            m_t = jnp.maximum(a_h, jnp.max(dlog, axis=1, keepdims=True))
            p = jnp.exp(dlog - m_t)
            w_int = jnp.exp(a_h - m_t)
            q_h = q_sc[sl, hs]
            k_h = k_sc[sl, hs]
            vaug = jnp.concatenate([va_ref[sl, hs], ones], axis=1)
            s = (p * lax.dot_general(q_h, k_h, NT_DIMS, preferred_element_type=F32)).astype(BF16)
            st = st_sc[h]
            acc = w_int * _mm(q_h, st.astype(BF16)) + _mm(s, vaug.astype(BF16))
            hh = acc[:, :DH] / jnp.maximum(jnp.abs(acc[:, DH:]), jnp.exp(-m_t))
            hn_ref[sl, hs] = _rms(hh, gha_ref[h:h + 1, :])
            wv = (ws_c[:, h:h + 1] * vaug).astype(BF16)
            st_sc[h] = decay[:, h:h + 1] * st + lax.dot_general(k_h, wv, TN_DIMS, preferred_element_type=F32)

    @pl.when(j == pl.num_programs(1) - 1)
    def _():
        for h in range(HA):
            st = st_sc[h]
            c_ref[h] = st[:, :DH].T
            n_ref[h:h + 1, :] = st[:, DH:].T[0:1, :]
        m_ref[...] = m_sc[...]


def _mlstm(qk_a, v_a, gates, conv0, c0, n0, m0, conv_w, conv_b, g_head_a):
    bsz, t, _ = qk_a.shape
    chunk = min(t, CHUNK)
    n_chunks = min(t // chunk, ROW_TILE // chunk)
    tb = chunk * n_chunks
    nblk = t // tb
    grow = gates.reshape(bsz, t // chunk, chunk, 2 * HA).transpose(0, 1, 3, 2)
    conv0p = jnp.pad(conv0, ((0, 0), (8 - (CONV_W - 1), 0), (0, 0)))
    m0p = jnp.pad(m0, ((0, 0), (0, 128 - HA)))[:, None, :] * jnp.ones((1, 8, 1), F32)
    kern = functools.partial(_mlstm_kernel, chunk=chunk, n_chunks=n_chunks)
    return pl.pallas_call(
        kern,
        grid=(bsz, nblk),
        in_specs=[pl.BlockSpec((None, tb, 2 * D_A), lambda b, j: (b, j, 0)),
                  pl.BlockSpec((None, tb, D_A), lambda b, j: (b, j, 0)),
                  pl.BlockSpec((None, tb, 2 * HA), lambda b, j: (b, j, 0)),
                  pl.BlockSpec((None, n_chunks, 2 * HA, chunk), lambda b, j: (b, j, 0, 0)),
                  pl.BlockSpec((None, 8, 2 * D_A), lambda b, j: (b, 0, 0)),
                  pl.BlockSpec((None, HA, DH, DH), lambda b, j: (b, 0, 0, 0)),
                  pl.BlockSpec((None, HA, DH), lambda b, j: (b, 0, 0)),
                  pl.BlockSpec((None, 8, 128), lambda b, j: (b, 0, 0)),
                  _const_spec((CONV_W, 2 * D_A)), _const_spec((1, 2 * D_A)), _const_spec((HA, DH))],
        out_specs=[pl.BlockSpec((None, tb, D_A), lambda b, j: (b, j, 0)),
                   pl.BlockSpec((None, HA, DH, DH), lambda b, j: (b, 0, 0, 0)),
                   pl.BlockSpec((None, HA, DH), lambda b, j: (b, 0, 0)),
                   pl.BlockSpec((None, 8, 128), lambda b, j: (b, 0, 0))],
        out_shape=[jax.ShapeDtypeStruct((bsz, t, D_A), F32), jax.ShapeDtypeStruct((bsz, HA, DH, DH), F32),
                   jax.ShapeDtypeStruct((bsz, HA, DH), F32), jax.ShapeDtypeStruct((bsz, 8, 128), F32)],
        scratch_shapes=[pltpu.VMEM((tb + 8, 2 * D_A), F32), pltpu.VMEM((tb, D_A), BF16),
                        pltpu.VMEM((tb, D_A), BF16), pltpu.VMEM((HA, DH, 2 * DH), F32),
                        pltpu.VMEM((8, 128), F32)],
        compiler_params=pltpu.CompilerParams(dimension_semantics=("parallel", "arbitrary"),
                                             vmem_limit_bytes=VMEM_LIMIT_BYTES),
        name="mlstm",
    )(qk_a, v_a, gates, grow, conv0p, c0, n0, m0p, conv_w, conv_b.reshape(1, 2 * D_A), g_head_a)


def _attn_kernel(qt_ref, k_ref, vt_ref, bd_ref, bp_ref, ghb_ref, lam_ref, o_ref,
                 qpad_sc, m_sc, l_sc, acc_sc, s_sc, mx_sc, *, tq, tk, n_qblk, n_diag):
    i = pl.program_id(1)
    lv = lam_ref[...]
    lam = (jnp.exp(jnp.sum(lv[0:1, :] * lv[1:2, :], axis=1, keepdims=True))
           - jnp.exp(jnp.sum(lv[2:3, :] * lv[3:4, :], axis=1, keepdims=True)) + LAMBDA_INIT)

    first_map = lax.broadcasted_iota(jnp.int32, (2 * DK, tq), 0) < DK
    for h in range(HB):
        qh = qt_ref[h * 2 * DK:(h + 1) * 2 * DK, :]
        zero = jnp.zeros_like(qh)
        qpad_sc[h, :, 0:tq] = jnp.where(first_map, qh, zero)
        qpad_sc[h, :, tq:2 * tq] = jnp.where(first_map, zero, qh)
    m_sc[...] = jnp.full(m_sc.shape, NEG, F32)
    l_sc[...] = jnp.zeros(l_sc.shape, F32)
    acc_sc[...] = jnp.zeros(acc_sc.shape, F32)

    def logits(jb, h, bias):
        row0 = pl.multiple_of(jb * tk, tk)
        s = _mm(k_ref[h, pl.ds(row0, tk), :], qpad_sc[h])
        if bias is not None:
            s = s + jnp.concatenate([bias, bias], axis=1)
        return s

    def accumulate(jb, h, s, mx):
        m_old = m_sc[h]
        m_new = jnp.maximum(m_old, mx)
        alpha = jnp.exp2(m_old - m_new)
        p = jnp.exp2(s - m_new)
        l_sc[h] = alpha * l_sc[h] + jnp.sum(p, axis=0, keepdims=True)
        acc_sc[h] = alpha * acc_sc[h] + _mm(vt_ref[h, jb], p.astype(BF16))
        m_sc[h] = m_new

    def stage1(jb, pair, slot):
        for hh in range(2):
            s = logits(jb, 2 * pair + hh, None)
            s_sc[slot, hh] = s
            mx_sc[slot, hh] = jnp.max(s, axis=0, keepdims=True)

    def stage2(jb, pair, slot):
        for hh in range(2):
            accumulate(jb, 2 * pair + hh, s_sc[slot, hh], mx_sc[slot, hh])

    def block(jb, bias_ref):
        for h in range(HB):
            s = logits(jb, h, bias_ref[h])
            accumulate(jb, h, s, jnp.max(s, axis=0, keepdims=True))

    first_diag = i * n_diag
    if n_qblk > 1:
        n_far = jnp.maximum(first_diag - 1, 0)

        @pl.when(n_far > 0)
        def _():
            stage1(0, 0, 0)

        def far(jb, carry):
            stage1(jb, 1, 1)
            stage2(jb, 0, 0)
            stage1(jb + 1, 0, 0)
            stage2(jb, 1, 1)
            return carry
        lax.fori_loop(0, n_far, far, 0)

        @pl.when(i >= 1)
        def _():
            block(first_diag - 1, bp_ref)
    for d in range(n_diag):
        block(first_diag + d, bd_ref.at[d])

    for h in range(HB):
        on = acc_sc[h] / l_sc[h]
        o = on[:, 0:tq] - lam * on[:, tq:2 * tq]
        o = o * lax.rsqrt(jnp.mean(o * o, axis=0, keepdims=True) + EPS) * (ghb_ref[...] * (1.0 - LAMBDA_INIT))
        o_ref[:, h * DV:(h + 1) * DV] = o.T.astype(o_ref.dtype)


def _bucket(rel):
    half, max_exact = NUM_BUCKETS // 2, NUM_BUCKETS // 4
    n = np.abs(rel).astype(np.int64)
    sq = np.maximum(n * n // (max_exact * max_exact), 1)
    large = max_exact + np.floor(np.log2(sq.astype(np.float64)) + 1e-9).astype(np.int64)
    large = np.minimum(large, half - 1)
    return np.where(rel > 0, half, 0) + np.where(n < max_exact, n, large)


def _bias_tiles(rel_bias, rel0, nk, tq):
    table = (rel_bias - rel_bias[NUM_BUCKETS // 2 - 1:NUM_BUCKETS // 2, :]) * LOG2E
    length = nk + tq - 1
    onehot = np.eye(NUM_BUCKETS, dtype=bool)[_bucket(rel0 + (nk - 1) - np.arange(length))]
    vec = jnp.sum(jnp.where(onehot[:, :, None], table[None], 0.0), axis=1).T
    a = jnp.broadcast_to(vec[:, None, :], (HB, nk, length))
    a = jnp.pad(a, ((0, 0), (0, 0), (0, 1))).reshape(HB, nk * (length + 1))[:, :nk * length]
    return a.reshape(HB, nk, length)[:, :, nk - 1:nk - 1 + tq]


def _attn(qt_blk, k16, vt_blk, bias_diag, bias_prev, g_head_b, lam_vecs, bsz, t, tq, tk):
    n_qblk = t // tq
    nk = k16.shape[2]
    n_kblk = nk // tk
    n_diag = bias_diag.shape[0]
    kern = functools.partial(_attn_kernel, tq=tq, tk=tk, n_qblk=n_qblk, n_diag=n_diag)
    return pl.pallas_call(
        kern,
        grid=(bsz, n_qblk),
        in_specs=[pl.BlockSpec((None, D_QB, tq), lambda b, i: (b * n_qblk + i, 0, 0)),
                  pl.BlockSpec((HB, None, nk, 2 * DK), lambda b, i: (0, b, 0, 0), pipeline_mode=pl.Buffered(1)),
                  pl.BlockSpec((HB, n_kblk, DV, tk), lambda b, i: (0, b, 0, 0), pipeline_mode=pl.Buffered(1)),
                  _const_spec(bias_diag.shape), _const_spec(bias_prev.shape),
                  _const_spec((DV, 1)), _const_spec((4, DK))],
        out_specs=pl.BlockSpec((None, tq, D_VB), lambda b, i: (b, i, 0)),
        out_shape=jax.ShapeDtypeStruct((bsz, t, D_VB), BF16),
        scratch_shapes=[pltpu.VMEM((HB, 2 * DK, 2 * tq), BF16), pltpu.VMEM((HB, 1, 2 * tq), F32),
                        pltpu.VMEM((HB, 1, 2 * tq), F32), pltpu.VMEM((HB, DV, 2 * tq), F32),
                        pltpu.VMEM((2, 2, tk, 2 * tq), F32), pltpu.VMEM((2, 2, 1, 2 * tq), F32)],
        compiler_params=pltpu.CompilerParams(dimension_semantics=("parallel", "arbitrary"),
                                             vmem_limit_bytes=VMEM_LIMIT_BYTES),
        name="attn",
    )(qt_blk, k16, vt_blk, bias_diag, bias_prev, g_head_b.reshape(DV, 1), lam_vecs)


def _out_kernel(x_ref, hn_ref, ob_ref, gpre_ref, gpost_ref, gpref_ref, gpostf_ref, woa_ref, boa_ref,
                wga_ref, bga_ref, wgb_ref, bgb_ref, wpa_ref, wpb_ref, wout_ref, wff1_ref, wff2_ref, o_ref):
    x = x_ref[...]
    u = _rms(x, gpre_ref[...]).astype(BF16)
    h_a = (hn_ref[...] * jax.nn.sigmoid(_mm(u, woa_ref[...]) + boa_ref[...])).astype(BF16)
    y_a = _mm(h_a, wpa_ref[...])
    y_b = _mm(ob_ref[...], wpb_ref[...])
    gate_a = jax.nn.sigmoid(_mm(u, wga_ref[...]) + bga_ref[...])
    gate_b = jax.nn.sigmoid(_mm(u, wgb_ref[...]) + bgb_ref[...])
    mix = _mm((gate_a * y_a + gate_b * y_b).astype(BF16), wout_ref[...])
    x1 = x + _rms(mix, gpost_ref[...])
    f = _rms(x1, gpref_ref[...]).astype(BF16)
    hf = jnp.zeros_like(x1)
    for c in range(FF // FF_CHUNK):
        mid = jnp.maximum(_mm(f, wff1_ref[:, c * FF_CHUNK:(c + 1) * FF_CHUNK]), 0.0)
        hf = hf + _mm((mid * mid).astype(BF16), wff2_ref[c * FF_CHUNK:(c + 1) * FF_CHUNK, :])
    o_ref[...] = x1 + _rms(hf, gpostf_ref[...])


def _out(x2d, hn2d, ob2d, g_pre_mix, g_post_mix, g_pre_ffn, g_post_ffn, w_in, b_in, w_pa, w_pb, w_out,
         w_ff1, w_ff2):
    rows = x2d.shape[0]
    tm = min(rows, OUT_ROW_TILE)
    o_oa = 3 * D_A
    o_ga = 4 * D_A + 2 * HA + 2 * D_QB + D_VB
    o_gb = o_ga + D_MODEL

    def cols(lo, n):
        return w_in[:, lo:lo + n].astype(BF16), b_in[lo:lo + n].reshape(1, n)

    woa, boa = cols(o_oa, D_A)
    wga, bga = cols(o_ga, D_MODEL)
    wgb, bgb = cols(o_gb, D_MODEL)
    consts = (g_pre_mix.reshape(1, -1), g_post_mix.reshape(1, -1), g_pre_ffn.reshape(1, -1),
              g_post_ffn.reshape(1, -1), woa, boa, wga, bga, wgb, bgb, w_pa.astype(BF16), w_pb.astype(BF16),
              w_out.astype(BF16), w_ff1.astype(BF16), w_ff2.astype(BF16))

    def row_spec(n):
        return pl.BlockSpec((tm, n), lambda i: (i, 0))

    return pl.pallas_call(
        _out_kernel,
        grid=(rows // tm,),
        in_specs=[row_spec(D_MODEL), row_spec(D_A), row_spec(D_VB)] + [_const_spec(c.shape) for c in consts],
        out_specs=row_spec(D_MODEL),
        out_shape=jax.ShapeDtypeStruct((rows, D_MODEL), F32),
        compiler_params=pltpu.CompilerParams(dimension_semantics=("parallel",),
                                             vmem_limit_bytes=VMEM_LIMIT_BYTES),
        name="out",
    )(x2d, hn2d, ob2d, *consts)


def _layer(x, k_past, v_past, c0, n0, m0, conv0, rel_bias, g_pre_mix, g_post_mix, g_pre_ffn, g_post_ffn,
           w_in, b_in, conv_w, conv_b, g_head_a, w_pa, lam_vecs, g_head_b, w_pb, w_out, w_ff1, w_ff2):
    bsz, t, _ = x.shape
    past = k_past.shape[1]
    x2d = x.reshape(bsz * t, D_MODEL)
    qk_a, v_a, gates, k_new, k16, v_new, qt_blk, vt_blk = _proj(x2d, g_pre_mix, w_in, b_in)

    hn, c_new, n_new, m_new = _mlstm(qk_a.reshape(bsz, t, 2 * D_A), v_a.reshape(bsz, t, D_A),
                                     gates.reshape(bsz, t, 2 * HA), conv0, c0, n0, m0, conv_w, conv_b, g_head_a)

    if past == 0:
        tq = min(t, ROW_TILE)
        tk = min(t, KEY_TILE)
        tiles = _bias_tiles(rel_bias, -tk, tk + tq, tq)
        s_idx = lax.broadcasted_iota(jnp.int32, (tq, tq), 0)
        t_idx = lax.broadcasted_iota(jnp.int32, (tq, tq), 1)
        allowed = (s_idx // CHUNK) <= (t_idx // CHUNK)
        bias_prev = tiles[:, :tk]
        bias_diag = jnp.where(allowed[None], tiles[:, tk:], NEG).reshape(HB, tq // tk, tk, tq).transpose(1, 0, 2, 3)
        k16 = k16.reshape(HB, bsz, t, 2 * DK)
    else:
        tq = t
        nk = past + t
        tk = -(-nk // 128) * 128
        k16 = jnp.concatenate([jnp.transpose(k_past.astype(BF16), (2, 0, 1, 3)), k16.reshape(HB, bsz, t, 2 * DK),
                               jnp.zeros((HB, bsz, tk - nk, 2 * DK), BF16)], axis=2)
        vt_past = jnp.transpose(v_past.astype(BF16), (2, 0, 3, 1))
        vt_new = jnp.transpose(vt_blk, (0, 2, 1, 3)).reshape(HB, DV, bsz, t)
        vt_new = jnp.transpose(vt_new, (0, 2, 1, 3))
        vt_blk = jnp.concatenate([vt_past, vt_new, jnp.zeros((HB, bsz, DV, tk - nk), BF16)], axis=3)
        qt_blk = jnp.transpose(qt_blk.reshape(D_QB, bsz, t), (1, 0, 2))
        s_idx = lax.broadcasted_iota(jnp.int32, (tk, tq), 0)
        t_idx = lax.broadcasted_iota(jnp.int32, (tk, tq), 1)
        allowed = (s_idx < nk) & ((s_idx // CHUNK) <= ((past + t_idx) // CHUNK))
        bias_diag = jnp.where(allowed[None], _bias_tiles(rel_bias, -past, tk, tq), NEG)[None]
        bias_prev = bias_diag[0]
    o_b = _attn(qt_blk, k16, vt_blk, bias_diag, bias_prev, g_head_b, lam_vecs, bsz, t, tq, tk)

    y = _out(x2d, hn.reshape(bsz * t, D_A), o_b.reshape(bsz * t, D_VB), g_pre_mix, g_post_mix, g_pre_ffn,
             g_post_ffn, w_in, b_in, w_pa, w_pb, w_out, w_ff1, w_ff2)
    return (y.reshape(bsz, t, D_MODEL), k_new.reshape(bsz, t, HB, 2 * DK), v_new.reshape(bsz, t, HB, DV),
            c_new, n_new, m_new[:, 0, :HA], qk_a.reshape(bsz, t, 2 * D_A)[:, t - (CONV_W - 1):, :])


def kernel(x_prompt, x_sample, cache_k, cache_v, state_C, state_n, state_m, state_conv, rel_bias, g_pre_mix,
           g_post_mix, g_pre_ffn, g_post_ffn, w_in, b_in, conv_w, conv_b, g_head_a, w_pa, lambda_q1, lambda_k1,
           lambda_q2, lambda_k2, g_head_b, w_pb, w_out, w_ff1, w_ff2):
    lam_vecs = jnp.concatenate([lambda_q1, lambda_k1, lambda_q2, lambda_k2], axis=0)
    weights = (rel_bias, g_pre_mix[0], g_post_mix[0], g_pre_ffn[0], g_post_ffn[0], w_in[0], b_in[0], conv_w[0],
               conv_b[0], g_head_a[0], w_pa[0], lam_vecs, g_head_b[0], w_pb[0], w_out[0], w_ff1[0], w_ff2[0])
    bp = x_prompt.shape[0]
    prompt = _layer(x_prompt, jnp.zeros((bp, 0, HB, 2 * DK), F32), jnp.zeros((bp, 0, HB, DV), F32),
                    jnp.zeros((bp, HA, DH, DH), F32), jnp.zeros((bp, HA, DH), F32), jnp.zeros((bp, HA), F32),
                    jnp.zeros((bp, CONV_W - 1, 2 * D_A), F32), *weights)
    sample = _layer(x_sample, cache_k[0], cache_v[0], state_C[0], state_n[0], state_m[0], state_conv[0], *weights)
    yp, ys = prompt[0], sample[0]
    return (yp, ys) + tuple(a[None] for a in prompt[1:]) + tuple(a[None] for a in sample[1:])
```

```python
import functools
import math

import numpy as np
import jax
import jax.numpy as jnp
from jax import lax
from jax.experimental import pallas as pl
from jax.experimental.pallas import tpu as pltpu

F32 = jnp.float32
BF16 = jnp.bfloat16

D_MODEL = 1024
HA = 4
DH = 128
D_A = HA * DH
HB = 4
DK = 64
DV = 2 * DK
D_QB = HB * 2 * DK
D_VB = HB * DV
CONV_W = 4
FF = 4 * D_MODEL
NUM_BUCKETS = 32
MAX_DISTANCE = 128
CHUNK = 64
EPS = 1e-6
LAMBDA_INIT = 0.8 - 0.6 * math.exp(-0.3 * 0)
NEG = -1e30
LOG2E = math.log2(math.e)
QSCALE = DK ** -0.5 * LOG2E
N_GATE_PAD = 128
ROW_TILE = 512
KEY_TILE = 256
OUT_ROW_TILE = 256
FF_CHUNK = 1024
VMEM_LIMIT_BYTES = 56 * 1024 * 1024

NT_DIMS = (((1,), (1,)), ((), ()))
TN_DIMS = (((0,), (0,)), ((), ()))


def _const_spec(shape):
    zeros = (0,) * len(shape)
    return pl.BlockSpec(shape, lambda *_: zeros, pipeline_mode=pl.Buffered(1))


def _rms(x, g):
    return x * lax.rsqrt(jnp.mean(x * x, axis=-1, keepdims=True) + EPS) * g


def _log_sigmoid(x):
    return -(jnp.maximum(-x, 0.0) + jnp.log(1.0 + jnp.exp(-jnp.abs(x))))


def _split3(x):
    x1 = x.astype(BF16)
    r = x - x1.astype(F32)
    x2 = r.astype(BF16)
    r = r - x2.astype(F32)
    return x1, x2, r.astype(BF16)


def _mm(a, b):
    return jnp.dot(a, b, preferred_element_type=F32)


def _proj_kernel(x_ref, g_ref, wqk_ref, bqk_ref, wva_ref, bva_ref, wg_ref, bg_ref, wk_ref, bk_ref,
                 wv_ref, bv_ref, wqt_ref, bqt_ref, wvt_ref, bvt_ref,
                 qk_ref, va_ref, gate_ref, k_ref, k16_ref, v_ref, qt_ref, vt_ref):
    u = _rms(x_ref[...], g_ref[...]).astype(BF16)
    qk_ref[...] = _mm(u, wqk_ref[...]) + bqk_ref[...]
    va_ref[...] = _mm(u, wva_ref[...]) + bva_ref[...]
    gate_ref[...] = (_mm(u, wg_ref[...]) + bg_ref[...])[:, :2 * HA]
    k = _mm(u, wk_ref[...]) + bk_ref[...]
    k_ref[...] = k
    k16 = k.astype(BF16)
    for h in range(HB):
        k16_ref[h] = k16[:, h * 2 * DK:(h + 1) * 2 * DK]
    v_ref[...] = _mm(u, wv_ref[...]) + bv_ref[...]
    qt = lax.dot_general(wqt_ref[...], u, NT_DIMS, preferred_element_type=F32) + bqt_ref[...]
    qt_ref[...] = (qt * QSCALE).astype(BF16)
    vt = lax.dot_general(wvt_ref[...], u, NT_DIMS, preferred_element_type=F32) + bvt_ref[...]
    vt = vt.astype(BF16)
    tk = vt_ref.shape[-1]
    for h in range(HB):
        for c in range(vt_ref.shape[1]):
            vt_ref[h, c] = vt[h * DV:(h + 1) * DV, c * tk:(c + 1) * tk]


def _proj(x2d, g_pre, w_in, b_in):
    rows = x2d.shape[0]
    tm = min(rows, ROW_TILE)
    nblk = rows // tm
    tk = min(tm, KEY_TILE)
    o_qk, o_va, o_oa, o_i, o_q, o_k, o_v, o_ga = 0, 2 * D_A, 3 * D_A, 4 * D_A, 4 * D_A + 2 * HA, \
        4 * D_A + 2 * HA + D_QB, 4 * D_A + 2 * HA + 2 * D_QB, 4 * D_A + 2 * HA + 2 * D_QB + D_VB
    del o_oa, o_ga

    def cols(lo, n):
        return w_in[:, lo:lo + n].astype(BF16), b_in[lo:lo + n].reshape(1, n)

    wqk, bqk = cols(o_qk, 2 * D_A)
    wva, bva = cols(o_va, D_A)
    wg = jnp.pad(w_in[:, o_i:o_i + 2 * HA], ((0, 0), (0, N_GATE_PAD - 2 * HA))).astype(BF16)
    bg = jnp.pad(b_in[o_i:o_i + 2 * HA], (0, N_GATE_PAD - 2 * HA)).reshape(1, N_GATE_PAD)
    wk, bk = cols(o_k, D_QB)
    wv, bv = cols(o_v, D_VB)
    wqt = w_in[:, o_q:o_q + D_QB].T.astype(BF16)
    bqt = b_in[o_q:o_q + D_QB].reshape(D_QB, 1)
    wvt = w_in[:, o_v:o_v + D_VB].T.astype(BF16)
    bvt = b_in[o_v:o_v + D_VB].reshape(D_VB, 1)
    consts = (g_pre.reshape(1, D_MODEL), wqk, bqk, wva, bva, wg, bg, wk, bk, wv, bv, wqt, bqt, wvt, bvt)

    def row_spec(n):
        return pl.BlockSpec((tm, n), lambda i: (i, 0))

    return pl.pallas_call(
        _proj_kernel,
        grid=(nblk,),
        in_specs=[row_spec(D_MODEL)] + [_const_spec(c.shape) for c in consts],
        out_specs=[row_spec(2 * D_A), row_spec(D_A), row_spec(2 * HA), row_spec(D_QB),
                   pl.BlockSpec((HB, tm, 2 * DK), lambda i: (0, i, 0)), row_spec(D_VB),
                   pl.BlockSpec((None, D_QB, tm), lambda i: (i, 0, 0)),
                   pl.BlockSpec((HB, tm // tk, DV, tk), lambda i: (0, i, 0, 0))],
        out_shape=[jax.ShapeDtypeStruct((rows, 2 * D_A), F32), jax.ShapeDtypeStruct((rows, D_A), F32),
                   jax.ShapeDtypeStruct((rows, 2 * HA), F32), jax.ShapeDtypeStruct((rows, D_QB), F32),
                   jax.ShapeDtypeStruct((HB, rows, 2 * DK), BF16), jax.ShapeDtypeStruct((rows, D_VB), F32),
                   jax.ShapeDtypeStruct((nblk, D_QB, tm), BF16),
                   jax.ShapeDtypeStruct((HB, rows // tk, DV, tk), BF16)],
        compiler_params=pltpu.CompilerParams(dimension_semantics=("parallel",),
                                             vmem_limit_bytes=VMEM_LIMIT_BYTES),
        name="proj",
    )(x2d, *consts)


def _mlstm_kernel(qk_ref, va_ref, gcol_ref, grow_ref, conv0_ref, c0_ref, n0_ref, m0_ref, cw_ref, cb_ref,
                  gha_ref, hn_ref, c_ref, n_ref, m_ref, xp_sc, q_sc, k_sc, st_sc, m_sc, *, chunk, n_chunks):
    j = pl.program_id(1)
    tb = chunk * n_chunks

    @pl.when(j == 0)
    def _():
        xp_sc[0:8, :] = conv0_ref[...]
        for h in range(HA):
            st_sc[h, :, 0:DH] = c0_ref[h].T
            st_sc[h, :, DH:2 * DH] = jnp.broadcast_to(n0_ref[h:h + 1, :], (DH, DH)).T
        m_sc[...] = m0_ref[...]

    @pl.when(j > 0)
    def _():
        xp_sc[0:8, :] = xp_sc[tb:tb + 8, :]

    xp_sc[8:8 + tb, :] = qk_ref[...]
    cw = cw_ref[...]
    conv = cb_ref[...] + xp_sc[8:8 + tb, :] * cw[CONV_W - 1:CONV_W, :]
    for t in range(CONV_W - 1):
        conv = conv + xp_sc[8 - (CONV_W - 1) + t:8 - (CONV_W - 1) + t + tb, :] * cw[t:t + 1, :]
    act = conv * jax.nn.sigmoid(conv)
    q_sc[...] = act[:, :D_A].astype(BF16)
    k_sc[...] = (act[:, D_A:] * DH ** -0.5).astype(BF16)

    rr = lax.broadcasted_iota(jnp.int32, (chunk, chunk), 0)
    cc = lax.broadcasted_iota(jnp.int32, (chunk, chunk), 1)
    tril = cc <= rr
    ltri = tril.astype(BF16)
    utri = (rr <= cc).astype(BF16)
    ones = jnp.ones((chunk, DH), F32)

    for c in range(n_chunks):
        sl = slice(c * chunk, (c + 1) * chunk)
        gc = gcol_ref[sl, :]
        ig_c = gc[:, 0:HA]
        b_c = sum(_mm(ltri, p) for p in _split3(_log_sigmoid(gc[:, HA:2 * HA])))
        g = b_c[chunk - 1:chunk, :]
        ls_c = g - b_c + ig_c
        m_old = m_sc[0:1, 0:HA]
        m_new = jnp.maximum(g + m_old, jnp.max(ls_c, axis=0, keepdims=True))
        ws_c = jnp.exp(ls_c - m_new)
        decay = jnp.exp(g + m_old - m_new)
        a_c = b_c + m_old
        m_sc[0:1, 0:HA] = m_new
        gr = grow_ref[c]
        b_r = sum(_mm(p, utri) for p in _split3(_log_sigmoid(gr[HA:2 * HA, :])))
        c_r = gr[0:HA, :] - b_r
        for h in range(HA):
            hs = slice(h * DH, (h + 1) * DH)
            dlog = jnp.where(tril, b_c[:, h:h + 1] + c_r[h:h + 1, :], NEG)
            a_h = a_c[:, h:h + 1]
            m_t = jnp.maximum(a_h, jnp.max(dlog, axis=1, keepdims=True))
            p = jnp.exp(dlog - m_t)
            w_int = jnp.exp(a_h - m_t)
            q_h = q_sc[sl, hs]
            k_h = k_sc[sl, hs]
            vaug = jnp.concatenate([va_ref[sl, hs], ones], axis=1)
            s = (p * lax.dot_general(q_h, k_h, NT_DIMS, preferred_element_type=F32)).astype(BF16)
            st = st_sc[h]
            acc = w_int * _mm(q_h, st.astype(BF16)) + _mm(s, vaug.astype(BF16))
            hh = acc[:, :DH] / jnp.maximum(jnp.abs(acc[:, DH:]), jnp.exp(-m_t))
            hn_ref[sl, hs] = _rms(hh, gha_ref[h:h + 1, :])
            wv = (ws_c[:, h:h + 1] * vaug).astype(BF16)
            st_sc[h] = decay[:, h:h + 1] * st + lax.dot_general(k_h, wv, TN_DIMS, preferred_element_type=F32)

    @pl.when(j == pl.num_programs(1) - 1)
    def _():
        for h in range(HA):
            st = st_sc[h]
            c_ref[h] = st[:, :DH].T
            n_ref[h:h + 1, :] = st[:, DH:].T[0:1, :]
        m_ref[...] = m_sc[...]


def _mlstm(qk_a, v_a, gates, conv0, c0, n0, m0, conv_w, conv_b, g_head_a):
    bsz, t, _ = qk_a.shape
    chunk = min(t, CHUNK)
    n_chunks = min(t // chunk, ROW_TILE // chunk)
    tb = chunk * n_chunks
    nblk = t // tb
    grow = gates.reshape(bsz, t // chunk, chunk, 2 * HA).transpose(0, 1, 3, 2)
    conv0p = jnp.pad(conv0, ((0, 0), (8 - (CONV_W - 1), 0), (0, 0)))
    m0p = jnp.pad(m0, ((0, 0), (0, 128 - HA)))[:, None, :] * jnp.ones((1, 8, 1), F32)
    kern = functools.partial(_mlstm_kernel, chunk=chunk, n_chunks=n_chunks)
    return pl.pallas_call(
        kern,
        grid=(bsz, nblk),
        in_specs=[pl.BlockSpec((None, tb, 2 * D_A), lambda b, j: (b, j, 0)),
                  pl.BlockSpec((None, tb, D_A), lambda b, j: (b, j, 0)),
                  pl.BlockSpec((None, tb, 2 * HA), lambda b, j: (b, j, 0)),
                  pl.BlockSpec((None, n_chunks, 2 * HA, chunk), lambda b, j: (b, j, 0, 0)),
                  pl.BlockSpec((None, 8, 2 * D_A), lambda b, j: (b, 0, 0)),
                  pl.BlockSpec((None, HA, DH, DH), lambda b, j: (b, 0, 0, 0)),
                  pl.BlockSpec((None, HA, DH), lambda b, j: (b, 0, 0)),
                  pl.BlockSpec((None, 8, 128), lambda b, j: (b, 0, 0)),
                  _const_spec((CONV_W, 2 * D_A)), _const_spec((1, 2 * D_A)), _const_spec((HA, DH))],
        out_specs=[pl.BlockSpec((None, tb, D_A), lambda b, j: (b, j, 0)),
                   pl.BlockSpec((None, HA, DH, DH), lambda b, j: (b, 0, 0, 0)),
                   pl.BlockSpec((None, HA, DH), lambda b, j: (b, 0, 0)),
                   pl.BlockSpec((None, 8, 128), lambda b, j: (b, 0, 0))],
        out_shape=[jax.ShapeDtypeStruct((bsz, t, D_A), F32), jax.ShapeDtypeStruct((bsz, HA, DH, DH), F32),
                   jax.ShapeDtypeStruct((bsz, HA, DH), F32), jax.ShapeDtypeStruct((bsz, 8, 128), F32)],
        scratch_shapes=[pltpu.VMEM((tb + 8, 2 * D_A), F32), pltpu.VMEM((tb, D_A), BF16),
                        pltpu.VMEM((tb, D_A), BF16), pltpu.VMEM((HA, DH, 2 * DH), F32),
                        pltpu.VMEM((8, 128), F32)],
        compiler_params=pltpu.CompilerParams(dimension_semantics=("parallel", "arbitrary"),
                                             vmem_limit_bytes=VMEM_LIMIT_BYTES),
        name="mlstm",
    )(qk_a, v_a, gates, grow, conv0p, c0, n0, m0p, conv_w, conv_b.reshape(1, 2 * D_A), g_head_a)


def _attn_kernel(qt_ref, k_ref, vt_ref, bd_ref, bp_ref, ghb_ref, lam_ref, o_ref,
                 qpad_sc, m_sc, l_sc, acc_sc, s_sc, mx_sc, *, tq, tk, n_qblk, n_diag):
    i = pl.program_id(1)
    lv = lam_ref[...]
    lam = (jnp.exp(jnp.sum(lv[0:1, :] * lv[1:2, :], axis=1, keepdims=True))
           - jnp.exp(jnp.sum(lv[2:3, :] * lv[3:4, :], axis=1, keepdims=True)) + LAMBDA_INIT)

    first_map = lax.broadcasted_iota(jnp.int32, (2 * DK, tq), 0) < DK
    for h in range(HB):
        qh = qt_ref[h * 2 * DK:(h + 1) * 2 * DK, :]
        zero = jnp.zeros_like(qh)
        qpad_sc[h, :, 0:tq] = jnp.where(first_map, qh, zero)
        qpad_sc[h, :, tq:2 * tq] = jnp.where(first_map, zero, qh)
    m_sc[...] = jnp.full(m_sc.shape, NEG, F32)
    l_sc[...] = jnp.zeros(l_sc.shape, F32)
    acc_sc[...] = jnp.zeros(acc_sc.shape, F32)

    def logits(jb, h, bias):
        row0 = pl.multiple_of(jb * tk, tk)
        s = _mm(k_ref[h, pl.ds(row0, tk), :], qpad_sc[h])
        if bias is not None:
            s = s + jnp.concatenate([bias, bias], axis=1)
        return s

    def accumulate(jb, h, s, mx):
        m_old = m_sc[h]
        m_new = jnp.maximum(m_old, mx)
        alpha = jnp.exp2(m_old - m_new)
        p = jnp.exp2(s - m_new)
        l_sc[h] = alpha * l_sc[h] + jnp.sum(p, axis=0, keepdims=True)
        acc_sc[h] = alpha * acc_sc[h] + _mm(vt_ref[h, jb], p.astype(BF16))
        m_sc[h] = m_new

    def stage1(jb, pair, slot):
        for hh in range(2):
            s = logits(jb, 2 * pair + hh, None)
            s_sc[slot, hh] = s
            mx_sc[slot, hh] = jnp.max(s, axis=0, keepdims=True)

    def stage2(jb, pair, slot):
        for hh in range(2):
            accumulate(jb, 2 * pair + hh, s_sc[slot, hh], mx_sc[slot, hh])

    def block(jb, bias_ref):
        for h in range(HB):
            s = logits(jb, h, bias_ref[h])
            accumulate(jb, h, s, jnp.max(s, axis=0, keepdims=True))

    first_diag = i * n_diag
    if n_qblk > 1:
        n_far = jnp.maximum(first_diag - 1, 0)

        @pl.when(n_far > 0)
        def _():
            stage1(0, 0, 0)

        def far(jb, carry):
            stage1(jb, 1, 1)
            stage2(jb, 0, 0)
            stage1(jb + 1, 0, 0)
            stage2(jb, 1, 1)
            return carry
        lax.fori_loop(0, n_far, far, 0)

        @pl.when(i >= 1)
        def _():
            block(first_diag - 1, bp_ref)
    for d in range(n_diag):
        block(first_diag + d, bd_ref.at[d])

    for h in range(HB):
        on = acc_sc[h] / l_sc[h]
        o = on[:, 0:tq] - lam * on[:, tq:2 * tq]
        o = o * lax.rsqrt(jnp.mean(o * o, axis=0, keepdims=True) + EPS) * (ghb_ref[...] * (1.0 - LAMBDA_INIT))
        o_ref[:, h * DV:(h + 1) * DV] = o.T.astype(o_ref.dtype)


def _bucket(rel):
    half, max_exact = NUM_BUCKETS // 2, NUM_BUCKETS // 4
    n = np.abs(rel).astype(np.int64)
    sq = np.maximum(n * n // (max_exact * max_exact), 1)
    large = max_exact + np.floor(np.log2(sq.astype(np.float64)) + 1e-9).astype(np.int64)
    large = np.minimum(large, half - 1)
    return np.where(rel > 0, half, 0) + np.where(n < max_exact, n, large)


def _bias_tiles(rel_bias, rel0, nk, tq):
    table = (rel_bias - rel_bias[NUM_BUCKETS // 2 - 1:NUM_BUCKETS // 2, :]) * LOG2E
    half, max_exact = NUM_BUCKETS // 2, NUM_BUCKETS // 4
    dist = np.arange(4 * MAX_DISTANCE)
    buckets = _bucket(-dist)
    rel = rel0 + lax.broadcasted_iota(jnp.int32, (nk, tq), 0) - lax.broadcasted_iota(jnp.int32, (nk, tq), 1)
    n = jnp.abs(rel)
    large = jnp.full((nk, tq), max_exact, jnp.int32)
    for b in range(max_exact + 1, half):
        large = large + (n >= int(dist[buckets >= b][0])).astype(jnp.int32)
    bucket = jnp.where(rel > 0, half, 0) + jnp.where(n < max_exact, n, large)
    tile = jnp.zeros((HB, nk, tq), F32)
    for b in range(NUM_BUCKETS):
        tile = jnp.where((bucket == b)[None], table[b][:, None, None], tile)
    return tile


def _attn(qt_blk, k16, vt_blk, bias_diag, bias_prev, g_head_b, lam_vecs, bsz, t, tq, tk):
    n_qblk = t // tq
    nk = k16.shape[2]
    n_kblk = nk // tk
    n_diag = bias_diag.shape[0]
    kern = functools.partial(_attn_kernel, tq=tq, tk=tk, n_qblk=n_qblk, n_diag=n_diag)
    return pl.pallas_call(
        kern,
        grid=(bsz, n_qblk),
        in_specs=[pl.BlockSpec((None, D_QB, tq), lambda b, i: (b * n_qblk + i, 0, 0)),
                  pl.BlockSpec((HB, None, nk, 2 * DK), lambda b, i: (0, b, 0, 0), pipeline_mode=pl.Buffered(1)),
                  pl.BlockSpec((HB, n_kblk, DV, tk), lambda b, i: (0, b, 0, 0), pipeline_mode=pl.Buffered(1)),
                  _const_spec(bias_diag.shape), _const_spec(bias_prev.shape),
                  _const_spec((DV, 1)), _const_spec((4, DK))],
        out_specs=pl.BlockSpec((None, tq, D_VB), lambda b, i: (b, i, 0)),
        out_shape=jax.ShapeDtypeStruct((bsz, t, D_VB), BF16),
        scratch_shapes=[pltpu.VMEM((HB, 2 * DK, 2 * tq), BF16), pltpu.VMEM((HB, 1, 2 * tq), F32),
                        pltpu.VMEM((HB, 1, 2 * tq), F32), pltpu.VMEM((HB, DV, 2 * tq), F32),
                        pltpu.VMEM((2, 2, tk, 2 * tq), F32), pltpu.VMEM((2, 2, 1, 2 * tq), F32)],
        compiler_params=pltpu.CompilerParams(dimension_semantics=("parallel", "arbitrary"),
                                             vmem_limit_bytes=VMEM_LIMIT_BYTES),
        name="attn",
    )(qt_blk, k16, vt_blk, bias_diag, bias_prev, g_head_b.reshape(DV, 1), lam_vecs)


def _out_kernel(x_ref, hn_ref, ob_ref, gpre_ref, gpost_ref, gpref_ref, gpostf_ref, woa_ref, boa_ref,
                wga_ref, bga_ref, wgb_ref, bgb_ref, wpa_ref, wpb_ref, wout_ref, wff1_ref, wff2_ref, o_ref):
    x = x_ref[...]
    u = _rms(x, gpre_ref[...]).astype(BF16)
    h_a = (hn_ref[...] * jax.nn.sigmoid(_mm(u, woa_ref[...]) + boa_ref[...])).astype(BF16)
    y_a = _mm(h_a, wpa_ref[...])
    y_b = _mm(ob_ref[...], wpb_ref[...])
    gate_a = jax.nn.sigmoid(_mm(u, wga_ref[...]) + bga_ref[...])
    gate_b = jax.nn.sigmoid(_mm(u, wgb_ref[...]) + bgb_ref[...])
    mix = _mm((gate_a * y_a + gate_b * y_b).astype(BF16), wout_ref[...])
    x1 = x + _rms(mix, gpost_ref[...])
    f = _rms(x1, gpref_ref[...]).astype(BF16)
    hf = jnp.zeros_like(x1)
    for c in range(FF // FF_CHUNK):
        mid = jnp.maximum(_mm(f, wff1_ref[:, c * FF_CHUNK:(c + 1) * FF_CHUNK]), 0.0)
        hf = hf + _mm((mid * mid).astype(BF16), wff2_ref[c * FF_CHUNK:(c + 1) * FF_CHUNK, :])
    o_ref[...] = x1 + _rms(hf, gpostf_ref[...])


def _out(x2d, hn2d, ob2d, g_pre_mix, g_post_mix, g_pre_ffn, g_post_ffn, w_in, b_in, w_pa, w_pb, w_out,
         w_ff1, w_ff2):
    rows = x2d.shape[0]
    tm = min(rows, OUT_ROW_TILE)
    o_oa = 3 * D_A
    o_ga = 4 * D_A + 2 * HA + 2 * D_QB + D_VB
    o_gb = o_ga + D_MODEL

    def cols(lo, n):
        return w_in[:, lo:lo + n].astype(BF16), b_in[lo:lo + n].reshape(1, n)

    woa, boa = cols(o_oa, D_A)
    wga, bga = cols(o_ga, D_MODEL)
    wgb, bgb = cols(o_gb, D_MODEL)
    consts = (g_pre_mix.reshape(1, -1), g_post_mix.reshape(1, -1), g_pre_ffn.reshape(1, -1),
              g_post_ffn.reshape(1, -1), woa, boa, wga, bga, wgb, bgb, w_pa.astype(BF16), w_pb.astype(BF16),
              w_out.astype(BF16), w_ff1.astype(BF16), w_ff2.astype(BF16))

    def row_spec(n):
        return pl.BlockSpec((tm, n), lambda i: (i, 0))

    return pl.pallas_call(
        _out_kernel,
        grid=(rows // tm,),
        in_specs=[row_spec(D_MODEL), row_spec(D_A), row_spec(D_VB)] + [_const_spec(c.shape) for c in consts],
        out_specs=row_spec(D_MODEL),
        out_shape=jax.ShapeDtypeStruct((rows, D_MODEL), F32),
        compiler_params=pltpu.CompilerParams(dimension_semantics=("parallel",),
                                             vmem_limit_bytes=VMEM_LIMIT_BYTES),
        name="out",
    )(x2d, hn2d, ob2d, *consts)


def _layer(x, k_past, v_past, c0, n0, m0, conv0, rel_bias, g_pre_mix, g_post_mix, g_pre_ffn, g_post_ffn,
           w_in, b_in, conv_w, conv_b, g_head_a, w_pa, lam_vecs, g_head_b, w_pb, w_out, w_ff1, w_ff2):
    bsz, t, _ = x.shape
    past = k_past.shape[1]
    x2d = x.reshape(bsz * t, D_MODEL)
    qk_a, v_a, gates, k_new, k16, v_new, qt_blk, vt_blk = _proj(x2d, g_pre_mix, w_in, b_in)

    hn, c_new, n_new, m_new = _mlstm(qk_a.reshape(bsz, t, 2 * D_A), v_a.reshape(bsz, t, D_A),
                                     gates.reshape(bsz, t, 2 * HA), conv0, c0, n0, m0, conv_w, conv_b, g_head_a)

    if past == 0:
        tq = min(t, ROW_TILE)
        tk = min(t, KEY_TILE)
        tiles = _bias_tiles(rel_bias, -tk, tk + tq, tq)
        s_idx = lax.broadcasted_iota(jnp.int32, (tq, tq), 0)
        t_idx = lax.broadcasted_iota(jnp.int32, (tq, tq), 1)
        allowed = (s_idx // CHUNK) <= (t_idx // CHUNK)
        bias_prev = tiles[:, :tk]
        bias_diag = jnp.where(allowed[None], tiles[:, tk:], NEG).reshape(HB, tq // tk, tk, tq).transpose(1, 0, 2, 3)
        k16 = k16.reshape(HB, bsz, t, 2 * DK)
    else:
        tq = t
        nk = past + t
        tk = -(-nk // 128) * 128
        k16 = jnp.concatenate([jnp.transpose(k_past.astype(BF16), (2, 0, 1, 3)), k16.reshape(HB, bsz, t, 2 * DK),
                               jnp.zeros((HB, bsz, tk - nk, 2 * DK), BF16)], axis=2)
        vt_past = jnp.transpose(v_past.astype(BF16), (2, 0, 3, 1))
        vt_new = jnp.transpose(vt_blk, (0, 2, 1, 3)).reshape(HB, DV, bsz, t)
        vt_new = jnp.transpose(vt_new, (0, 2, 1, 3))
        vt_blk = jnp.concatenate([vt_past, vt_new, jnp.zeros((HB, bsz, DV, tk - nk), BF16)], axis=3)
        qt_blk = jnp.transpose(qt_blk.reshape(D_QB, bsz, t), (1, 0, 2))
        s_idx = lax.broadcasted_iota(jnp.int32, (tk, tq), 0)
        t_idx = lax.broadcasted_iota(jnp.int32, (tk, tq), 1)
        allowed = (s_idx < nk) & ((s_idx // CHUNK) <= ((past + t_idx) // CHUNK))
        bias_diag = jnp.where(allowed[None], _bias_tiles(rel_bias, -past, tk, tq), NEG)[None]
        bias_prev = bias_diag[0]
    o_b = _attn(qt_blk, k16, vt_blk, bias_diag, bias_prev, g_head_b, lam_vecs, bsz, t, tq, tk)

    y = _out(x2d, hn.reshape(bsz * t, D_A), o_b.reshape(bsz * t, D_VB), g_pre_mix, g_post_mix, g_pre_ffn,
             g_post_ffn, w_in, b_in, w_pa, w_pb, w_out, w_ff1, w_ff2)
    return (y.reshape(bsz, t, D_MODEL), k_new.reshape(bsz, t, HB, 2 * DK), v_new.reshape(bsz, t, HB, DV),
            c_new, n_new, m_new[:, 0, :HA], qk_a.reshape(bsz, t, 2 * D_A)[:, t - (CONV_W - 1):, :])


def kernel(x_prompt, x_sample, cache_k, cache_v, state_C, state_n, state_m, state_conv, rel_bias, g_pre_mix,
           g_post_mix, g_pre_ffn, g_post_ffn, w_in, b_in, conv_w, conv_b, g_head_a, w_pa, lambda_q1, lambda_k1,
           lambda_q2, lambda_k2, g_head_b, w_pb, w_out, w_ff1, w_ff2):
    lam_vecs = jnp.concatenate([lambda_q1, lambda_k1, lambda_q2, lambda_k2], axis=0)
    weights = (rel_bias, g_pre_mix[0], g_post_mix[0], g_pre_ffn[0], g_post_ffn[0], w_in[0], b_in[0], conv_w[0],
               conv_b[0], g_head_a[0], w_pa[0], lam_vecs, g_head_b[0], w_pb[0], w_out[0], w_ff1[0], w_ff2[0])
    bp = x_prompt.shape[0]
    prompt = _layer(x_prompt, jnp.zeros((bp, 0, HB, 2 * DK), F32), jnp.zeros((bp, 0, HB, DV), F32),
                    jnp.zeros((bp, HA, DH, DH), F32), jnp.zeros((bp, HA, DH), F32), jnp.zeros((bp, HA), F32),
                    jnp.zeros((bp, CONV_W - 1, 2 * D_A), F32), *weights)
    sample = _layer(x_sample, cache_k[0], cache_v[0], state_C[0], state_n[0], state_m[0], state_conv[0], *weights)
    yp, ys = prompt[0], sample[0]
    return (yp, ys) + tuple(a[None] for a in prompt[1:]) + tuple(a[None] for a in sample[1:])
```

```python
import functools
import math

import numpy as np
import jax
import jax.numpy as jnp
from jax import lax
from jax.experimental import pallas as pl
from jax.experimental.pallas import tpu as pltpu

F32 = jnp.float32
BF16 = jnp.bfloat16

D_MODEL = 1024
HA = 4
DH = 128
D_A = HA * DH
HB = 4
DK = 64
DV = 2 * DK
D_QB = HB * 2 * DK
D_VB = HB * DV
CONV_W = 4
FF = 4 * D_MODEL
NUM_BUCKETS = 32
MAX_DISTANCE = 128
CHUNK = 64
MLSTM_CHUNK = 128
EPS = 1e-6
LAMBDA_INIT = 0.8 - 0.6 * math.exp(-0.3 * 0)
NEG = -1e30
LOG2E = math.log2(math.e)
QSCALE = DK ** -0.5 * LOG2E
N_GATE_PAD = 128
ROW_TILE = 512
KEY_TILE = 512
ONES_ROWS = 16
BIAS_CORNER = 128
OUT_ROW_TILE = 256
FF_CHUNK = 1024
VMEM_LIMIT_BYTES = 56 * 1024 * 1024

NT_DIMS = (((1,), (1,)), ((), ()))
TN_DIMS = (((0,), (0,)), ((), ()))


def _const_spec(shape):
    zeros = (0,) * len(shape)
    return pl.BlockSpec(shape, lambda *_: zeros, pipeline_mode=pl.Buffered(1))


def _rms(x, g):
    return x * lax.rsqrt(jnp.mean(x * x, axis=-1, keepdims=True) + EPS) * g


def _log_sigmoid(x):
    return -(jnp.maximum(-x, 0.0) + jnp.log(1.0 + jnp.exp(-jnp.abs(x))))


def _split3(x):
    x1 = x.astype(BF16)
    r = x - x1.astype(F32)
    x2 = r.astype(BF16)
    r = r - x2.astype(F32)
    return x1, x2, r.astype(BF16)


def _mm(a, b):
    return jnp.dot(a, b, preferred_element_type=F32)


def _proj_kernel(x_ref, g_ref, wqk_ref, bqk_ref, wva_ref, bva_ref, wg_ref, bg_ref, wk_ref, bk_ref,
                 wv_ref, bv_ref, wqt_ref, bqt_ref, wvt_ref, bvt_ref,
                 qk_ref, va_ref, gate_ref, k_ref, k16_ref, v_ref, qt_ref, vt_ref):
    u = _rms(x_ref[...], g_ref[...]).astype(BF16)
    qk_ref[...] = _mm(u, wqk_ref[...]) + bqk_ref[...]
    va_ref[...] = _mm(u, wva_ref[...]) + bva_ref[...]
    gate_ref[...] = (_mm(u, wg_ref[...]) + bg_ref[...])[:, :2 * HA]
    k = _mm(u, wk_ref[...]) + bk_ref[...]
    k_ref[...] = k
    k16 = k.astype(BF16)
    for h in range(HB):
        k16_ref[h] = k16[:, h * 2 * DK:(h + 1) * 2 * DK]
    v_ref[...] = _mm(u, wv_ref[...]) + bv_ref[...]
    qt = lax.dot_general(wqt_ref[...], u, NT_DIMS, preferred_element_type=F32) + bqt_ref[...]
    qt_ref[...] = (qt * QSCALE).astype(BF16)
    vt = lax.dot_general(wvt_ref[...], u, NT_DIMS, preferred_element_type=F32) + bvt_ref[...]
    vt = vt.astype(BF16)
    tk = vt_ref.shape[-1]
    for h in range(HB):
        for c in range(vt_ref.shape[1]):
            vt_ref[h, c, 0:DV, :] = vt[h * DV:(h + 1) * DV, c * tk:(c + 1) * tk]
            vt_ref[h, c, DV:DV + ONES_ROWS, :] = jnp.ones((ONES_ROWS, tk), BF16)


def _proj(x2d, g_pre, w_in, b_in):
    rows = x2d.shape[0]
    tm = min(rows, ROW_TILE)
    nblk = rows // tm
    tk = min(tm, KEY_TILE)
    o_qk, o_va, o_oa, o_i, o_q, o_k, o_v, o_ga = 0, 2 * D_A, 3 * D_A, 4 * D_A, 4 * D_A + 2 * HA, \
        4 * D_A + 2 * HA + D_QB, 4 * D_A + 2 * HA + 2 * D_QB, 4 * D_A + 2 * HA + 2 * D_QB + D_VB
    del o_oa, o_ga

    def cols(lo, n):
        return w_in[:, lo:lo + n].astype(BF16), b_in[lo:lo + n].reshape(1, n)

    wqk, bqk = cols(o_qk, 2 * D_A)
    wva, bva = cols(o_va, D_A)
    wg = jnp.pad(w_in[:, o_i:o_i + 2 * HA], ((0, 0), (0, N_GATE_PAD - 2 * HA))).astype(BF16)
    bg = jnp.pad(b_in[o_i:o_i + 2 * HA], (0, N_GATE_PAD - 2 * HA)).reshape(1, N_GATE_PAD)
    wk, bk = cols(o_k, D_QB)
    wv, bv = cols(o_v, D_VB)
    wqt = w_in[:, o_q:o_q + D_QB].T.astype(BF16)
    bqt = b_in[o_q:o_q + D_QB].reshape(D_QB, 1)
    wvt = w_in[:, o_v:o_v + D_VB].T.astype(BF16)
    bvt = b_in[o_v:o_v + D_VB].reshape(D_VB, 1)
    consts = (g_pre.reshape(1, D_MODEL), wqk, bqk, wva, bva, wg, bg, wk, bk, wv, bv, wqt, bqt, wvt, bvt)

    def row_spec(n):
        return pl.BlockSpec((tm, n), lambda i: (i, 0))

    return pl.pallas_call(
        _proj_kernel,
        grid=(nblk,),
        in_specs=[row_spec(D_MODEL)] + [_const_spec(c.shape) for c in consts],
        out_specs=[row_spec(2 * D_A), row_spec(D_A), row_spec(2 * HA), row_spec(D_QB),
                   pl.BlockSpec((HB, tm, 2 * DK), lambda i: (0, i, 0)), row_spec(D_VB),
                   pl.BlockSpec((None, D_QB, tm), lambda i: (i, 0, 0)),
                   pl.BlockSpec((HB, tm // tk, DV + ONES_ROWS, tk), lambda i: (0, i, 0, 0))],
        out_shape=[jax.ShapeDtypeStruct((rows, 2 * D_A), F32), jax.ShapeDtypeStruct((rows, D_A), F32),
                   jax.ShapeDtypeStruct((rows, 2 * HA), F32), jax.ShapeDtypeStruct((rows, D_QB), F32),
                   jax.ShapeDtypeStruct((HB, rows, 2 * DK), BF16), jax.ShapeDtypeStruct((rows, D_VB), F32),
                   jax.ShapeDtypeStruct((nblk, D_QB, tm), BF16),
                   jax.ShapeDtypeStruct((HB, rows // tk, DV + ONES_ROWS, tk), BF16)],
        compiler_params=pltpu.CompilerParams(dimension_semantics=("parallel",),
                                             vmem_limit_bytes=VMEM_LIMIT_BYTES),
        name="proj",
    )(x2d, *consts)


def _mlstm_kernel(qk_ref, va_ref, gcol_ref, grow_ref, conv0_ref, c0_ref, n0_ref, m0_ref, cw_ref, cb_ref,
                  gha_ref, hn_ref, c_ref, n_ref, m_ref, xp_sc, q_sc, k_sc, st_sc, m_sc, *, chunk, n_chunks):
    j = pl.program_id(1)
    tb = chunk * n_chunks

    @pl.when(j == 0)
    def _():
        xp_sc[0:8, :] = conv0_ref[...]
        for h in range(HA):
            st_sc[h, :, 0:DH] = c0_ref[h].T
            st_sc[h, :, DH:2 * DH] = jnp.broadcast_to(n0_ref[h:h + 1, :], (DH, DH)).T
        m_sc[...] = m0_ref[...]

    @pl.when(j > 0)
    def _():
        xp_sc[0:8, :] = xp_sc[tb:tb + 8, :]

    xp_sc[8:8 + tb, :] = qk_ref[...]
    cw = cw_ref[...]
    conv = cb_ref[...] + xp_sc[8:8 + tb, :] * cw[CONV_W - 1:CONV_W, :]
    for t in range(CONV_W - 1):
        conv = conv + xp_sc[8 - (CONV_W - 1) + t:8 - (CONV_W - 1) + t + tb, :] * cw[t:t + 1, :]
    act = conv * jax.nn.sigmoid(conv)
    q_sc[...] = act[:, :D_A].astype(BF16)
    k_sc[...] = (act[:, D_A:] * DH ** -0.5).astype(BF16)

    rr = lax.broadcasted_iota(jnp.int32, (chunk, chunk), 0)
    cc = lax.broadcasted_iota(jnp.int32, (chunk, chunk), 1)
    tril = cc <= rr
    ltri = tril.astype(BF16)
    utri = (rr <= cc).astype(BF16)
    ones = jnp.ones((chunk, DH), F32)

    for c in range(n_chunks):
        sl = slice(c * chunk, (c + 1) * chunk)
        gc = gcol_ref[sl, :]
        ig_c = gc[:, 0:HA]
        b_c = sum(_mm(ltri, p) for p in _split3(_log_sigmoid(gc[:, HA:2 * HA])))
        g = b_c[chunk - 1:chunk, :]
        ls_c = g - b_c + ig_c
        m_old = m_sc[0:1, 0:HA]
        m_new = jnp.maximum(g + m_old, jnp.max(ls_c, axis=0, keepdims=True))
        ws_c = jnp.exp(ls_c - m_new)
        decay = jnp.exp(g + m_old - m_new)
        a_c = b_c + m_old
        m_sc[0:1, 0:HA] = m_new
        gr = grow_ref[c]
        b_r = sum(_mm(p, utri) for p in _split3(_log_sigmoid(gr[HA:2 * HA, :])))
        c_r = gr[0:HA, :] - b_r
        for h in range(HA):
            hs = slice(h * DH, (h + 1) * DH)
            dlog = jnp.where(tril, b_c[:, h:h + 1] + c_r[h:h + 1, :], NEG)
            a_h = a_c[:, h:h + 1]
            m_t = jnp.maximum(a_h, jnp.max(dlog, axis=1, keepdims=True))
            p = jnp.exp(dlog - m_t)
            w_int = jnp.exp(a_h - m_t)
            q_h = q_sc[sl, hs]
            k_h = k_sc[sl, hs]
            vaug = jnp.concatenate([va_ref[sl, hs], ones], axis=1)
            s = (p * lax.dot_general(q_h, k_h, NT_DIMS, preferred_element_type=F32)).astype(BF16)
            st = st_sc[h]
            acc = w_int * _mm(q_h, st.astype(BF16)) + _mm(s, vaug.astype(BF16))
            hh = acc[:, :DH] / jnp.maximum(jnp.abs(acc[:, DH:]), jnp.exp(-m_t))
            hn_ref[sl, hs] = _rms(hh, gha_ref[h:h + 1, :])
            wv = (ws_c[:, h:h + 1] * vaug).astype(BF16)
            st_sc[h] = decay[:, h:h + 1] * st + lax.dot_general(k_h, wv, TN_DIMS, preferred_element_type=F32)

    @pl.when(j == pl.num_programs(1) - 1)
    def _():
        for h in range(HA):
            st = st_sc[h]
            c_ref[h] = st[:, :DH].T
            n_ref[h:h + 1, :] = st[:, DH:].T[0:1, :]
        m_ref[...] = m_sc[...]


def _mlstm(qk_a, v_a, gates, conv0, c0, n0, m0, conv_w, conv_b, g_head_a):
    bsz, t, _ = qk_a.shape
    chunk = min(t, MLSTM_CHUNK)
    n_chunks = min(t // chunk, ROW_TILE // chunk)
    tb = chunk * n_chunks
    nblk = t // tb
    grow = gates.reshape(bsz, t // chunk, chunk, 2 * HA).transpose(0, 1, 3, 2)
    conv0p = jnp.pad(conv0, ((0, 0), (8 - (CONV_W - 1), 0), (0, 0)))
    m0p = jnp.pad(m0, ((0, 0), (0, 128 - HA)))[:, None, :] * jnp.ones((1, 8, 1), F32)
    kern = functools.partial(_mlstm_kernel, chunk=chunk, n_chunks=n_chunks)
    return pl.pallas_call(
        kern,
        grid=(bsz, nblk),
        in_specs=[pl.BlockSpec((None, tb, 2 * D_A), lambda b, j: (b, j, 0)),
                  pl.BlockSpec((None, tb, D_A), lambda b, j: (b, j, 0)),
                  pl.BlockSpec((None, tb, 2 * HA), lambda b, j: (b, j, 0)),
                  pl.BlockSpec((None, n_chunks, 2 * HA, chunk), lambda b, j: (b, j, 0, 0)),
                  pl.BlockSpec((None, 8, 2 * D_A), lambda b, j: (b, 0, 0)),
                  pl.BlockSpec((None, HA, DH, DH), lambda b, j: (b, 0, 0, 0)),
                  pl.BlockSpec((None, HA, DH), lambda b, j: (b, 0, 0)),
                  pl.BlockSpec((None, 8, 128), lambda b, j: (b, 0, 0)),
                  _const_spec((CONV_W, 2 * D_A)), _const_spec((1, 2 * D_A)), _const_spec((HA, DH))],
        out_specs=[pl.BlockSpec((None, tb, D_A), lambda b, j: (b, j, 0)),
                   pl.BlockSpec((None, HA, DH, DH), lambda b, j: (b, 0, 0, 0)),
                   pl.BlockSpec((None, HA, DH), lambda b, j: (b, 0, 0)),
                   pl.BlockSpec((None, 8, 128), lambda b, j: (b, 0, 0))],
        out_shape=[jax.ShapeDtypeStruct((bsz, t, D_A), F32), jax.ShapeDtypeStruct((bsz, HA, DH, DH), F32),
                   jax.ShapeDtypeStruct((bsz, HA, DH), F32), jax.ShapeDtypeStruct((bsz, 8, 128), F32)],
        scratch_shapes=[pltpu.VMEM((tb + 8, 2 * D_A), F32), pltpu.VMEM((tb, D_A), BF16),
                        pltpu.VMEM((tb, D_A), BF16), pltpu.VMEM((HA, DH, 2 * DH), F32),
                        pltpu.VMEM((8, 128), F32)],
        compiler_params=pltpu.CompilerParams(dimension_semantics=("parallel", "arbitrary"),
                                             vmem_limit_bytes=VMEM_LIMIT_BYTES),
        name="mlstm",
    )(qk_a, v_a, gates, grow, conv0p, c0, n0, m0p, conv_w, conv_b.reshape(1, 2 * D_A), g_head_a)


def _attn_kernel(qt_ref, k_ref, vt_ref, bd_ref, bp_ref, ghb_ref, lam_ref, o_ref,
                 qpad_sc, m_sc, acc_sc, s_sc, mx_sc, *, tq, tk, n_qblk):
    i = pl.program_id(1)
    lv = lam_ref[...]
    lam = (jnp.exp(jnp.sum(lv[0:1, :] * lv[1:2, :], axis=1, keepdims=True))
           - jnp.exp(jnp.sum(lv[2:3, :] * lv[3:4, :], axis=1, keepdims=True)) + LAMBDA_INIT)

    first_map = lax.broadcasted_iota(jnp.int32, (2 * DK, tq), 0) < DK
    for h in range(HB):
        qh = qt_ref[h * 2 * DK:(h + 1) * 2 * DK, :]
        zero = jnp.zeros_like(qh)
        qpad_sc[h, :, 0:tq] = jnp.where(first_map, qh, zero)
        qpad_sc[h, :, tq:2 * tq] = jnp.where(first_map, zero, qh)
    m_sc[...] = jnp.full(m_sc.shape, NEG, F32)
    acc_sc[...] = jnp.zeros(acc_sc.shape, F32)

    corner = bp_ref.shape[-1]

    def logits(jb, h):
        row0 = pl.multiple_of(jb * tk, tk)
        return _mm(k_ref[h, pl.ds(row0, tk), :], qpad_sc[h])

    def put(slot, s):
        s_sc[slot] = s
        mx_sc[slot] = jnp.max(s, axis=0, keepdims=True)

    def stage1(jb, h, slot, is_prev):
        put(slot, logits(jb, h))
        if is_prev is not None:
            bias = jnp.where(is_prev, bp_ref[h], 0.0)
            for c0 in (0, tq):
                sc = s_sc[slot, tk - corner:tk, c0:c0 + corner] + bias
                s_sc[slot, tk - corner:tk, c0:c0 + corner] = sc
                mx_sc[slot, :, c0:c0 + corner] = jnp.maximum(mx_sc[slot, :, c0:c0 + corner],
                                                             jnp.max(sc, axis=0, keepdims=True))

    def diag_bias(h):
        return jnp.concatenate([bd_ref[h], bd_ref[h]], axis=1)

    def stage2(jb, h, slot):
        m_old = m_sc[h]
        m_new = jnp.maximum(m_old, mx_sc[slot])
        p = jnp.exp2(s_sc[slot] - m_new).astype(BF16)
        acc_sc[h] = jnp.exp2(m_old - m_new) * acc_sc[h] + _mm(vt_ref[h, jb], p)
        m_sc[h] = m_new

    if n_qblk > 1:
        stage1(0, 0, 0, i == 1)

        def before_diag(jb, carry):
            is_prev = jb == i - 1
            stage1(jb, 1, 1, is_prev)
            stage2(jb, 0, 0)
            stage1(jb, 2, 0, is_prev)
            stage2(jb, 1, 1)
            stage1(jb, 3, 1, is_prev)
            stage2(jb, 2, 0)
            stage1(jb + 1, 0, 0, jb == i - 2)
            stage2(jb, 3, 1)
            return carry
        lax.fori_loop(0, i, before_diag, 0)
    else:
        stage1(0, 0, 0, None)

    put(0, s_sc[0] + diag_bias(0))
    put(1, logits(i, 1) + diag_bias(1))
    stage2(i, 0, 0)
    put(0, logits(i, 2) + diag_bias(2))
    stage2(i, 1, 1)
    put(1, logits(i, 3) + diag_bias(3))
    stage2(i, 2, 0)
    stage2(i, 3, 1)

    for h in range(HB):
        on = acc_sc[h, 0:DV, :] / acc_sc[h, DV:DV + 1, :]
        o = on[:, 0:tq] - lam * on[:, tq:2 * tq]
        o = o * lax.rsqrt(jnp.mean(o * o, axis=0, keepdims=True) + EPS) * (ghb_ref[...] * (1.0 - LAMBDA_INIT))
        o_ref[:, h * DV:(h + 1) * DV] = o.T.astype(o_ref.dtype)


def _bucket(rel):
    half, max_exact = NUM_BUCKETS // 2, NUM_BUCKETS // 4
    n = np.abs(rel).astype(np.int64)
    sq = np.maximum(n * n // (max_exact * max_exact), 1)
    large = max_exact + np.floor(np.log2(sq.astype(np.float64)) + 1e-9).astype(np.int64)
    large = np.minimum(large, half - 1)
    return np.where(rel > 0, half, 0) + np.where(n < max_exact, n, large)


def _bias_tiles(rel_bias, rel0, nk, tq):
    table = (rel_bias - rel_bias[NUM_BUCKETS // 2 - 1:NUM_BUCKETS // 2, :]) * LOG2E
    half, max_exact = NUM_BUCKETS // 2, NUM_BUCKETS // 4
    dist = np.arange(4 * MAX_DISTANCE)
    buckets = _bucket(-dist)
    rel = rel0 + lax.broadcasted_iota(jnp.int32, (nk, tq), 0) - lax.broadcasted_iota(jnp.int32, (nk, tq), 1)
    n = jnp.abs(rel)
    large = jnp.full((nk, tq), max_exact, jnp.int32)
    for b in range(max_exact + 1, half):
        large = large + (n >= int(dist[buckets >= b][0])).astype(jnp.int32)
    bucket = jnp.where(rel > 0, half, 0) + jnp.where(n < max_exact, n, large)
    tile = jnp.zeros((HB, nk, tq), F32)
    for b in range(NUM_BUCKETS):
        tile = jnp.where((bucket == b)[None], table[b][:, None, None], tile)
    return tile


def _attn(qt_blk, k16, vt_blk, bias_diag, bias_prev, g_head_b, lam_vecs, bsz, t, tq, tk):
    n_qblk = t // tq
    nk = k16.shape[2]
    n_kblk = nk // tk
    assert n_qblk == 1 or tq == tk
    kern = functools.partial(_attn_kernel, tq=tq, tk=tk, n_qblk=n_qblk)
    return pl.pallas_call(
        kern,
        grid=(bsz, n_qblk),
        in_specs=[pl.BlockSpec((None, D_QB, tq), lambda b, i: (b * n_qblk + i, 0, 0)),
                  pl.BlockSpec((HB, None, nk, 2 * DK), lambda b, i: (0, b, 0, 0), pipeline_mode=pl.Buffered(1)),
                  pl.BlockSpec((HB, n_kblk, DV + ONES_ROWS, tk), lambda b, i: (0, b, 0, 0),
                               pipeline_mode=pl.Buffered(1)),
                  _const_spec(bias_diag.shape), _const_spec(bias_prev.shape),
                  _const_spec((DV, 1)), _const_spec((4, DK))],
        out_specs=pl.BlockSpec((None, tq, D_VB), lambda b, i: (b, i, 0)),
        out_shape=jax.ShapeDtypeStruct((bsz, t, D_VB), BF16),
        scratch_shapes=[pltpu.VMEM((HB, 2 * DK, 2 * tq), BF16), pltpu.VMEM((HB, 1, 2 * tq), F32),
                        pltpu.VMEM((HB, DV + ONES_ROWS, 2 * tq), F32),
                        pltpu.VMEM((2, tk, 2 * tq), F32), pltpu.VMEM((2, 1, 2 * tq), F32)],
        compiler_params=pltpu.CompilerParams(dimension_semantics=("parallel", "arbitrary"),
                                             vmem_limit_bytes=VMEM_LIMIT_BYTES),
        name="attn",
    )(qt_blk, k16, vt_blk, bias_diag, bias_prev, g_head_b.reshape(DV, 1), lam_vecs)


def _out_kernel(x_ref, hn_ref, ob_ref, gpre_ref, gpost_ref, gpref_ref, gpostf_ref, woa_ref, boa_ref,
                wga_ref, bga_ref, wgb_ref, bgb_ref, wpa_ref, wpb_ref, wout_ref, wff1_ref, wff2_ref, o_ref):
    x = x_ref[...]
    u = _rms(x, gpre_ref[...]).astype(BF16)
    h_a = (hn_ref[...] * jax.nn.sigmoid(_mm(u, woa_ref[...]) + boa_ref[...])).astype(BF16)
    y_a = _mm(h_a, wpa_ref[...])
    y_b = _mm(ob_ref[...], wpb_ref[...])
    gate_a = jax.nn.sigmoid(_mm(u, wga_ref[...]) + bga_ref[...])
    gate_b = jax.nn.sigmoid(_mm(u, wgb_ref[...]) + bgb_ref[...])
    mix = _mm((gate_a * y_a + gate_b * y_b).astype(BF16), wout_ref[...])
    x1 = x + _rms(mix, gpost_ref[...])
    f = _rms(x1, gpref_ref[...]).astype(BF16)
    hf = jnp.zeros_like(x1)
    for c in range(FF // FF_CHUNK):
        mid = jnp.maximum(_mm(f, wff1_ref[:, c * FF_CHUNK:(c + 1) * FF_CHUNK]), 0.0)
        hf = hf + _mm((mid * mid).astype(BF16), wff2_ref[c * FF_CHUNK:(c + 1) * FF_CHUNK, :])
    o_ref[...] = x1 + _rms(hf, gpostf_ref[...])


def _out(x2d, hn2d, ob2d, g_pre_mix, g_post_mix, g_pre_ffn, g_post_ffn, w_in, b_in, w_pa, w_pb, w_out,
         w_ff1, w_ff2):
    rows = x2d.shape[0]
    tm = min(rows, OUT_ROW_TILE)
    o_oa = 3 * D_A
    o_ga = 4 * D_A + 2 * HA + 2 * D_QB + D_VB
    o_gb = o_ga + D_MODEL

    def cols(lo, n):
        return w_in[:, lo:lo + n].astype(BF16), b_in[lo:lo + n].reshape(1, n)

    woa, boa = cols(o_oa, D_A)
    wga, bga = cols(o_ga, D_MODEL)
    wgb, bgb = cols(o_gb, D_MODEL)
    consts = (g_pre_mix.reshape(1, -1), g_post_mix.reshape(1, -1), g_pre_ffn.reshape(1, -1),
              g_post_ffn.reshape(1, -1), woa, boa, wga, bga, wgb, bgb, w_pa.astype(BF16), w_pb.astype(BF16),
              w_out.astype(BF16), w_ff1.astype(BF16), w_ff2.astype(BF16))

    def row_spec(n):
        return pl.BlockSpec((tm, n), lambda i: (i, 0))

    return pl.pallas_call(
        _out_kernel,
        grid=(rows // tm,),
        in_specs=[row_spec(D_MODEL), row_spec(D_A), row_spec(D_VB)] + [_const_spec(c.shape) for c in consts],
        out_specs=row_spec(D_MODEL),
        out_shape=jax.ShapeDtypeStruct((rows, D_MODEL), F32),
        compiler_params=pltpu.CompilerParams(dimension_semantics=("parallel",),
                                             vmem_limit_bytes=VMEM_LIMIT_BYTES),
        name="out",
    )(x2d, hn2d, ob2d, *consts)


def _layer(x, k_past, v_past, c0, n0, m0, conv0, rel_bias, g_pre_mix, g_post_mix, g_pre_ffn, g_post_ffn,
           w_in, b_in, conv_w, conv_b, g_head_a, w_pa, lam_vecs, g_head_b, w_pb, w_out, w_ff1, w_ff2):
    bsz, t, _ = x.shape
    past = k_past.shape[1]
    x2d = x.reshape(bsz * t, D_MODEL)
    qk_a, v_a, gates, k_new, k16, v_new, qt_blk, vt_blk = _proj(x2d, g_pre_mix, w_in, b_in)

    hn, c_new, n_new, m_new = _mlstm(qk_a.reshape(bsz, t, 2 * D_A), v_a.reshape(bsz, t, D_A),
                                     gates.reshape(bsz, t, 2 * HA), conv0, c0, n0, m0, conv_w, conv_b, g_head_a)

    if past == 0:
        tq = tk = min(t, ROW_TILE, KEY_TILE)
        s_idx = lax.broadcasted_iota(jnp.int32, (tk, tq), 0)
        t_idx = lax.broadcasted_iota(jnp.int32, (tk, tq), 1)
        allowed = (s_idx // CHUNK) <= (t_idx // CHUNK)
        bias_diag = jnp.where(allowed[None], _bias_tiles(rel_bias, 0, tk, tq), NEG)
        corner = min(BIAS_CORNER, tk)
        bias_prev = _bias_tiles(rel_bias, -corner, corner, corner)
        k16 = k16.reshape(HB, bsz, t, 2 * DK)
    else:
        tq = t
        nk = past + t
        tk = -(-nk // 128) * 128
        k16 = jnp.concatenate([jnp.transpose(k_past.astype(BF16), (2, 0, 1, 3)), k16.reshape(HB, bsz, t, 2 * DK),
                               jnp.zeros((HB, bsz, tk - nk, 2 * DK), BF16)], axis=2)
        vt_past = jnp.transpose(v_past.astype(BF16), (2, 0, 3, 1))
        vt_new = jnp.transpose(vt_blk[:, :, 0:DV, :], (0, 2, 1, 3)).reshape(HB, DV, bsz, t)
        vt_new = jnp.transpose(vt_new, (0, 2, 1, 3))
        vt_blk = jnp.concatenate([vt_past, vt_new, jnp.zeros((HB, bsz, DV, tk - nk), BF16)], axis=3)
        vt_blk = jnp.concatenate([vt_blk, jnp.ones((HB, bsz, ONES_ROWS, tk), BF16)], axis=2)
        qt_blk = jnp.transpose(qt_blk.reshape(D_QB, bsz, t), (1, 0, 2))
        s_idx = lax.broadcasted_iota(jnp.int32, (tk, tq), 0)
        t_idx = lax.broadcasted_iota(jnp.int32, (tk, tq), 1)
        allowed = (s_idx < nk) & ((s_idx // CHUNK) <= ((past + t_idx) // CHUNK))
        bias_diag = jnp.where(allowed[None], _bias_tiles(rel_bias, -past, tk, tq), NEG)
        bias_prev = bias_diag[:, 0:8, 0:8]
    o_b = _attn(qt_blk, k16, vt_blk, bias_diag, bias_prev, g_head_b, lam_vecs, bsz, t, tq, tk)

    y = _out(x2d, hn.reshape(bsz * t, D_A), o_b.reshape(bsz * t, D_VB), g_pre_mix, g_post_mix, g_pre_ffn,
             g_post_ffn, w_in, b_in, w_pa, w_pb, w_out, w_ff1, w_ff2)
    return (y.reshape(bsz, t, D_MODEL), k_new.reshape(bsz, t, HB, 2 * DK), v_new.reshape(bsz, t, HB, DV),
            c_new, n_new, m_new[:, 0, :HA], qk_a.reshape(bsz, t, 2 * D_A)[:, t - (CONV_W - 1):, :])


def kernel(x_prompt, x_sample, cache_k, cache_v, state_C, state_n, state_m, state_conv, rel_bias, g_pre_mix,
           g_post_mix, g_pre_ffn, g_post_ffn, w_in, b_in, conv_w, conv_b, g_head_a, w_pa, lambda_q1, lambda_k1,
           lambda_q2, lambda_k2, g_head_b, w_pb, w_out, w_ff1, w_ff2):
    lam_vecs = jnp.concatenate([lambda_q1, lambda_k1, lambda_q2, lambda_k2], axis=0)
    weights = (rel_bias, g_pre_mix[0], g_post_mix[0], g_pre_ffn[0], g_post_ffn[0], w_in[0], b_in[0], conv_w[0],
               conv_b[0], g_head_a[0], w_pa[0], lam_vecs, g_head_b[0], w_pb[0], w_out[0], w_ff1[0], w_ff2[0])
    bp = x_prompt.shape[0]
    prompt = _layer(x_prompt, jnp.zeros((bp, 0, HB, 2 * DK), F32), jnp.zeros((bp, 0, HB, DV), F32),
                    jnp.zeros((bp, HA, DH, DH), F32), jnp.zeros((bp, HA, DH), F32), jnp.zeros((bp, HA), F32),
                    jnp.zeros((bp, CONV_W - 1, 2 * D_A), F32), *weights)
    sample = _layer(x_sample, cache_k[0], cache_v[0], state_C[0], state_n[0], state_m[0], state_conv[0], *weights)
    yp, ys = prompt[0], sample[0]
    return (yp, ys) + tuple(a[None] for a in prompt[1:]) + tuple(a[None] for a in sample[1:])
```

```python
import functools
import math

import numpy as np
import jax
import jax.numpy as jnp
from jax import lax
from jax.experimental import pallas as pl
from jax.experimental.pallas import tpu as pltpu

F32 = jnp.float32
BF16 = jnp.bfloat16

D_MODEL = 1024
HA = 4
DH = 128
D_A = HA * DH
HB = 4
DK = 64
DV = 2 * DK
D_QB = HB * 2 * DK
D_VB = HB * DV
CONV_W = 4
FF = 4 * D_MODEL
NUM_BUCKETS = 32
MAX_DISTANCE = 128
CHUNK = 64
MLSTM_CHUNK = 128
EPS = 1e-6
LAMBDA_INIT = 0.8 - 0.6 * math.exp(-0.3 * 0)
NEG = -1e30
LOG2E = math.log2(math.e)
QSCALE = DK ** -0.5 * LOG2E
N_GATE_PAD = 128
ROW_TILE = 512
KEY_TILE = 512
ONES_ROWS = 16
BIAS_CORNER = 128
OUT_ROW_TILE = 256
FF_CHUNK = 1024
VMEM_LIMIT_BYTES = 56 * 1024 * 1024

NT_DIMS = (((1,), (1,)), ((), ()))
TN_DIMS = (((0,), (0,)), ((), ()))


def _const_spec(shape):
    zeros = (0,) * len(shape)
    return pl.BlockSpec(shape, lambda *_: zeros, pipeline_mode=pl.Buffered(1))


def _rms(x, g):
    return x * lax.rsqrt(jnp.mean(x * x, axis=-1, keepdims=True) + EPS) * g


def _log_sigmoid(x):
    return -(jnp.maximum(-x, 0.0) + jnp.log(1.0 + jnp.exp(-jnp.abs(x))))


def _split3(x):
    x1 = x.astype(BF16)
    r = x - x1.astype(F32)
    x2 = r.astype(BF16)
    r = r - x2.astype(F32)
    return x1, x2, r.astype(BF16)


def _mm(a, b):
    return jnp.dot(a, b, preferred_element_type=F32)


def _proj_kernel(x_ref, g_ref, wqk_ref, bqk_ref, wva_ref, bva_ref, wg_ref, bg_ref, wk_ref, bk_ref,
                 wv_ref, bv_ref, wqt_ref, bqt_ref,
                 qk_ref, va_ref, gate_ref, k_ref, k16_ref, v_ref, qt_ref, vt_ref):
    u = _rms(x_ref[...], g_ref[...]).astype(BF16)
    qk_ref[...] = _mm(u, wqk_ref[...]) + bqk_ref[...]
    va_ref[...] = _mm(u, wva_ref[...]) + bva_ref[...]
    gate_ref[...] = (_mm(u, wg_ref[...]) + bg_ref[...])[:, :2 * HA]
    k = _mm(u, wk_ref[...]) + bk_ref[...]
    k_ref[...] = k.reshape(k_ref.shape)
    k16 = k.astype(BF16)
    for h in range(HB):
        k16_ref[h] = k16[:, h * 2 * DK:(h + 1) * 2 * DK]
    v = _mm(u, wv_ref[...]) + bv_ref[...]
    v_ref[...] = v.reshape(v_ref.shape)
    qt = lax.dot_general(wqt_ref[...], u, NT_DIMS, preferred_element_type=F32) + bqt_ref[...]
    qt_ref[...] = (qt * QSCALE).astype(BF16)
    vt = v.T.astype(BF16)
    tk = vt_ref.shape[-1]
    for h in range(HB):
        for c in range(vt_ref.shape[1]):
            vt_ref[h, c, 0:DV, :] = vt[h * DV:(h + 1) * DV, c * tk:(c + 1) * tk]
            vt_ref[h, c, DV:DV + ONES_ROWS, :] = jnp.ones((ONES_ROWS, tk), BF16)


def _proj(x2d, g_pre, w_in, b_in):
    rows = x2d.shape[0]
    tm = min(rows, ROW_TILE)
    nblk = rows // tm
    tk = min(tm, KEY_TILE)
    o_qk, o_va, o_oa, o_i, o_q, o_k, o_v, o_ga = 0, 2 * D_A, 3 * D_A, 4 * D_A, 4 * D_A + 2 * HA, \
        4 * D_A + 2 * HA + D_QB, 4 * D_A + 2 * HA + 2 * D_QB, 4 * D_A + 2 * HA + 2 * D_QB + D_VB
    del o_oa, o_ga

    def cols(lo, n):
        return w_in[:, lo:lo + n].astype(BF16), b_in[lo:lo + n].reshape(1, n)

    wqk, bqk = cols(o_qk, 2 * D_A)
    wva, bva = cols(o_va, D_A)
    wg = jnp.pad(w_in[:, o_i:o_i + 2 * HA], ((0, 0), (0, N_GATE_PAD - 2 * HA))).astype(BF16)
    bg = jnp.pad(b_in[o_i:o_i + 2 * HA], (0, N_GATE_PAD - 2 * HA)).reshape(1, N_GATE_PAD)
    wk, bk = cols(o_k, D_QB)
    wv, bv = cols(o_v, D_VB)
    wqt = w_in[:, o_q:o_q + D_QB].T.astype(BF16)
    bqt = b_in[o_q:o_q + D_QB].reshape(D_QB, 1)
    consts = (g_pre.reshape(1, D_MODEL), wqk, bqk, wva, bva, wg, bg, wk, bk, wv, bv, wqt, bqt)

    def row_spec(n):
        return pl.BlockSpec((tm, n), lambda i: (i, 0))

    return pl.pallas_call(
        _proj_kernel,
        grid=(nblk,),
        in_specs=[row_spec(D_MODEL)] + [_const_spec(c.shape) for c in consts],
        out_specs=[row_spec(2 * D_A), row_spec(D_A), row_spec(2 * HA),
                   pl.BlockSpec((tm, HB, 2 * DK), lambda i: (i, 0, 0)),
                   pl.BlockSpec((HB, tm, 2 * DK), lambda i: (0, i, 0)),
                   pl.BlockSpec((tm, HB, DV), lambda i: (i, 0, 0)),
                   pl.BlockSpec((None, D_QB, tm), lambda i: (i, 0, 0)),
                   pl.BlockSpec((HB, tm // tk, DV + ONES_ROWS, tk), lambda i: (0, i, 0, 0))],
        out_shape=[jax.ShapeDtypeStruct((rows, 2 * D_A), F32), jax.ShapeDtypeStruct((rows, D_A), F32),
                   jax.ShapeDtypeStruct((rows, 2 * HA), F32), jax.ShapeDtypeStruct((rows, HB, 2 * DK), F32),
                   jax.ShapeDtypeStruct((HB, rows, 2 * DK), BF16), jax.ShapeDtypeStruct((rows, HB, DV), F32),
                   jax.ShapeDtypeStruct((nblk, D_QB, tm), BF16),
                   jax.ShapeDtypeStruct((HB, rows // tk, DV + ONES_ROWS, tk), BF16)],
        compiler_params=pltpu.CompilerParams(dimension_semantics=("parallel",),
                                             vmem_limit_bytes=VMEM_LIMIT_BYTES),
        name="proj",
    )(x2d, *consts)


def _mlstm_kernel(qk_ref, va_ref, gcol_ref, grow_ref, conv0_ref, c0_ref, n0_ref, m0_ref, cw_ref, cb_ref,
                  gha_ref, hn_ref, c_ref, n_ref, m_ref, xp_sc, q_sc, k_sc, st_sc, m_sc, *, chunk, n_chunks):
    j = pl.program_id(1)
    tb = chunk * n_chunks

    @pl.when(j == 0)
    def _():
        xp_sc[0:8, :] = conv0_ref[...]
        for h in range(HA):
            st_sc[h, :, 0:DH] = c0_ref[h].T
            st_sc[h, :, DH:2 * DH] = jnp.broadcast_to(n0_ref[h:h + 1, :], (DH, DH)).T
        m_sc[...] = m0_ref[...]

    @pl.when(j > 0)
    def _():
        xp_sc[0:8, :] = xp_sc[tb:tb + 8, :]

    xp_sc[8:8 + tb, :] = qk_ref[...]
    cw = cw_ref[...]
    conv = cb_ref[...] + xp_sc[8:8 + tb, :] * cw[CONV_W - 1:CONV_W, :]
    for t in range(CONV_W - 1):
        conv = conv + xp_sc[8 - (CONV_W - 1) + t:8 - (CONV_W - 1) + t + tb, :] * cw[t:t + 1, :]
    act = conv * jax.nn.sigmoid(conv)
    q_sc[...] = act[:, :D_A].astype(BF16)
    k_sc[...] = (act[:, D_A:] * DH ** -0.5).astype(BF16)

    rr = lax.broadcasted_iota(jnp.int32, (chunk, chunk), 0)
    cc = lax.broadcasted_iota(jnp.int32, (chunk, chunk), 1)
    tril = cc <= rr
    ltri = tril.astype(BF16)
    utri = (rr <= cc).astype(BF16)
    ones = jnp.ones((chunk, DH), F32)

    for c in range(n_chunks):
        sl = slice(c * chunk, (c + 1) * chunk)
        gc = gcol_ref[sl, :]
        ig_c = gc[:, 0:HA]
        b_c = sum(_mm(ltri, p) for p in _split3(_log_sigmoid(gc[:, HA:2 * HA])))
        g = b_c[chunk - 1:chunk, :]
        ls_c = g - b_c + ig_c
        m_old = m_sc[0:1, 0:HA]
        m_new = jnp.maximum(g + m_old, jnp.max(ls_c, axis=0, keepdims=True))
        ws_c = jnp.exp(ls_c - m_new)
        decay = jnp.exp(g + m_old - m_new)
        a_c = b_c + m_old
        m_sc[0:1, 0:HA] = m_new
        gr = grow_ref[c]
        b_r = sum(_mm(p, utri) for p in _split3(_log_sigmoid(gr[HA:2 * HA, :])))
        c_r = gr[0:HA, :] - b_r
        for h in range(HA):
            hs = slice(h * DH, (h + 1) * DH)
            dlog = jnp.where(tril, b_c[:, h:h + 1] + c_r[h:h + 1, :], NEG)
            a_h = a_c[:, h:h + 1]
            m_t = jnp.maximum(a_h, jnp.max(dlog, axis=1, keepdims=True))
            p = jnp.exp(dlog - m_t)
            w_int = jnp.exp(a_h - m_t)
            q_h = q_sc[sl, hs]
            k_h = k_sc[sl, hs]
            vaug = jnp.concatenate([va_ref[sl, hs], ones], axis=1)
            s = (p * lax.dot_general(q_h, k_h, NT_DIMS, preferred_element_type=F32)).astype(BF16)
            st = st_sc[h]
            acc = w_int * _mm(q_h, st.astype(BF16)) + _mm(s, vaug.astype(BF16))
            hh = acc[:, :DH] / jnp.maximum(jnp.abs(acc[:, DH:]), jnp.exp(-m_t))
            hn_ref[sl, hs] = _rms(hh, gha_ref[h:h + 1, :])
            wv = (ws_c[:, h:h + 1] * vaug).astype(BF16)
            st_sc[h] = decay[:, h:h + 1] * st + lax.dot_general(k_h, wv, TN_DIMS, preferred_element_type=F32)

    @pl.when(j == pl.num_programs(1) - 1)
    def _():
        for h in range(HA):
            st = st_sc[h]
            c_ref[h] = st[:, :DH].T
            n_ref[h:h + 1, :] = st[:, DH:].T[0:1, :]
        m_ref[...] = m_sc[...]


def _mlstm(qk_a, v_a, gates, conv0, c0, n0, m0, conv_w, conv_b, g_head_a):
    bsz, t, _ = qk_a.shape
    chunk = min(t, MLSTM_CHUNK)
    n_chunks = min(t // chunk, ROW_TILE // chunk)
    tb = chunk * n_chunks
    nblk = t // tb
    grow = gates.reshape(bsz, t // chunk, chunk, 2 * HA).transpose(0, 1, 3, 2)
    conv0p = jnp.pad(conv0, ((0, 0), (8 - (CONV_W - 1), 0), (0, 0)))
    m0p = jnp.pad(m0, ((0, 0), (0, 128 - HA)))[:, None, :] * jnp.ones((1, 8, 1), F32)
    kern = functools.partial(_mlstm_kernel, chunk=chunk, n_chunks=n_chunks)
    return pl.pallas_call(
        kern,
        grid=(bsz, nblk),
        in_specs=[pl.BlockSpec((None, tb, 2 * D_A), lambda b, j: (b, j, 0)),
                  pl.BlockSpec((None, tb, D_A), lambda b, j: (b, j, 0)),
                  pl.BlockSpec((None, tb, 2 * HA), lambda b, j: (b, j, 0)),
                  pl.BlockSpec((None, n_chunks, 2 * HA, chunk), lambda b, j: (b, j, 0, 0)),
                  pl.BlockSpec((None, 8, 2 * D_A), lambda b, j: (b, 0, 0)),
                  pl.BlockSpec((None, HA, DH, DH), lambda b, j: (b, 0, 0, 0)),
                  pl.BlockSpec((None, HA, DH), lambda b, j: (b, 0, 0)),
                  pl.BlockSpec((None, 8, 128), lambda b, j: (b, 0, 0)),
                  _const_spec((CONV_W, 2 * D_A)), _const_spec((1, 2 * D_A)), _const_spec((HA, DH))],
        out_specs=[pl.BlockSpec((None, tb, D_A), lambda b, j: (b, j, 0)),
                   pl.BlockSpec((None, HA, DH, DH), lambda b, j: (b, 0, 0, 0)),
                   pl.BlockSpec((None, HA, DH), lambda b, j: (b, 0, 0)),
                   pl.BlockSpec((None, 8, 128), lambda b, j: (b, 0, 0))],
        out_shape=[jax.ShapeDtypeStruct((bsz, t, D_A), F32), jax.ShapeDtypeStruct((bsz, HA, DH, DH), F32),
                   jax.ShapeDtypeStruct((bsz, HA, DH), F32), jax.ShapeDtypeStruct((bsz, 8, 128), F32)],
        scratch_shapes=[pltpu.VMEM((tb + 8, 2 * D_A), F32), pltpu.VMEM((tb, D_A), BF16),
                        pltpu.VMEM((tb, D_A), BF16), pltpu.VMEM((HA, DH, 2 * DH), F32),
                        pltpu.VMEM((8, 128), F32)],
        compiler_params=pltpu.CompilerParams(dimension_semantics=("parallel", "arbitrary"),
                                             vmem_limit_bytes=VMEM_LIMIT_BYTES),
        name="mlstm",
    )(qk_a, v_a, gates, grow, conv0p, c0, n0, m0p, conv_w, conv_b.reshape(1, 2 * D_A), g_head_a)


def _attn_kernel(qt_ref, k_ref, vt_ref, bd_ref, bp_ref, ghb_ref, lam_ref, o_ref,
                 qpad_sc, m_sc, acc_sc, s_sc, mx_sc, *, tq, tk, n_qblk):
    i = pl.program_id(1)
    lv = lam_ref[...]
    lam = (jnp.exp(jnp.sum(lv[0:1, :] * lv[1:2, :], axis=1, keepdims=True))
           - jnp.exp(jnp.sum(lv[2:3, :] * lv[3:4, :], axis=1, keepdims=True)) + LAMBDA_INIT)

    first_map = lax.broadcasted_iota(jnp.int32, (2 * DK, tq), 0) < DK
    for h in range(HB):
        qh = qt_ref[h * 2 * DK:(h + 1) * 2 * DK, :]
        zero = jnp.zeros_like(qh)
        qpad_sc[h, :, 0:tq] = jnp.where(first_map, qh, zero)
        qpad_sc[h, :, tq:2 * tq] = jnp.where(first_map, zero, qh)
    m_sc[...] = jnp.full(m_sc.shape, NEG, F32)
    acc_sc[...] = jnp.zeros(acc_sc.shape, F32)

    corner = bp_ref.shape[-1]

    def logits(jb, h):
        row0 = pl.multiple_of(jb * tk, tk)
        return _mm(k_ref[h, pl.ds(row0, tk), :], qpad_sc[h])

    def put(slot, s):
        s_sc[slot] = s
        mx_sc[slot] = jnp.max(s, axis=0, keepdims=True)

    def stage1(jb, h, slot, is_prev):
        put(slot, logits(jb, h))
        if is_prev is not None:
            bias = jnp.where(is_prev, bp_ref[h], 0.0)
            for c0 in (0, tq):
                sc = s_sc[slot, tk - corner:tk, c0:c0 + corner] + bias
                s_sc[slot, tk - corner:tk, c0:c0 + corner] = sc
                mx_sc[slot, :, c0:c0 + corner] = jnp.maximum(mx_sc[slot, :, c0:c0 + corner],
                                                             jnp.max(sc, axis=0, keepdims=True))

    def diag_bias(h):
        return jnp.concatenate([bd_ref[h], bd_ref[h]], axis=1)

    def stage2(jb, h, slot):
        m_old = m_sc[h]
        m_new = jnp.maximum(m_old, mx_sc[slot])
        p = jnp.exp2(s_sc[slot] - m_new).astype(BF16)
        acc_sc[h] = jnp.exp2(m_old - m_new) * acc_sc[h] + _mm(vt_ref[h, jb], p)
        m_sc[h] = m_new

    if n_qblk > 1:
        stage1(0, 0, 0, i == 1)

        def before_diag(jb, carry):
            is_prev = jb == i - 1
            stage1(jb, 1, 1, is_prev)
            stage2(jb, 0, 0)
            stage1(jb, 2, 0, is_prev)
            stage2(jb, 1, 1)
            stage1(jb, 3, 1, is_prev)
            stage2(jb, 2, 0)
            stage1(jb + 1, 0, 0, jb == i - 2)
            stage2(jb, 3, 1)
            return carry
        lax.fori_loop(0, i, before_diag, 0)
    else:
        stage1(0, 0, 0, None)

    put(0, s_sc[0] + diag_bias(0))
    put(1, logits(i, 1) + diag_bias(1))
    stage2(i, 0, 0)
    put(0, logits(i, 2) + diag_bias(2))
    stage2(i, 1, 1)
    put(1, logits(i, 3) + diag_bias(3))
    stage2(i, 2, 0)
    stage2(i, 3, 1)

    for h in range(HB):
        on = acc_sc[h, 0:DV, :] / acc_sc[h, DV:DV + 1, :]
        o = on[:, 0:tq] - lam * on[:, tq:2 * tq]
        o = o * lax.rsqrt(jnp.mean(o * o, axis=0, keepdims=True) + EPS) * (ghb_ref[...] * (1.0 - LAMBDA_INIT))
        o_ref[:, h * DV:(h + 1) * DV] = o.T.astype(o_ref.dtype)


def _bucket(rel):
    half, max_exact = NUM_BUCKETS // 2, NUM_BUCKETS // 4
    n = np.abs(rel).astype(np.int64)
    sq = np.maximum(n * n // (max_exact * max_exact), 1)
    large = max_exact + np.floor(np.log2(sq.astype(np.float64)) + 1e-9).astype(np.int64)
    large = np.minimum(large, half - 1)
    return np.where(rel > 0, half, 0) + np.where(n < max_exact, n, large)


def _bias_tiles(rel_bias, rel0, nk, tq):
    table = (rel_bias - rel_bias[NUM_BUCKETS // 2 - 1:NUM_BUCKETS // 2, :]) * LOG2E
    half, max_exact = NUM_BUCKETS // 2, NUM_BUCKETS // 4
    dist = np.arange(4 * MAX_DISTANCE)
    buckets = _bucket(-dist)
    rel = rel0 + lax.broadcasted_iota(jnp.int32, (nk, tq), 0) - lax.broadcasted_iota(jnp.int32, (nk, tq), 1)
    n = jnp.abs(rel)
    large = jnp.full((nk, tq), max_exact, jnp.int32)
    for b in range(max_exact + 1, half):
        large = large + (n >= int(dist[buckets >= b][0])).astype(jnp.int32)
    bucket = jnp.where(rel > 0, half, 0) + jnp.where(n < max_exact, n, large)
    tile = jnp.zeros((HB, nk, tq), F32)
    for b in range(NUM_BUCKETS):
        tile = jnp.where((bucket == b)[None], table[b][:, None, None], tile)
    return tile


def _attn(qt_blk, k16, vt_blk, bias_diag, bias_prev, g_head_b, lam_vecs, bsz, t, tq, tk):
    n_qblk = t // tq
    nk = k16.shape[2]
    n_kblk = nk // tk
    assert n_qblk == 1 or tq == tk
    kern = functools.partial(_attn_kernel, tq=tq, tk=tk, n_qblk=n_qblk)
    return pl.pallas_call(
        kern,
        grid=(bsz, n_qblk),
        in_specs=[pl.BlockSpec((None, D_QB, tq), lambda b, i: (b * n_qblk + i, 0, 0)),
                  pl.BlockSpec((HB, None, nk, 2 * DK), lambda b, i: (0, b, 0, 0), pipeline_mode=pl.Buffered(1)),
                  pl.BlockSpec((HB, n_kblk, DV + ONES_ROWS, tk), lambda b, i: (0, b, 0, 0),
                               pipeline_mode=pl.Buffered(1)),
                  _const_spec(bias_diag.shape), _const_spec(bias_prev.shape),
                  _const_spec((DV, 1)), _const_spec((4, DK))],
        out_specs=pl.BlockSpec((None, tq, D_VB), lambda b, i: (b, i, 0)),
        out_shape=jax.ShapeDtypeStruct((bsz, t, D_VB), BF16),
        scratch_shapes=[pltpu.VMEM((HB, 2 * DK, 2 * tq), BF16), pltpu.VMEM((HB, 1, 2 * tq), F32),
                        pltpu.VMEM((HB, DV + ONES_ROWS, 2 * tq), F32),
                        pltpu.VMEM((2, tk, 2 * tq), F32), pltpu.VMEM((2, 1, 2 * tq), F32)],
        compiler_params=pltpu.CompilerParams(dimension_semantics=("parallel", "arbitrary"),
                                             vmem_limit_bytes=VMEM_LIMIT_BYTES),
        name="attn",
    )(qt_blk, k16, vt_blk, bias_diag, bias_prev, g_head_b.reshape(DV, 1), lam_vecs)


def _out_kernel(x_ref, hn_ref, ob_ref, gpre_ref, gpost_ref, gpref_ref, gpostf_ref, woa_ref, boa_ref,
                wga_ref, bga_ref, wgb_ref, bgb_ref, wpa_ref, wpb_ref, wout_ref, wff1_ref, wff2_ref, o_ref):
    x = x_ref[...]
    u = _rms(x, gpre_ref[...]).astype(BF16)
    h_a = (hn_ref[...] * jax.nn.sigmoid(_mm(u, woa_ref[...]) + boa_ref[...])).astype(BF16)
    y_a = _mm(h_a, wpa_ref[...])
    y_b = _mm(ob_ref[...], wpb_ref[...])
    gate_a = jax.nn.sigmoid(_mm(u, wga_ref[...]) + bga_ref[...])
    gate_b = jax.nn.sigmoid(_mm(u, wgb_ref[...]) + bgb_ref[...])
    mix = _mm((gate_a * y_a + gate_b * y_b).astype(BF16), wout_ref[...])
    x1 = x + _rms(mix, gpost_ref[...])
    f = _rms(x1, gpref_ref[...]).astype(BF16)
    hf = jnp.zeros_like(x1)
    for c in range(FF // FF_CHUNK):
        mid = jnp.maximum(_mm(f, wff1_ref[:, c * FF_CHUNK:(c + 1) * FF_CHUNK]), 0.0)
        hf = hf + _mm((mid * mid).astype(BF16), wff2_ref[c * FF_CHUNK:(c + 1) * FF_CHUNK, :])
    o_ref[...] = x1 + _rms(hf, gpostf_ref[...])


def _out(x2d, hn2d, ob2d, g_pre_mix, g_post_mix, g_pre_ffn, g_post_ffn, w_in, b_in, w_pa, w_pb, w_out,
         w_ff1, w_ff2):
    rows = x2d.shape[0]
    tm = min(rows, OUT_ROW_TILE)
    o_oa = 3 * D_A
    o_ga = 4 * D_A + 2 * HA + 2 * D_QB + D_VB
    o_gb = o_ga + D_MODEL

    def cols(lo, n):
        return w_in[:, lo:lo + n].astype(BF16), b_in[lo:lo + n].reshape(1, n)

    woa, boa = cols(o_oa, D_A)
    wga, bga = cols(o_ga, D_MODEL)
    wgb, bgb = cols(o_gb, D_MODEL)
    consts = (g_pre_mix.reshape(1, -1), g_post_mix.reshape(1, -1), g_pre_ffn.reshape(1, -1),
              g_post_ffn.reshape(1, -1), woa, boa, wga, bga, wgb, bgb, w_pa.astype(BF16), w_pb.astype(BF16),
              w_out.astype(BF16), w_ff1.astype(BF16), w_ff2.astype(BF16))

    def row_spec(n):
        return pl.BlockSpec((tm, n), lambda i: (i, 0))

    return pl.pallas_call(
        _out_kernel,
        grid=(rows // tm,),
        in_specs=[row_spec(D_MODEL), row_spec(D_A), row_spec(D_VB)] + [_const_spec(c.shape) for c in consts],
        out_specs=row_spec(D_MODEL),
        out_shape=jax.ShapeDtypeStruct((rows, D_MODEL), F32),
        compiler_params=pltpu.CompilerParams(dimension_semantics=("parallel",),
                                             vmem_limit_bytes=VMEM_LIMIT_BYTES),
        name="out",
    )(x2d, hn2d, ob2d, *consts)


def _layer(x, k_past, v_past, c0, n0, m0, conv0, rel_bias, g_pre_mix, g_post_mix, g_pre_ffn, g_post_ffn,
           w_in, b_in, conv_w, conv_b, g_head_a, w_pa, lam_vecs, g_head_b, w_pb, w_out, w_ff1, w_ff2):
    bsz, t, _ = x.shape
    past = k_past.shape[1]
    x2d = x.reshape(bsz * t, D_MODEL)
    qk_a, v_a, gates, k_new, k16, v_new, qt_blk, vt_blk = _proj(x2d, g_pre_mix, w_in, b_in)

    hn, c_new, n_new, m_new = _mlstm(qk_a.reshape(bsz, t, 2 * D_A), v_a.reshape(bsz, t, D_A),
                                     gates.reshape(bsz, t, 2 * HA), conv0, c0, n0, m0, conv_w, conv_b, g_head_a)

    if past == 0:
        tq = tk = min(t, ROW_TILE, KEY_TILE)
        s_idx = lax.broadcasted_iota(jnp.int32, (tk, tq), 0)
        t_idx = lax.broadcasted_iota(jnp.int32, (tk, tq), 1)
        allowed = (s_idx // CHUNK) <= (t_idx // CHUNK)
        bias_diag = jnp.where(allowed[None], _bias_tiles(rel_bias, 0, tk, tq), NEG)
        corner = min(BIAS_CORNER, tk)
        bias_prev = _bias_tiles(rel_bias, -corner, corner, corner)
        k16 = k16.reshape(HB, bsz, t, 2 * DK)
    else:
        tq = t
        nk = past + t
        tk = -(-nk // 128) * 128
        k16 = jnp.concatenate([jnp.transpose(k_past.astype(BF16), (2, 0, 1, 3)), k16.reshape(HB, bsz, t, 2 * DK),
                               jnp.zeros((HB, bsz, tk - nk, 2 * DK), BF16)], axis=2)
        vt_past = jnp.transpose(v_past.astype(BF16), (2, 0, 3, 1))
        vt_new = jnp.transpose(vt_blk[:, :, 0:DV, :], (0, 2, 1, 3)).reshape(HB, DV, bsz, t)
        vt_new = jnp.transpose(vt_new, (0, 2, 1, 3))
        vt_blk = jnp.concatenate([vt_past, vt_new, jnp.zeros((HB, bsz, DV, tk - nk), BF16)], axis=3)
        vt_blk = jnp.concatenate([vt_blk, jnp.ones((HB, bsz, ONES_ROWS, tk), BF16)], axis=2)
        qt_blk = jnp.transpose(qt_blk.reshape(D_QB, bsz, t), (1, 0, 2))
        s_idx = lax.broadcasted_iota(jnp.int32, (tk, tq), 0)
        t_idx = lax.broadcasted_iota(jnp.int32, (tk, tq), 1)
        allowed = (s_idx < nk) & ((s_idx // CHUNK) <= ((past + t_idx) // CHUNK))
        bias_diag = jnp.where(allowed[None], _bias_tiles(rel_bias, -past, tk, tq), NEG)
        bias_prev = bias_diag[:, 0:8, 0:8]
    o_b = _attn(qt_blk, k16, vt_blk, bias_diag, bias_prev, g_head_b, lam_vecs, bsz, t, tq, tk)

    y = _out(x2d, hn.reshape(bsz * t, D_A), o_b.reshape(bsz * t, D_VB), g_pre_mix, g_post_mix, g_pre_ffn,
             g_post_ffn, w_in, b_in, w_pa, w_pb, w_out, w_ff1, w_ff2)
    return (y.reshape(bsz, t, D_MODEL), k_new.reshape(bsz, t, HB, 2 * DK), v_new.reshape(bsz, t, HB, DV),
            c_new, n_new, m_new[:, 0, :HA], qk_a.reshape(bsz, t, 2 * D_A)[:, t - (CONV_W - 1):, :])


def kernel(x_prompt, x_sample, cache_k, cache_v, state_C, state_n, state_m, state_conv, rel_bias, g_pre_mix,
           g_post_mix, g_pre_ffn, g_post_ffn, w_in, b_in, conv_w, conv_b, g_head_a, w_pa, lambda_q1, lambda_k1,
           lambda_q2, lambda_k2, g_head_b, w_pb, w_out, w_ff1, w_ff2):
    lam_vecs = jnp.concatenate([lambda_q1, lambda_k1, lambda_q2, lambda_k2], axis=0)
    weights = (rel_bias, g_pre_mix[0], g_post_mix[0], g_pre_ffn[0], g_post_ffn[0], w_in[0], b_in[0], conv_w[0],
               conv_b[0], g_head_a[0], w_pa[0], lam_vecs, g_head_b[0], w_pb[0], w_out[0], w_ff1[0], w_ff2[0])
    bp = x_prompt.shape[0]
    prompt = _layer(x_prompt, jnp.zeros((bp, 0, HB, 2 * DK), F32), jnp.zeros((bp, 0, HB, DV), F32),
                    jnp.zeros((bp, HA, DH, DH), F32), jnp.zeros((bp, HA, DH), F32), jnp.zeros((bp, HA), F32),
                    jnp.zeros((bp, CONV_W - 1, 2 * D_A), F32), *weights)
    sample = _layer(x_sample, cache_k[0], cache_v[0], state_C[0], state_n[0], state_m[0], state_conv[0], *weights)
    yp, ys = prompt[0], sample[0]
    return (yp, ys) + tuple(a[None] for a in prompt[1:]) + tuple(a[None] for a in sample[1:])
```

```python
import functools
import math

import numpy as np
import jax
import jax.numpy as jnp
from jax import lax
from jax.experimental import pallas as pl
from jax.experimental.pallas import tpu as pltpu

F32 = jnp.float32
BF16 = jnp.bfloat16

D_MODEL = 1024
HA = 4
DH = 128
D_A = HA * DH
HB = 4
DK = 64
DV = 2 * DK
D_QB = HB * 2 * DK
D_VB = HB * DV
CONV_W = 4
FF = 4 * D_MODEL
NUM_BUCKETS = 32
MAX_DISTANCE = 128
CHUNK = 64
MLSTM_CHUNK = 128
EPS = 1e-6
LAMBDA_INIT = 0.8 - 0.6 * math.exp(-0.3 * 0)
NEG = -1e30
LOG2E = math.log2(math.e)
QSCALE = DK ** -0.5 * LOG2E
N_GATE_PAD = 128
ROW_TILE = 512
KEY_TILE = 512
ONES_ROWS = 16
BIAS_CORNER = 128
OUT_ROW_TILE = 256
FF_CHUNK = 1024
VMEM_LIMIT_BYTES = 56 * 1024 * 1024

NT_DIMS = (((1,), (1,)), ((), ()))
TN_DIMS = (((0,), (0,)), ((), ()))


def _const_spec(shape):
    zeros = (0,) * len(shape)
    return pl.BlockSpec(shape, lambda *_: zeros, pipeline_mode=pl.Buffered(1))


def _rms(x, g):
    return x * lax.rsqrt(jnp.mean(x * x, axis=-1, keepdims=True) + EPS) * g


def _log_sigmoid(x):
    return -(jnp.maximum(-x, 0.0) + jnp.log(1.0 + jnp.exp(-jnp.abs(x))))


def _split3(x):
    x1 = x.astype(BF16)
    r = x - x1.astype(F32)
    x2 = r.astype(BF16)
    r = r - x2.astype(F32)
    return x1, x2, r.astype(BF16)


def _mm(a, b):
    return jnp.dot(a, b, preferred_element_type=F32)


def _proj_kernel(x_ref, g_ref, wqk_ref, bqk_ref, wva_ref, bva_ref, wg_ref, bg_ref, wk_ref, bk_ref,
                 wv_ref, bv_ref, wqt_ref, bqt_ref,
                 qk_ref, va_ref, gate_ref, k_ref, k16_ref, v_ref, qt_ref, vt_ref):
    u = _rms(x_ref[...], g_ref[...]).astype(BF16)
    qk_ref[...] = _mm(u, wqk_ref[...]) + bqk_ref[...]
    va_ref[...] = _mm(u, wva_ref[...]) + bva_ref[...]
    gate_ref[...] = (_mm(u, wg_ref[...]) + bg_ref[...])[:, :2 * HA]
    k = _mm(u, wk_ref[...]) + bk_ref[...]
    k_ref[...] = k.reshape(k_ref.shape)
    k16 = k.astype(BF16)
    for h in range(HB):
        k16_ref[h] = k16[:, h * 2 * DK:(h + 1) * 2 * DK]
    v = _mm(u, wv_ref[...]) + bv_ref[...]
    v_ref[...] = v.reshape(v_ref.shape)
    qt = lax.dot_general(wqt_ref[...], u, NT_DIMS, preferred_element_type=F32) + bqt_ref[...]
    qt_ref[...] = (qt * QSCALE).astype(BF16)
    vt = v.T.astype(BF16)
    tk = vt_ref.shape[-1]
    for h in range(HB):
        for c in range(vt_ref.shape[1]):
            vt_ref[h, c, 0:DV, :] = vt[h * DV:(h + 1) * DV, c * tk:(c + 1) * tk]
            vt_ref[h, c, DV:DV + ONES_ROWS, :] = jnp.ones((ONES_ROWS, tk), BF16)


def _proj(x2d, g_pre, w_in, b_in):
    rows = x2d.shape[0]
    tm = min(rows, ROW_TILE)
    nblk = rows // tm
    tk = min(tm, KEY_TILE)
    o_qk, o_va, o_oa, o_i, o_q, o_k, o_v, o_ga = 0, 2 * D_A, 3 * D_A, 4 * D_A, 4 * D_A + 2 * HA, \
        4 * D_A + 2 * HA + D_QB, 4 * D_A + 2 * HA + 2 * D_QB, 4 * D_A + 2 * HA + 2 * D_QB + D_VB
    del o_oa, o_ga

    def cols(lo, n):
        return w_in[:, lo:lo + n].astype(BF16), b_in[lo:lo + n].reshape(1, n)

    wqk, bqk = cols(o_qk, 2 * D_A)
    wva, bva = cols(o_va, D_A)
    wg = jnp.pad(w_in[:, o_i:o_i + 2 * HA], ((0, 0), (0, N_GATE_PAD - 2 * HA))).astype(BF16)
    bg = jnp.pad(b_in[o_i:o_i + 2 * HA], (0, N_GATE_PAD - 2 * HA)).reshape(1, N_GATE_PAD)
    wk, bk = cols(o_k, D_QB)
    wv, bv = cols(o_v, D_VB)
    wqt = w_in[:, o_q:o_q + D_QB].T.astype(BF16)
    bqt = b_in[o_q:o_q + D_QB].reshape(D_QB, 1)
    consts = (g_pre.reshape(1, D_MODEL), wqk, bqk, wva, bva, wg, bg, wk, bk, wv, bv, wqt, bqt)

    def row_spec(n):
        return pl.BlockSpec((tm, n), lambda i: (i, 0))

    return pl.pallas_call(
        _proj_kernel,
        grid=(nblk,),
        in_specs=[row_spec(D_MODEL)] + [_const_spec(c.shape) for c in consts],
        out_specs=[row_spec(2 * D_A), row_spec(D_A), row_spec(2 * HA),
                   pl.BlockSpec((tm, HB, 2 * DK), lambda i: (i, 0, 0)),
                   pl.BlockSpec((HB, tm, 2 * DK), lambda i: (0, i, 0)),
                   pl.BlockSpec((tm, HB, DV), lambda i: (i, 0, 0)),
                   pl.BlockSpec((None, D_QB, tm), lambda i: (i, 0, 0)),
                   pl.BlockSpec((HB, tm // tk, DV + ONES_ROWS, tk), lambda i: (0, i, 0, 0))],
        out_shape=[jax.ShapeDtypeStruct((rows, 2 * D_A), F32), jax.ShapeDtypeStruct((rows, D_A), F32),
                   jax.ShapeDtypeStruct((rows, 2 * HA), F32), jax.ShapeDtypeStruct((rows, HB, 2 * DK), F32),
                   jax.ShapeDtypeStruct((HB, rows, 2 * DK), BF16), jax.ShapeDtypeStruct((rows, HB, DV), F32),
                   jax.ShapeDtypeStruct((nblk, D_QB, tm), BF16),
                   jax.ShapeDtypeStruct((HB, rows // tk, DV + ONES_ROWS, tk), BF16)],
        compiler_params=pltpu.CompilerParams(dimension_semantics=("parallel",),
                                             vmem_limit_bytes=VMEM_LIMIT_BYTES),
        name="proj",
    )(x2d, *consts)


def _mlstm_kernel(qk_ref, va_ref, gcol_ref, grow_ref, conv0_ref, c0_ref, n0_ref, m0_ref, cw_ref, cb_ref,
                  h_ref, c_ref, n_ref, m_ref, xp_sc, q_sc, k_sc, st_sc, m_sc, *, chunk, n_chunks):
    j = pl.program_id(1)
    tb = chunk * n_chunks

    @pl.when(j == 0)
    def _():
        xp_sc[0:8, :] = conv0_ref[...]
        for h in range(HA):
            st_sc[h, :, 0:DH] = c0_ref[h].T
            st_sc[h, :, DH:2 * DH] = jnp.broadcast_to(n0_ref[h:h + 1, :], (DH, DH)).T
        m_sc[...] = m0_ref[...]

    @pl.when(j > 0)
    def _():
        xp_sc[0:8, :] = xp_sc[tb:tb + 8, :]

    xp_sc[8:8 + tb, :] = qk_ref[...]
    cw = cw_ref[...]
    conv = cb_ref[...] + xp_sc[8:8 + tb, :] * cw[CONV_W - 1:CONV_W, :]
    for t in range(CONV_W - 1):
        conv = conv + xp_sc[8 - (CONV_W - 1) + t:8 - (CONV_W - 1) + t + tb, :] * cw[t:t + 1, :]
    act = conv * jax.nn.sigmoid(conv)
    q_sc[...] = act[:, :D_A].astype(BF16)
    k_sc[...] = (act[:, D_A:] * DH ** -0.5).astype(BF16)

    rr = lax.broadcasted_iota(jnp.int32, (chunk, chunk), 0)
    cc = lax.broadcasted_iota(jnp.int32, (chunk, chunk), 1)
    tril = cc <= rr
    ltri = tril.astype(BF16)
    utri = (rr <= cc).astype(BF16)
    ones = jnp.ones((chunk, DH), F32)

    for c in range(n_chunks):
        sl = slice(c * chunk, (c + 1) * chunk)
        gc = gcol_ref[sl, :]
        ig_c = gc[:, 0:HA]
        b_c = sum(_mm(ltri, p) for p in _split3(_log_sigmoid(gc[:, HA:2 * HA])))
        g = b_c[chunk - 1:chunk, :]
        ls_c = g - b_c + ig_c
        m_old = m_sc[0:1, 0:HA]
        m_new = jnp.maximum(g + m_old, jnp.max(ls_c, axis=0, keepdims=True))
        ws_c = jnp.exp(ls_c - m_new)
        decay = jnp.exp(g + m_old - m_new)
        a_c = b_c + m_old
        run = ig_c - b_c
        shift = 1
        while shift < chunk:
            run = jnp.maximum(run, jnp.concatenate([jnp.full((shift, HA), NEG, F32), run[:chunk - shift, :]], axis=0))
            shift *= 2
        mt_c = jnp.maximum(a_c, b_c + run)
        m_sc[0:1, 0:HA] = m_new
        gr = grow_ref[c]
        b_r = sum(_mm(p, utri) for p in _split3(_log_sigmoid(gr[HA:2 * HA, :])))
        c_r = gr[0:HA, :] - b_r
        for h in range(HA):
            hs = slice(h * DH, (h + 1) * DH)
            dlog = jnp.where(tril, b_c[:, h:h + 1] + c_r[h:h + 1, :], NEG)
            a_h = a_c[:, h:h + 1]
            m_t = mt_c[:, h:h + 1]
            p = jnp.exp(dlog - m_t)
            w_int = jnp.exp(a_h - m_t)
            q_h = q_sc[sl, hs]
            k_h = k_sc[sl, hs]
            vaug = jnp.concatenate([va_ref[sl, hs], ones], axis=1)
            s = (p * lax.dot_general(q_h, k_h, NT_DIMS, preferred_element_type=F32)).astype(BF16)
            st = st_sc[h]
            acc = w_int * _mm(q_h, st.astype(BF16)) + _mm(s, vaug.astype(BF16))
            hh = acc[:, :DH] / jnp.maximum(jnp.abs(acc[:, DH:]), jnp.exp(-m_t))
            h_ref[sl, hs] = hh
            wv = (ws_c[:, h:h + 1] * vaug).astype(BF16)
            st_sc[h] = decay[:, h:h + 1] * st + lax.dot_general(k_h, wv, TN_DIMS, preferred_element_type=F32)

    @pl.when(j == pl.num_programs(1) - 1)
    def _():
        for h in range(HA):
            st = st_sc[h]
            c_ref[h] = st[:, :DH].T
            n_ref[h:h + 1, :] = st[:, DH:].T[0:1, :]
        m_ref[...] = m_sc[...]


def _mlstm(qk_a, v_a, gates, conv0, c0, n0, m0, conv_w, conv_b):
    bsz, t, _ = qk_a.shape
    chunk = min(t, MLSTM_CHUNK)
    n_chunks = min(t // chunk, ROW_TILE // chunk)
    tb = chunk * n_chunks
    nblk = t // tb
    grow = gates.reshape(bsz, t // chunk, chunk, 2 * HA).transpose(0, 1, 3, 2)
    conv0p = jnp.pad(conv0, ((0, 0), (8 - (CONV_W - 1), 0), (0, 0)))
    m0p = jnp.pad(m0, ((0, 0), (0, 128 - HA)))[:, None, :] * jnp.ones((1, 8, 1), F32)
    kern = functools.partial(_mlstm_kernel, chunk=chunk, n_chunks=n_chunks)
    return pl.pallas_call(
        kern,
        grid=(bsz, nblk),
        in_specs=[pl.BlockSpec((None, tb, 2 * D_A), lambda b, j: (b, j, 0)),
                  pl.BlockSpec((None, tb, D_A), lambda b, j: (b, j, 0)),
                  pl.BlockSpec((None, tb, 2 * HA), lambda b, j: (b, j, 0)),
                  pl.BlockSpec((None, n_chunks, 2 * HA, chunk), lambda b, j: (b, j, 0, 0)),
                  pl.BlockSpec((None, 8, 2 * D_A), lambda b, j: (b, 0, 0)),
                  pl.BlockSpec((None, HA, DH, DH), lambda b, j: (b, 0, 0, 0)),
                  pl.BlockSpec((None, HA, DH), lambda b, j: (b, 0, 0)),
                  pl.BlockSpec((None, 8, 128), lambda b, j: (b, 0, 0)),
                  _const_spec((CONV_W, 2 * D_A)), _const_spec((1, 2 * D_A))],
        out_specs=[pl.BlockSpec((None, tb, D_A), lambda b, j: (b, j, 0)),
                   pl.BlockSpec((None, HA, DH, DH), lambda b, j: (b, 0, 0, 0)),
                   pl.BlockSpec((None, HA, DH), lambda b, j: (b, 0, 0)),
                   pl.BlockSpec((None, 8, 128), lambda b, j: (b, 0, 0))],
        out_shape=[jax.ShapeDtypeStruct((bsz, t, D_A), F32), jax.ShapeDtypeStruct((bsz, HA, DH, DH), F32),
                   jax.ShapeDtypeStruct((bsz, HA, DH), F32), jax.ShapeDtypeStruct((bsz, 8, 128), F32)],
        scratch_shapes=[pltpu.VMEM((tb + 8, 2 * D_A), F32), pltpu.VMEM((tb, D_A), BF16),
                        pltpu.VMEM((tb, D_A), BF16), pltpu.VMEM((HA, DH, 2 * DH), F32),
                        pltpu.VMEM((8, 128), F32)],
        compiler_params=pltpu.CompilerParams(dimension_semantics=("parallel", "arbitrary"),
                                             vmem_limit_bytes=VMEM_LIMIT_BYTES),
        name="mlstm",
    )(qk_a, v_a, gates, grow, conv0p, c0, n0, m0p, conv_w, conv_b.reshape(1, 2 * D_A))


def _attn_kernel(qt_ref, k_ref, vt_ref, bd_ref, bp_ref, ghb_ref, lam_ref, o_ref,
                 qpad_sc, m_sc, acc_sc, s_sc, mx_sc, *, tq, tk, n_qblk):
    i = pl.program_id(1)
    lv = lam_ref[...]
    lam = (jnp.exp(jnp.sum(lv[0:1, :] * lv[1:2, :], axis=1, keepdims=True))
           - jnp.exp(jnp.sum(lv[2:3, :] * lv[3:4, :], axis=1, keepdims=True)) + LAMBDA_INIT)

    first_map = lax.broadcasted_iota(jnp.int32, (2 * DK, tq), 0) < DK
    for h in range(HB):
        qh = qt_ref[h * 2 * DK:(h + 1) * 2 * DK, :]
        zero = jnp.zeros_like(qh)
        qpad_sc[h, :, 0:tq] = jnp.where(first_map, qh, zero)
        qpad_sc[h, :, tq:2 * tq] = jnp.where(first_map, zero, qh)
    m_sc[...] = jnp.full(m_sc.shape, NEG, F32)
    acc_sc[...] = jnp.zeros(acc_sc.shape, F32)

    corner = bp_ref.shape[-1]

    def logits(jb, h):
        row0 = pl.multiple_of(jb * tk, tk)
        return _mm(k_ref[h, pl.ds(row0, tk), :], qpad_sc[h])

    def put(slot, s):
        s_sc[slot] = s
        mx_sc[slot] = jnp.max(s, axis=0, keepdims=True)

    def stage1(jb, h, slot, is_prev):
        put(slot, logits(jb, h))
        if is_prev is not None:
            bias = jnp.where(is_prev, bp_ref[h], 0.0)
            for c0 in (0, tq):
                sc = s_sc[slot, tk - corner:tk, c0:c0 + corner] + bias
                s_sc[slot, tk - corner:tk, c0:c0 + corner] = sc
                mx_sc[slot, :, c0:c0 + corner] = jnp.maximum(mx_sc[slot, :, c0:c0 + corner],
                                                             jnp.max(sc, axis=0, keepdims=True))

    def diag_bias(h):
        return jnp.concatenate([bd_ref[h], bd_ref[h]], axis=1)

    def stage2(jb, h, slot):
        m_old = m_sc[h]
        m_new = jnp.maximum(m_old, mx_sc[slot])
        p = jnp.exp2(s_sc[slot] - m_new).astype(BF16)
        acc_sc[h] = jnp.exp2(m_old - m_new) * acc_sc[h] + _mm(vt_ref[h, jb], p)
        m_sc[h] = m_new

    if n_qblk > 1:
        stage1(0, 0, 0, i == 1)

        def before_diag(jb, carry):
            is_prev = jb == i - 1
            stage1(jb, 1, 1, is_prev)
            stage2(jb, 0, 0)
            stage1(jb, 2, 0, is_prev)
            stage2(jb, 1, 1)
            stage1(jb, 3, 1, is_prev)
            stage2(jb, 2, 0)
            stage1(jb + 1, 0, 0, jb == i - 2)
            stage2(jb, 3, 1)
            return carry
        lax.fori_loop(0, i, before_diag, 0)
    else:
        stage1(0, 0, 0, None)

    put(0, s_sc[0] + diag_bias(0))
    put(1, logits(i, 1) + diag_bias(1))
    stage2(i, 0, 0)
    put(0, logits(i, 2) + diag_bias(2))
    stage2(i, 1, 1)
    put(1, logits(i, 3) + diag_bias(3))
    stage2(i, 2, 0)
    stage2(i, 3, 1)

    for h in range(HB):
        on = acc_sc[h, 0:DV, :] / acc_sc[h, DV:DV + 1, :]
        o = on[:, 0:tq] - lam * on[:, tq:2 * tq]
        o = o * lax.rsqrt(jnp.mean(o * o, axis=0, keepdims=True) + EPS) * (ghb_ref[...] * (1.0 - LAMBDA_INIT))
        o_ref[:, h * DV:(h + 1) * DV] = o.T.astype(o_ref.dtype)


def _bucket(rel):
    half, max_exact = NUM_BUCKETS // 2, NUM_BUCKETS // 4
    n = np.abs(rel).astype(np.int64)
    sq = np.maximum(n * n // (max_exact * max_exact), 1)
    large = max_exact + np.floor(np.log2(sq.astype(np.float64)) + 1e-9).astype(np.int64)
    large = np.minimum(large, half - 1)
    return np.where(rel > 0, half, 0) + np.where(n < max_exact, n, large)


def _bias_tiles(rel_bias, rel0, nk, tq):
    table = (rel_bias - rel_bias[NUM_BUCKETS // 2 - 1:NUM_BUCKETS // 2, :]) * LOG2E
    half, max_exact = NUM_BUCKETS // 2, NUM_BUCKETS // 4
    dist = np.arange(4 * MAX_DISTANCE)
    buckets = _bucket(-dist)
    rel = rel0 + lax.broadcasted_iota(jnp.int32, (nk, tq), 0) - lax.broadcasted_iota(jnp.int32, (nk, tq), 1)
    n = jnp.abs(rel)
    large = jnp.full((nk, tq), max_exact, jnp.int32)
    for b in range(max_exact + 1, half):
        large = large + (n >= int(dist[buckets >= b][0])).astype(jnp.int32)
    bucket = jnp.where(rel > 0, half, 0) + jnp.where(n < max_exact, n, large)
    tile = jnp.zeros((HB, nk, tq), F32)
    for b in range(NUM_BUCKETS):
        tile = jnp.where((bucket == b)[None], table[b][:, None, None], tile)
    return tile


def _attn(qt_blk, k16, vt_blk, bias_diag, bias_prev, g_head_b, lam_vecs, bsz, t, tq, tk):
    n_qblk = t // tq
    nk = k16.shape[2]
    n_kblk = nk // tk
    assert n_qblk == 1 or tq == tk
    kern = functools.partial(_attn_kernel, tq=tq, tk=tk, n_qblk=n_qblk)
    return pl.pallas_call(
        kern,
        grid=(bsz, n_qblk),
        in_specs=[pl.BlockSpec((None, D_QB, tq), lambda b, i: (b * n_qblk + i, 0, 0)),
                  pl.BlockSpec((HB, None, nk, 2 * DK), lambda b, i: (0, b, 0, 0), pipeline_mode=pl.Buffered(1)),
                  pl.BlockSpec((HB, n_kblk, DV + ONES_ROWS, tk), lambda b, i: (0, b, 0, 0),
                               pipeline_mode=pl.Buffered(1)),
                  _const_spec(bias_diag.shape), _const_spec(bias_prev.shape),
                  _const_spec((DV, 1)), _const_spec((4, DK))],
        out_specs=pl.BlockSpec((None, tq, D_VB), lambda b, i: (b, i, 0)),
        out_shape=jax.ShapeDtypeStruct((bsz, t, D_VB), BF16),
        scratch_shapes=[pltpu.VMEM((HB, 2 * DK, 2 * tq), BF16), pltpu.VMEM((HB, 1, 2 * tq), F32),
                        pltpu.VMEM((HB, DV + ONES_ROWS, 2 * tq), F32),
                        pltpu.VMEM((2, tk, 2 * tq), F32), pltpu.VMEM((2, 1, 2 * tq), F32)],
        compiler_params=pltpu.CompilerParams(dimension_semantics=("parallel", "arbitrary"),
                                             vmem_limit_bytes=VMEM_LIMIT_BYTES),
        name="attn",
    )(qt_blk, k16, vt_blk, bias_diag, bias_prev, g_head_b.reshape(DV, 1), lam_vecs)


def _attn_step_kernel(qt_ref, kp_ref, vp_ref, kn_ref, vn_ref, bpast_ref, bnew_ref, ghb_ref, lam_ref, o_ref):
    tq = qt_ref.shape[-1]
    lv = lam_ref[...]
    lam = (jnp.exp(jnp.sum(lv[0:1, :] * lv[1:2, :], axis=1, keepdims=True))
           - jnp.exp(jnp.sum(lv[2:3, :] * lv[3:4, :], axis=1, keepdims=True)) + LAMBDA_INIT)
    first_map = lax.broadcasted_iota(jnp.int32, (2 * DK, tq), 0) < DK
    kp = kp_ref[...].reshape(kp_ref.shape[0], D_QB).astype(BF16)
    vp = vp_ref[...].reshape(vp_ref.shape[0], D_VB).astype(BF16)
    vn = vn_ref[...].reshape(vn_ref.shape[0], D_VB).astype(BF16)
    for h in range(HB):
        qh = qt_ref[h * 2 * DK:(h + 1) * 2 * DK, :]
        zero = jnp.zeros_like(qh)
        qpad = jnp.concatenate([jnp.where(first_map, qh, zero), jnp.where(first_map, zero, qh)], axis=1)
        s_p = _mm(kp[:, h * 2 * DK:(h + 1) * 2 * DK], qpad) + jnp.concatenate([bpast_ref[h], bpast_ref[h]], axis=1)
        s_n = _mm(kn_ref[h], qpad) + jnp.concatenate([bnew_ref[h], bnew_ref[h]], axis=1)
        m = jnp.maximum(jnp.max(s_p, axis=0, keepdims=True), jnp.max(s_n, axis=0, keepdims=True))
        p_p = jnp.exp2(s_p - m)
        p_n = jnp.exp2(s_n - m)
        den = jnp.sum(p_p, axis=0, keepdims=True) + jnp.sum(p_n, axis=0, keepdims=True)
        num = (lax.dot_general(vp[:, h * DV:(h + 1) * DV], p_p.astype(BF16), TN_DIMS, preferred_element_type=F32)
               + lax.dot_general(vn[:, h * DV:(h + 1) * DV], p_n.astype(BF16), TN_DIMS,
                                 preferred_element_type=F32))
        on = num / den
        o = on[:, 0:tq] - lam * on[:, tq:2 * tq]
        o = o * lax.rsqrt(jnp.mean(o * o, axis=0, keepdims=True) + EPS) * (ghb_ref[...] * (1.0 - LAMBDA_INIT))
        o_ref[:, h * DV:(h + 1) * DV] = o.T.astype(o_ref.dtype)


def _attn_step(qt, k_past, v_past, k_new16, v_new, bias_past, bias_new, g_head_b, lam_vecs):
    bsz, _, t = qt.shape
    past = k_past.shape[1]
    return pl.pallas_call(
        _attn_step_kernel,
        grid=(bsz,),
        in_specs=[pl.BlockSpec((None, D_QB, t), lambda b: (b, 0, 0)),
                  pl.BlockSpec((None, past, HB, 2 * DK), lambda b: (b, 0, 0, 0)),
                  pl.BlockSpec((None, past, HB, DV), lambda b: (b, 0, 0, 0)),
                  pl.BlockSpec((HB, None, t, 2 * DK), lambda b: (0, b, 0, 0)),
                  pl.BlockSpec((None, t, HB, DV), lambda b: (b, 0, 0, 0)),
                  _const_spec(bias_past.shape), _const_spec(bias_new.shape),
                  _const_spec((DV, 1)), _const_spec((4, DK))],
        out_specs=pl.BlockSpec((None, t, D_VB), lambda b: (b, 0, 0)),
        out_shape=jax.ShapeDtypeStruct((bsz, t, D_VB), BF16),
        compiler_params=pltpu.CompilerParams(dimension_semantics=("parallel",),
                                             vmem_limit_bytes=VMEM_LIMIT_BYTES),
        name="attn_step",
    )(qt, k_past, v_past, k_new16, v_new, bias_past, bias_new, g_head_b.reshape(DV, 1), lam_vecs)


def _out_kernel(x_ref, h_ref, ob_ref, gha_ref, gpre_ref, gpost_ref, gpref_ref, gpostf_ref, woa_ref, boa_ref,
                wga_ref, bga_ref, wgb_ref, bgb_ref, wpa_ref, wpb_ref, wout_ref, wff1_ref, wff2_ref, o_ref):
    x = x_ref[...]
    u = _rms(x, gpre_ref[...]).astype(BF16)
    hn = jnp.concatenate([_rms(h_ref[:, h * DH:(h + 1) * DH], gha_ref[h:h + 1, :]) for h in range(HA)], axis=1)
    h_a = (hn * jax.nn.sigmoid(_mm(u, woa_ref[...]) + boa_ref[...])).astype(BF16)
    y_a = _mm(h_a, wpa_ref[...])
    y_b = _mm(ob_ref[...], wpb_ref[...])
    gate_a = jax.nn.sigmoid(_mm(u, wga_ref[...]) + bga_ref[...])
    gate_b = jax.nn.sigmoid(_mm(u, wgb_ref[...]) + bgb_ref[...])
    mix = _mm((gate_a * y_a + gate_b * y_b).astype(BF16), wout_ref[...])
    x1 = x + _rms(mix, gpost_ref[...])
    f = _rms(x1, gpref_ref[...]).astype(BF16)
    hf = jnp.zeros_like(x1)
    for c in range(FF // FF_CHUNK):
        mid = jnp.maximum(_mm(f, wff1_ref[:, c * FF_CHUNK:(c + 1) * FF_CHUNK]), 0.0)
        hf = hf + _mm((mid * mid).astype(BF16), wff2_ref[c * FF_CHUNK:(c + 1) * FF_CHUNK, :])
    o_ref[...] = x1 + _rms(hf, gpostf_ref[...])


def _out(x2d, h2d, ob2d, g_head_a, g_pre_mix, g_post_mix, g_pre_ffn, g_post_ffn, w_in, b_in, w_pa, w_pb, w_out,
         w_ff1, w_ff2):
    rows = x2d.shape[0]
    tm = min(rows, OUT_ROW_TILE)
    o_oa = 3 * D_A
    o_ga = 4 * D_A + 2 * HA + 2 * D_QB + D_VB
    o_gb = o_ga + D_MODEL

    def cols(lo, n):
        return w_in[:, lo:lo + n].astype(BF16), b_in[lo:lo + n].reshape(1, n)

    woa, boa = cols(o_oa, D_A)
    wga, bga = cols(o_ga, D_MODEL)
    wgb, bgb = cols(o_gb, D_MODEL)
    consts = (g_head_a, g_pre_mix.reshape(1, -1), g_post_mix.reshape(1, -1), g_pre_ffn.reshape(1, -1),
              g_post_ffn.reshape(1, -1), woa, boa, wga, bga, wgb, bgb, w_pa.astype(BF16), w_pb.astype(BF16),
              w_out.astype(BF16), w_ff1.astype(BF16), w_ff2.astype(BF16))

    def row_spec(n):
        return pl.BlockSpec((tm, n), lambda i: (i, 0))

    return pl.pallas_call(
        _out_kernel,
        grid=(rows // tm,),
        in_specs=[row_spec(D_MODEL), row_spec(D_A), row_spec(D_VB)] + [_const_spec(c.shape) for c in consts],
        out_specs=row_spec(D_MODEL),
        out_shape=jax.ShapeDtypeStruct((rows, D_MODEL), F32),
        compiler_params=pltpu.CompilerParams(dimension_semantics=("parallel",),
                                             vmem_limit_bytes=VMEM_LIMIT_BYTES),
        name="out",
    )(x2d, h2d, ob2d, *consts)


def _layer(x, k_past, v_past, c0, n0, m0, conv0, rel_bias, g_pre_mix, g_post_mix, g_pre_ffn, g_post_ffn,
           w_in, b_in, conv_w, conv_b, g_head_a, w_pa, lam_vecs, g_head_b, w_pb, w_out, w_ff1, w_ff2):
    bsz, t, _ = x.shape
    past = k_past.shape[1]
    x2d = x.reshape(bsz * t, D_MODEL)
    qk_a, v_a, gates, k_new, k16, v_new, qt_blk, vt_blk = _proj(x2d, g_pre_mix, w_in, b_in)

    h_a, c_new, n_new, m_new = _mlstm(qk_a.reshape(bsz, t, 2 * D_A), v_a.reshape(bsz, t, D_A),
                                      gates.reshape(bsz, t, 2 * HA), conv0, c0, n0, m0, conv_w, conv_b)

    if past == 0:
        tq = tk = min(t, ROW_TILE, KEY_TILE)
        s_idx = lax.broadcasted_iota(jnp.int32, (tk, tq), 0)
        t_idx = lax.broadcasted_iota(jnp.int32, (tk, tq), 1)
        allowed = (s_idx // CHUNK) <= (t_idx // CHUNK)
        bias_diag = jnp.where(allowed[None], _bias_tiles(rel_bias, 0, tk, tq), NEG)
        corner = min(BIAS_CORNER, tk)
        bias_prev = _bias_tiles(rel_bias, -corner, corner, corner)
        o_b = _attn(qt_blk, k16.reshape(HB, bsz, t, 2 * DK), vt_blk, bias_diag, bias_prev, g_head_b, lam_vecs, bsz, t,
                    tq, tk)
    else:
        nk = past + t
        s_idx = lax.broadcasted_iota(jnp.int32, (nk, t), 0)
        t_idx = lax.broadcasted_iota(jnp.int32, (nk, t), 1)
        allowed = (s_idx // CHUNK) <= ((past + t_idx) // CHUNK)
        bias = jnp.where(allowed[None], _bias_tiles(rel_bias, -past, nk, t), NEG)
        o_b = _attn_step(jnp.transpose(qt_blk.reshape(D_QB, bsz, t), (1, 0, 2)), k_past, v_past,
                         k16.reshape(HB, bsz, t, 2 * DK), v_new.reshape(bsz, t, HB, DV),
                         bias[:, :past], bias[:, past:], g_head_b, lam_vecs)

    y = _out(x2d, h_a.reshape(bsz * t, D_A), o_b.reshape(bsz * t, D_VB), g_head_a, g_pre_mix, g_post_mix, g_pre_ffn,
             g_post_ffn, w_in, b_in, w_pa, w_pb, w_out, w_ff1, w_ff2)
    return (y.reshape(bsz, t, D_MODEL), k_new.reshape(bsz, t, HB, 2 * DK), v_new.reshape(bsz, t, HB, DV),
            c_new, n_new, m_new[:, 0, :HA], qk_a.reshape(bsz, t, 2 * D_A)[:, t - (CONV_W - 1):, :])


def kernel(x_prompt, x_sample, cache_k, cache_v, state_C, state_n, state_m, state_conv, rel_bias, g_pre_mix,
           g_post_mix, g_pre_ffn, g_post_ffn, w_in, b_in, conv_w, conv_b, g_head_a, w_pa, lambda_q1, lambda_k1,
           lambda_q2, lambda_k2, g_head_b, w_pb, w_out, w_ff1, w_ff2):
    lam_vecs = jnp.concatenate([lambda_q1, lambda_k1, lambda_q2, lambda_k2], axis=0)
    weights = (rel_bias, g_pre_mix[0], g_post_mix[0], g_pre_ffn[0], g_post_ffn[0], w_in[0], b_in[0], conv_w[0],
               conv_b[0], g_head_a[0], w_pa[0], lam_vecs, g_head_b[0], w_pb[0], w_out[0], w_ff1[0], w_ff2[0])
    bp = x_prompt.shape[0]
    prompt = _layer(x_prompt, jnp.zeros((bp, 0, HB, 2 * DK), F32), jnp.zeros((bp, 0, HB, DV), F32),
                    jnp.zeros((bp, HA, DH, DH), F32), jnp.zeros((bp, HA, DH), F32), jnp.zeros((bp, HA), F32),
                    jnp.zeros((bp, CONV_W - 1, 2 * D_A), F32), *weights)
    sample = _layer(x_sample, cache_k[0], cache_v[0], state_C[0], state_n[0], state_m[0], state_conv[0], *weights)
    yp, ys = prompt[0], sample[0]
    return (yp, ys) + tuple(a[None] for a in prompt[1:]) + tuple(a[None] for a in sample[1:])
```

```python
import functools
import math

import numpy as np
import jax
import jax.numpy as jnp
from jax import lax
from jax.experimental import pallas as pl
from jax.experimental.pallas import tpu as pltpu

F32 = jnp.float32
BF16 = jnp.bfloat16

D_MODEL = 1024
HA = 4
DH = 128
D_A = HA * DH
HB = 4
DK = 64
DV = 2 * DK
D_QB = HB * 2 * DK
D_VB = HB * DV
CONV_W = 4
FF = 4 * D_MODEL
NUM_BUCKETS = 32
MAX_DISTANCE = 128
CHUNK = 64
MLSTM_CHUNK = 128
EPS = 1e-6
LAMBDA_INIT = 0.8 - 0.6 * math.exp(-0.3 * 0)
NEG = -1e30
LOG2E = math.log2(math.e)
QSCALE = DK ** -0.5 * LOG2E
N_GATE_PAD = 128
ROW_TILE = 512
KEY_TILE = 512
ONES_ROWS = 16
BIAS_CORNER = 128
OUT_ROW_TILE = 512
MLSTM_ROW_TILE = 1024
FF_CHUNK = 1024
VMEM_LIMIT_BYTES = 56 * 1024 * 1024

NT_DIMS = (((1,), (1,)), ((), ()))
TN_DIMS = (((0,), (0,)), ((), ()))


def _const_spec(shape):
    zeros = (0,) * len(shape)
    return pl.BlockSpec(shape, lambda *_: zeros, pipeline_mode=pl.Buffered(1))


def _rms(x, g):
    return x * lax.rsqrt(jnp.mean(x * x, axis=-1, keepdims=True) + EPS) * g


def _log_sigmoid(x):
    return -(jnp.maximum(-x, 0.0) + jnp.log(1.0 + jnp.exp(-jnp.abs(x))))


def _split3(x):
    x1 = x.astype(BF16)
    r = x - x1.astype(F32)
    x2 = r.astype(BF16)
    r = r - x2.astype(F32)
    return x1, x2, r.astype(BF16)


def _mm(a, b):
    return jnp.dot(a, b, preferred_element_type=F32)


def _proj_kernel(x_ref, g_ref, wqk_ref, bqk_ref, wva_ref, bva_ref, wg_ref, bg_ref, wk_ref, bk_ref,
                 wv_ref, bv_ref, wqt_ref, bqt_ref,
                 qk_ref, va_ref, gate_ref, k_ref, k16_ref, v_ref, qt_ref, vt_ref):
    u = _rms(x_ref[...], g_ref[...]).astype(BF16)
    qk_ref[...] = _mm(u, wqk_ref[...]) + bqk_ref[...]
    va_ref[...] = _mm(u, wva_ref[...]) + bva_ref[...]
    gate_ref[...] = (_mm(u, wg_ref[...]) + bg_ref[...])[:, :2 * HA]
    k = _mm(u, wk_ref[...]) + bk_ref[...]
    k_ref[...] = k.reshape(k_ref.shape)
    k16 = k.astype(BF16)
    for h in range(HB):
        k16_ref[h] = k16[:, h * 2 * DK:(h + 1) * 2 * DK]
    v = _mm(u, wv_ref[...]) + bv_ref[...]
    v_ref[...] = v.reshape(v_ref.shape)
    qt = lax.dot_general(wqt_ref[...], u, NT_DIMS, preferred_element_type=F32) + bqt_ref[...]
    qt_ref[...] = (qt * QSCALE).astype(BF16)
    vt = v.T.astype(BF16)
    tk = vt_ref.shape[-1]
    for h in range(HB):
        for c in range(vt_ref.shape[1]):
            vt_ref[h, c, 0:DV, :] = vt[h * DV:(h + 1) * DV, c * tk:(c + 1) * tk]
            vt_ref[h, c, DV:DV + ONES_ROWS, :] = jnp.ones((ONES_ROWS, tk), BF16)


def _proj(x2d, g_pre, w_in, b_in):
    rows = x2d.shape[0]
    tm = min(rows, ROW_TILE)
    assert rows % tm == 0
    nblk = rows // tm
    tk = min(tm, KEY_TILE)
    o_qk, o_va, o_oa, o_i, o_q, o_k, o_v, o_ga = 0, 2 * D_A, 3 * D_A, 4 * D_A, 4 * D_A + 2 * HA, \
        4 * D_A + 2 * HA + D_QB, 4 * D_A + 2 * HA + 2 * D_QB, 4 * D_A + 2 * HA + 2 * D_QB + D_VB
    del o_oa, o_ga

    def cols(lo, n):
        return w_in[:, lo:lo + n].astype(BF16), b_in[lo:lo + n].reshape(1, n)

    wqk, bqk = cols(o_qk, 2 * D_A)
    wva, bva = cols(o_va, D_A)
    wg = jnp.pad(w_in[:, o_i:o_i + 2 * HA], ((0, 0), (0, N_GATE_PAD - 2 * HA))).astype(BF16)
    bg = jnp.pad(b_in[o_i:o_i + 2 * HA], (0, N_GATE_PAD - 2 * HA)).reshape(1, N_GATE_PAD)
    wk, bk = cols(o_k, D_QB)
    wv, bv = cols(o_v, D_VB)
    wqt = w_in[:, o_q:o_q + D_QB].T.astype(BF16)
    bqt = b_in[o_q:o_q + D_QB].reshape(D_QB, 1)
    consts = (g_pre.reshape(1, D_MODEL), wqk, bqk, wva, bva, wg, bg, wk, bk, wv, bv, wqt, bqt)

    def row_spec(n):
        return pl.BlockSpec((tm, n), lambda i: (i, 0))

    return pl.pallas_call(
        _proj_kernel,
        grid=(nblk,),
        in_specs=[row_spec(D_MODEL)] + [_const_spec(c.shape) for c in consts],
        out_specs=[row_spec(2 * D_A), row_spec(D_A), row_spec(2 * HA),
                   pl.BlockSpec((tm, HB, 2 * DK), lambda i: (i, 0, 0)),
                   pl.BlockSpec((HB, tm, 2 * DK), lambda i: (0, i, 0)),
                   pl.BlockSpec((tm, HB, DV), lambda i: (i, 0, 0)),
                   pl.BlockSpec((None, D_QB, tm), lambda i: (i, 0, 0)),
                   pl.BlockSpec((HB, tm // tk, DV + ONES_ROWS, tk), lambda i: (0, i, 0, 0))],
        out_shape=[jax.ShapeDtypeStruct((rows, 2 * D_A), F32), jax.ShapeDtypeStruct((rows, D_A), F32),
                   jax.ShapeDtypeStruct((rows, 2 * HA), F32), jax.ShapeDtypeStruct((rows, HB, 2 * DK), F32),
                   jax.ShapeDtypeStruct((HB, rows, 2 * DK), BF16), jax.ShapeDtypeStruct((rows, HB, DV), F32),
                   jax.ShapeDtypeStruct((nblk, D_QB, tm), BF16),
                   jax.ShapeDtypeStruct((HB, rows // tk, DV + ONES_ROWS, tk), BF16)],
        compiler_params=pltpu.CompilerParams(dimension_semantics=("parallel",),
                                             vmem_limit_bytes=VMEM_LIMIT_BYTES),
        name="proj",
    )(x2d, *consts)


def _mlstm_kernel(qk_ref, va_ref, gcol_ref, grow_ref, conv0_ref, c0_ref, n0_ref, m0_ref, cw_ref, cb_ref,
                  h_ref, c_ref, n_ref, m_ref, xp_sc, q_sc, k_sc, st_sc, m_sc, *, chunk, n_chunks):
    j = pl.program_id(1)
    tb = chunk * n_chunks

    @pl.when(j == 0)
    def _():
        xp_sc[0:8, :] = conv0_ref[...]
        for h in range(HA):
            st_sc[h, :, 0:DH] = c0_ref[h].T
            st_sc[h, :, DH:2 * DH] = jnp.broadcast_to(n0_ref[h:h + 1, :], (DH, DH)).T
        m_sc[...] = m0_ref[...]

    @pl.when(j > 0)
    def _():
        xp_sc[0:8, :] = xp_sc[tb:tb + 8, :]

    xp_sc[8:8 + tb, :] = qk_ref[...]
    cw = cw_ref[...]
    conv = cb_ref[...] + xp_sc[8:8 + tb, :] * cw[CONV_W - 1:CONV_W, :]
    for t in range(CONV_W - 1):
        conv = conv + xp_sc[8 - (CONV_W - 1) + t:8 - (CONV_W - 1) + t + tb, :] * cw[t:t + 1, :]
    act = conv * jax.nn.sigmoid(conv)
    q_sc[...] = act[:, :D_A].astype(BF16)
    k_sc[...] = (act[:, D_A:] * DH ** -0.5).astype(BF16)

    rr = lax.broadcasted_iota(jnp.int32, (chunk, chunk), 0)
    cc = lax.broadcasted_iota(jnp.int32, (chunk, chunk), 1)
    tril = cc <= rr
    ltri = tril.astype(BF16)
    utri = (rr <= cc).astype(BF16)
    ones = jnp.ones((chunk, DH), F32)

    for c in range(n_chunks):
        sl = slice(c * chunk, (c + 1) * chunk)
        gc = gcol_ref[sl, :]
        ig_c = gc[:, 0:HA]
        b_c = sum(_mm(ltri, p) for p in _split3(_log_sigmoid(gc[:, HA:2 * HA])))
        g = b_c[chunk - 1:chunk, :]
        ls_c = g - b_c + ig_c
        m_old = m_sc[0:1, 0:HA]
        m_new = jnp.maximum(g + m_old, jnp.max(ls_c, axis=0, keepdims=True))
        ws_c = jnp.exp(ls_c - m_new)
        decay = jnp.exp(g + m_old - m_new)
        a_c = b_c + m_old
        run = ig_c - b_c
        shift = 1
        while shift < chunk:
            run = jnp.maximum(run, jnp.concatenate([jnp.full((shift, HA), NEG, F32), run[:chunk - shift, :]], axis=0))
            shift *= 2
        mt_c = jnp.maximum(a_c, b_c + run)
        m_sc[0:1, 0:HA] = m_new
        gr = grow_ref[c]
        b_r = sum(_mm(p, utri) for p in _split3(_log_sigmoid(gr[HA:2 * HA, :])))
        c_r = gr[0:HA, :] - b_r
        for h in range(HA):
            hs = slice(h * DH, (h + 1) * DH)
            dlog = jnp.where(tril, b_c[:, h:h + 1] + c_r[h:h + 1, :], NEG)
            a_h = a_c[:, h:h + 1]
            m_t = mt_c[:, h:h + 1]
            p = jnp.exp(dlog - m_t)
            w_int = jnp.exp(a_h - m_t)
            q_h = q_sc[sl, hs]
            k_h = k_sc[sl, hs]
            vaug = jnp.concatenate([va_ref[sl, hs], ones], axis=1)
            s = (p * lax.dot_general(q_h, k_h, NT_DIMS, preferred_element_type=F32)).astype(BF16)
            st = st_sc[h]
            acc = w_int * _mm(q_h, st.astype(BF16)) + _mm(s, vaug.astype(BF16))
            hh = acc[:, :DH] / jnp.maximum(jnp.abs(acc[:, DH:]), jnp.exp(-m_t))
            h_ref[sl, hs] = hh
            wv = (ws_c[:, h:h + 1] * vaug).astype(BF16)
            st_sc[h] = decay[:, h:h + 1] * st + lax.dot_general(k_h, wv, TN_DIMS, preferred_element_type=F32)

    @pl.when(j == pl.num_programs(1) - 1)
    def _():
        for h in range(HA):
            st = st_sc[h]
            c_ref[h] = st[:, :DH].T
            n_ref[h:h + 1, :] = st[:, DH:].T[0:1, :]
        m_ref[...] = m_sc[...]


def _mlstm(qk_a, v_a, gates, conv0, c0, n0, m0, conv_w, conv_b):
    bsz, t, _ = qk_a.shape
    chunk = min(t, MLSTM_CHUNK)
    n_chunks = min(t // chunk, MLSTM_ROW_TILE // chunk)
    tb = chunk * n_chunks
    assert t % tb == 0
    nblk = t // tb
    grow = gates.reshape(bsz, t // chunk, chunk, 2 * HA).transpose(0, 1, 3, 2)
    conv0p = jnp.pad(conv0, ((0, 0), (8 - (CONV_W - 1), 0), (0, 0)))
    m0p = jnp.pad(m0, ((0, 0), (0, 128 - HA)))[:, None, :] * jnp.ones((1, 8, 1), F32)
    kern = functools.partial(_mlstm_kernel, chunk=chunk, n_chunks=n_chunks)
    return pl.pallas_call(
        kern,
        grid=(bsz, nblk),
        in_specs=[pl.BlockSpec((None, tb, 2 * D_A), lambda b, j: (b, j, 0)),
                  pl.BlockSpec((None, tb, D_A), lambda b, j: (b, j, 0)),
                  pl.BlockSpec((None, tb, 2 * HA), lambda b, j: (b, j, 0)),
                  pl.BlockSpec((None, n_chunks, 2 * HA, chunk), lambda b, j: (b, j, 0, 0)),
                  pl.BlockSpec((None, 8, 2 * D_A), lambda b, j: (b, 0, 0)),
                  pl.BlockSpec((None, HA, DH, DH), lambda b, j: (b, 0, 0, 0)),
                  pl.BlockSpec((None, HA, DH), lambda b, j: (b, 0, 0)),
                  pl.BlockSpec((None, 8, 128), lambda b, j: (b, 0, 0)),
                  _const_spec((CONV_W, 2 * D_A)), _const_spec((1, 2 * D_A))],
        out_specs=[pl.BlockSpec((None, tb, D_A), lambda b, j: (b, j, 0)),
                   pl.BlockSpec((None, HA, DH, DH), lambda b, j: (b, 0, 0, 0)),
                   pl.BlockSpec((None, HA, DH), lambda b, j: (b, 0, 0)),
                   pl.BlockSpec((None, 8, 128), lambda b, j: (b, 0, 0))],
        out_shape=[jax.ShapeDtypeStruct((bsz, t, D_A), F32), jax.ShapeDtypeStruct((bsz, HA, DH, DH), F32),
                   jax.ShapeDtypeStruct((bsz, HA, DH), F32), jax.ShapeDtypeStruct((bsz, 8, 128), F32)],
        scratch_shapes=[pltpu.VMEM((tb + 8, 2 * D_A), F32), pltpu.VMEM((tb, D_A), BF16),
                        pltpu.VMEM((tb, D_A), BF16), pltpu.VMEM((HA, DH, 2 * DH), F32),
                        pltpu.VMEM((8, 128), F32)],
        compiler_params=pltpu.CompilerParams(dimension_semantics=("parallel", "arbitrary"),
                                             vmem_limit_bytes=VMEM_LIMIT_BYTES),
        name="mlstm",
    )(qk_a, v_a, gates, grow, conv0p, c0, n0, m0p, conv_w, conv_b.reshape(1, 2 * D_A))


def _attn_kernel(qt_ref, k_ref, vt_ref, bd_ref, bp_ref, ghb_ref, lam_ref, o_ref,
                 qpad_sc, m_sc, acc_sc, s_sc, mx_sc, *, tq, tk, n_qblk):
    i = pl.program_id(1)
    lv = lam_ref[...]
    lam = (jnp.exp(jnp.sum(lv[0:1, :] * lv[1:2, :], axis=1, keepdims=True))
           - jnp.exp(jnp.sum(lv[2:3, :] * lv[3:4, :], axis=1, keepdims=True)) + LAMBDA_INIT)

    first_map = lax.broadcasted_iota(jnp.int32, (2 * DK, tq), 0) < DK
    for h in range(HB):
        qh = qt_ref[h * 2 * DK:(h + 1) * 2 * DK, :]
        zero = jnp.zeros_like(qh)
        qpad_sc[h, :, 0:tq] = jnp.where(first_map, qh, zero)
        qpad_sc[h, :, tq:2 * tq] = jnp.where(first_map, zero, qh)
    m_sc[...] = jnp.full(m_sc.shape, NEG, F32)
    acc_sc[...] = jnp.zeros(acc_sc.shape, F32)

    corner = bp_ref.shape[-1]

    def logits(jb, h):
        row0 = pl.multiple_of(jb * tk, tk)
        return _mm(k_ref[h, pl.ds(row0, tk), :], qpad_sc[h])

    def put(slot, s):
        s_sc[slot] = s
        mx_sc[slot] = jnp.max(s, axis=0, keepdims=True)

    def stage1(jb, h, slot, is_prev):
        put(slot, logits(jb, h))
        if is_prev is not None:
            bias = jnp.where(is_prev, bp_ref[h], 0.0)
            for c0 in (0, tq):
                sc = s_sc[slot, tk - corner:tk, c0:c0 + corner] + bias
                s_sc[slot, tk - corner:tk, c0:c0 + corner] = sc
                mx_sc[slot, :, c0:c0 + corner] = jnp.maximum(mx_sc[slot, :, c0:c0 + corner],
                                                             jnp.max(sc, axis=0, keepdims=True))

    def diag_bias(h):
        return jnp.concatenate([bd_ref[h], bd_ref[h]], axis=1)

    def stage2(jb, h, slot):
        m_old = m_sc[h]
        m_new = jnp.maximum(m_old, mx_sc[slot])
        p = jnp.exp2(s_sc[slot] - m_new).astype(BF16)
        acc_sc[h] = jnp.exp2(m_old - m_new) * acc_sc[h] + _mm(vt_ref[h, jb], p)
        m_sc[h] = m_new

    if n_qblk > 1:
        stage1(0, 0, 0, i == 1)

        def one_block(jb):
            is_prev = jb == i - 1
            stage1(jb, 1, 1, is_prev)
            stage2(jb, 0, 0)
            stage1(jb, 2, 0, is_prev)
            stage2(jb, 1, 1)
            stage1(jb, 3, 1, is_prev)
            stage2(jb, 2, 0)
            stage1(jb + 1, 0, 0, jb == i - 2)
            stage2(jb, 3, 1)

        def two_blocks(jj, carry):
            one_block(2 * jj)
            one_block(2 * jj + 1)
            return carry
        lax.fori_loop(0, i // 2, two_blocks, 0)

        @pl.when(i % 2 == 1)
        def _():
            one_block(i - 1)
    else:
        stage1(0, 0, 0, None)

    put(0, s_sc[0] + diag_bias(0))
    put(1, logits(i, 1) + diag_bias(1))
    stage2(i, 0, 0)
    put(0, logits(i, 2) + diag_bias(2))
    stage2(i, 1, 1)
    put(1, logits(i, 3) + diag_bias(3))
    stage2(i, 2, 0)
    stage2(i, 3, 1)

    for h in range(HB):
        on = acc_sc[h, 0:DV, :] / acc_sc[h, DV:DV + 1, :]
        o = on[:, 0:tq] - lam * on[:, tq:2 * tq]
        o = o * lax.rsqrt(jnp.mean(o * o, axis=0, keepdims=True) + EPS) * (ghb_ref[...] * (1.0 - LAMBDA_INIT))
        o_ref[:, h * DV:(h + 1) * DV] = o.T.astype(o_ref.dtype)


def _bucket(rel):
    half, max_exact = NUM_BUCKETS // 2, NUM_BUCKETS // 4
    n = np.abs(rel).astype(np.int64)
    sq = np.maximum(n * n // (max_exact * max_exact), 1)
    large = max_exact + np.floor(np.log2(sq.astype(np.float64)) + 1e-9).astype(np.int64)
    large = np.minimum(large, half - 1)
    return np.where(rel > 0, half, 0) + np.where(n < max_exact, n, large)


def _bias_tiles(rel_bias, rel0, nk, tq):
    table = (rel_bias - rel_bias[NUM_BUCKETS // 2 - 1:NUM_BUCKETS // 2, :]) * LOG2E
    half, max_exact = NUM_BUCKETS // 2, NUM_BUCKETS // 4
    dist = np.arange(4 * MAX_DISTANCE)
    buckets = _bucket(-dist)
    rel = rel0 + lax.broadcasted_iota(jnp.int32, (nk, tq), 0) - lax.broadcasted_iota(jnp.int32, (nk, tq), 1)
    n = jnp.abs(rel)
    large = jnp.full((nk, tq), max_exact, jnp.int32)
    for b in range(max_exact + 1, half):
        large = large + (n >= int(dist[buckets >= b][0])).astype(jnp.int32)
    bucket = jnp.where(rel > 0, half, 0) + jnp.where(n < max_exact, n, large)
    tile = jnp.zeros((HB, nk, tq), F32)
    for b in range(NUM_BUCKETS):
        tile = jnp.where((bucket == b)[None], table[b][:, None, None], tile)
    return tile


def _attn(qt_blk, k16, vt_blk, bias_diag, bias_prev, g_head_b, lam_vecs, bsz, t, tq, tk):
    n_qblk = t // tq
    nk = k16.shape[2]
    n_kblk = nk // tk
    assert t % tq == 0 and nk % tk == 0 and (n_qblk == 1 or tq == tk)
    kern = functools.partial(_attn_kernel, tq=tq, tk=tk, n_qblk=n_qblk)
    return pl.pallas_call(
        kern,
        grid=(bsz, n_qblk),
        in_specs=[pl.BlockSpec((None, D_QB, tq), lambda b, i: (b * n_qblk + i, 0, 0)),
                  pl.BlockSpec((HB, None, nk, 2 * DK), lambda b, i: (0, b, 0, 0), pipeline_mode=pl.Buffered(1)),
                  pl.BlockSpec((HB, n_kblk, DV + ONES_ROWS, tk), lambda b, i: (0, b, 0, 0),
                               pipeline_mode=pl.Buffered(1)),
                  _const_spec(bias_diag.shape), _const_spec(bias_prev.shape),
                  _const_spec((DV, 1)), _const_spec((4, DK))],
        out_specs=pl.BlockSpec((None, tq, D_VB), lambda b, i: (b, i, 0)),
        out_shape=jax.ShapeDtypeStruct((bsz, t, D_VB), BF16),
        scratch_shapes=[pltpu.VMEM((HB, 2 * DK, 2 * tq), BF16), pltpu.VMEM((HB, 1, 2 * tq), F32),
                        pltpu.VMEM((HB, DV + ONES_ROWS, 2 * tq), F32),
                        pltpu.VMEM((2, tk, 2 * tq), F32), pltpu.VMEM((2, 1, 2 * tq), F32)],
        compiler_params=pltpu.CompilerParams(dimension_semantics=("parallel", "arbitrary"),
                                             vmem_limit_bytes=VMEM_LIMIT_BYTES),
        name="attn",
    )(qt_blk, k16, vt_blk, bias_diag, bias_prev, g_head_b.reshape(DV, 1), lam_vecs)


def _attn_step_kernel(qt_ref, kp_ref, vp_ref, kn_ref, vn_ref, bpast_ref, bnew_ref, ghb_ref, lam_ref, o_ref):
    tq = qt_ref.shape[-1]
    lv = lam_ref[...]
    lam = (jnp.exp(jnp.sum(lv[0:1, :] * lv[1:2, :], axis=1, keepdims=True))
           - jnp.exp(jnp.sum(lv[2:3, :] * lv[3:4, :], axis=1, keepdims=True)) + LAMBDA_INIT)
    first_map = lax.broadcasted_iota(jnp.int32, (2 * DK, tq), 0) < DK
    kp = kp_ref[...].reshape(kp_ref.shape[0], D_QB).astype(BF16)
    vp = vp_ref[...].reshape(vp_ref.shape[0], D_VB).astype(BF16)
    vn = vn_ref[...].reshape(vn_ref.shape[0], D_VB).astype(BF16)
    for h in range(HB):
        qh = qt_ref[h * 2 * DK:(h + 1) * 2 * DK, :]
        zero = jnp.zeros_like(qh)
        qpad = jnp.concatenate([jnp.where(first_map, qh, zero), jnp.where(first_map, zero, qh)], axis=1)
        s_p = _mm(kp[:, h * 2 * DK:(h + 1) * 2 * DK], qpad) + jnp.concatenate([bpast_ref[h], bpast_ref[h]], axis=1)
        s_n = _mm(kn_ref[h], qpad) + jnp.concatenate([bnew_ref[h], bnew_ref[h]], axis=1)
        m = jnp.maximum(jnp.max(s_p, axis=0, keepdims=True), jnp.max(s_n, axis=0, keepdims=True))
        p_p = jnp.exp2(s_p - m)
        p_n = jnp.exp2(s_n - m)
        den = jnp.sum(p_p, axis=0, keepdims=True) + jnp.sum(p_n, axis=0, keepdims=True)
        num = (lax.dot_general(vp[:, h * DV:(h + 1) * DV], p_p.astype(BF16), TN_DIMS, preferred_element_type=F32)
               + lax.dot_general(vn[:, h * DV:(h + 1) * DV], p_n.astype(BF16), TN_DIMS,
                                 preferred_element_type=F32))
        on = num / den
        o = on[:, 0:tq] - lam * on[:, tq:2 * tq]
        o = o * lax.rsqrt(jnp.mean(o * o, axis=0, keepdims=True) + EPS) * (ghb_ref[...] * (1.0 - LAMBDA_INIT))
        o_ref[:, h * DV:(h + 1) * DV] = o.T.astype(o_ref.dtype)


def _attn_step(qt, k_past, v_past, k_new16, v_new, bias_past, bias_new, g_head_b, lam_vecs):
    bsz, _, t = qt.shape
    past = k_past.shape[1]
    return pl.pallas_call(
        _attn_step_kernel,
        grid=(bsz,),
        in_specs=[pl.BlockSpec((None, D_QB, t), lambda b: (b, 0, 0)),
                  pl.BlockSpec((None, past, HB, 2 * DK), lambda b: (b, 0, 0, 0)),
                  pl.BlockSpec((None, past, HB, DV), lambda b: (b, 0, 0, 0)),
                  pl.BlockSpec((HB, None, t, 2 * DK), lambda b: (0, b, 0, 0)),
                  pl.BlockSpec((None, t, HB, DV), lambda b: (b, 0, 0, 0)),
                  _const_spec(bias_past.shape), _const_spec(bias_new.shape),
                  _const_spec((DV, 1)), _const_spec((4, DK))],
        out_specs=pl.BlockSpec((None, t, D_VB), lambda b: (b, 0, 0)),
        out_shape=jax.ShapeDtypeStruct((bsz, t, D_VB), BF16),
        compiler_params=pltpu.CompilerParams(dimension_semantics=("parallel",),
                                             vmem_limit_bytes=VMEM_LIMIT_BYTES),
        name="attn_step",
    )(qt, k_past, v_past, k_new16, v_new, bias_past, bias_new, g_head_b.reshape(DV, 1), lam_vecs)


def _out_kernel(x_ref, h_ref, ob_ref, gha_ref, gpre_ref, gpost_ref, gpref_ref, gpostf_ref, woa_ref, boa_ref,
                wga_ref, bga_ref, wgb_ref, bgb_ref, wpa_ref, wpb_ref, wout_ref, wff1_ref, wff2_ref, o_ref):
    x = x_ref[...]
    u = _rms(x, gpre_ref[...]).astype(BF16)
    hn = jnp.concatenate([_rms(h_ref[:, h * DH:(h + 1) * DH], gha_ref[h:h + 1, :]) for h in range(HA)], axis=1)
    h_a = (hn * jax.nn.sigmoid(_mm(u, woa_ref[...]) + boa_ref[...])).astype(BF16)
    y_a = _mm(h_a, wpa_ref[...])
    y_b = _mm(ob_ref[...], wpb_ref[...])
    gate_a = jax.nn.sigmoid(_mm(u, wga_ref[...]) + bga_ref[...])
    gate_b = jax.nn.sigmoid(_mm(u, wgb_ref[...]) + bgb_ref[...])
    mix = _mm((gate_a * y_a + gate_b * y_b).astype(BF16), wout_ref[...])
    x1 = x + _rms(mix, gpost_ref[...])
    f = _rms(x1, gpref_ref[...]).astype(BF16)
    hf = jnp.zeros_like(x1)
    for c in range(FF // FF_CHUNK):
        mid = jnp.maximum(_mm(f, wff1_ref[:, c * FF_CHUNK:(c + 1) * FF_CHUNK]), 0.0)
        hf = hf + _mm((mid * mid).astype(BF16), wff2_ref[c * FF_CHUNK:(c + 1) * FF_CHUNK, :])
    o_ref[...] = x1 + _rms(hf, gpostf_ref[...])


def _out(x2d, h2d, ob2d, g_head_a, g_pre_mix, g_post_mix, g_pre_ffn, g_post_ffn, w_in, b_in, w_pa, w_pb, w_out,
         w_ff1, w_ff2):
    rows = x2d.shape[0]
    tm = min(rows, OUT_ROW_TILE)
    o_oa = 3 * D_A
    o_ga = 4 * D_A + 2 * HA + 2 * D_QB + D_VB
    o_gb = o_ga + D_MODEL

    def cols(lo, n):
        return w_in[:, lo:lo + n].astype(BF16), b_in[lo:lo + n].reshape(1, n)

    woa, boa = cols(o_oa, D_A)
    wga, bga = cols(o_ga, D_MODEL)
    wgb, bgb = cols(o_gb, D_MODEL)
    consts = (g_head_a, g_pre_mix.reshape(1, -1), g_post_mix.reshape(1, -1), g_pre_ffn.reshape(1, -1),
              g_post_ffn.reshape(1, -1), woa, boa, wga, bga, wgb, bgb, w_pa.astype(BF16), w_pb.astype(BF16),
              w_out.astype(BF16), w_ff1.astype(BF16), w_ff2.astype(BF16))

    assert rows % tm == 0

    def row_spec(n):
        return pl.BlockSpec((tm, n), lambda i: (i, 0))

    return pl.pallas_call(
        _out_kernel,
        grid=(rows // tm,),
        in_specs=[row_spec(D_MODEL), row_spec(D_A), row_spec(D_VB)] + [_const_spec(c.shape) for c in consts],
        out_specs=row_spec(D_MODEL),
        out_shape=jax.ShapeDtypeStruct((rows, D_MODEL), F32),
        compiler_params=pltpu.CompilerParams(dimension_semantics=("parallel",),
                                             vmem_limit_bytes=VMEM_LIMIT_BYTES),
        name="out",
    )(x2d, h2d, ob2d, *consts)


def _layer(x, k_past, v_past, c0, n0, m0, conv0, rel_bias, g_pre_mix, g_post_mix, g_pre_ffn, g_post_ffn,
           w_in, b_in, conv_w, conv_b, g_head_a, w_pa, lam_vecs, g_head_b, w_pb, w_out, w_ff1, w_ff2):
    bsz, t, _ = x.shape
    past = k_past.shape[1]
    x2d = x.reshape(bsz * t, D_MODEL)
    qk_a, v_a, gates, k_new, k16, v_new, qt_blk, vt_blk = _proj(x2d, g_pre_mix, w_in, b_in)

    h_a, c_new, n_new, m_new = _mlstm(qk_a.reshape(bsz, t, 2 * D_A), v_a.reshape(bsz, t, D_A),
                                      gates.reshape(bsz, t, 2 * HA), conv0, c0, n0, m0, conv_w, conv_b)

    if past == 0:
        tq = tk = min(t, ROW_TILE, KEY_TILE)
        s_idx = lax.broadcasted_iota(jnp.int32, (tk, tq), 0)
        t_idx = lax.broadcasted_iota(jnp.int32, (tk, tq), 1)
        allowed = (s_idx // CHUNK) <= (t_idx // CHUNK)
        bias_diag = jnp.where(allowed[None], _bias_tiles(rel_bias, 0, tk, tq), NEG)
        corner = min(BIAS_CORNER, tk)
        bias_prev = _bias_tiles(rel_bias, -corner, corner, corner)
        o_b = _attn(qt_blk, k16.reshape(HB, bsz, t, 2 * DK), vt_blk, bias_diag, bias_prev, g_head_b, lam_vecs, bsz, t,
                    tq, tk)
    else:
        nk = past + t
        s_idx = lax.broadcasted_iota(jnp.int32, (nk, t), 0)
        t_idx = lax.broadcasted_iota(jnp.int32, (nk, t), 1)
        allowed = (s_idx // CHUNK) <= ((past + t_idx) // CHUNK)
        bias = jnp.where(allowed[None], _bias_tiles(rel_bias, -past, nk, t), NEG)
        o_b = _attn_step(jnp.transpose(qt_blk.reshape(D_QB, bsz, t), (1, 0, 2)), k_past, v_past,
                         k16.reshape(HB, bsz, t, 2 * DK), v_new.reshape(bsz, t, HB, DV),
                         bias[:, :past], bias[:, past:], g_head_b, lam_vecs)

    y = _out(x2d, h_a.reshape(bsz * t, D_A), o_b.reshape(bsz * t, D_VB), g_head_a, g_pre_mix, g_post_mix, g_pre_ffn,
             g_post_ffn, w_in, b_in, w_pa, w_pb, w_out, w_ff1, w_ff2)
    return (y.reshape(bsz, t, D_MODEL), k_new.reshape(bsz, t, HB, 2 * DK), v_new.reshape(bsz, t, HB, DV),
            c_new, n_new, m_new[:, 0, :HA], qk_a.reshape(bsz, t, 2 * D_A)[:, t - (CONV_W - 1):, :])


def kernel(x_prompt, x_sample, cache_k, cache_v, state_C, state_n, state_m, state_conv, rel_bias, g_pre_mix,
           g_post_mix, g_pre_ffn, g_post_ffn, w_in, b_in, conv_w, conv_b, g_head_a, w_pa, lambda_q1, lambda_k1,
           lambda_q2, lambda_k2, g_head_b, w_pb, w_out, w_ff1, w_ff2):
    lam_vecs = jnp.concatenate([lambda_q1, lambda_k1, lambda_q2, lambda_k2], axis=0)
    weights = (rel_bias, g_pre_mix[0], g_post_mix[0], g_pre_ffn[0], g_post_ffn[0], w_in[0], b_in[0], conv_w[0],
               conv_b[0], g_head_a[0], w_pa[0], lam_vecs, g_head_b[0], w_pb[0], w_out[0], w_ff1[0], w_ff2[0])
    bp = x_prompt.shape[0]
    prompt = _layer(x_prompt, jnp.zeros((bp, 0, HB, 2 * DK), F32), jnp.zeros((bp, 0, HB, DV), F32),
                    jnp.zeros((bp, HA, DH, DH), F32), jnp.zeros((bp, HA, DH), F32), jnp.zeros((bp, HA), F32),
                    jnp.zeros((bp, CONV_W - 1, 2 * D_A), F32), *weights)
    sample = _layer(x_sample, cache_k[0], cache_v[0], state_C[0], state_n[0], state_m[0], state_conv[0], *weights)
    yp, ys = prompt[0], sample[0]
    return (yp, ys) + tuple(a[None] for a in prompt[1:]) + tuple(a[None] for a in sample[1:])
```

```python
import functools
import math

import numpy as np
import jax
import jax.numpy as jnp
from jax import lax
from jax.experimental import pallas as pl
from jax.experimental.pallas import tpu as pltpu

F32 = jnp.float32
BF16 = jnp.bfloat16

D_MODEL = 1024
HA = 4
DH = 128
D_A = HA * DH
HB = 4
DK = 64
DV = 2 * DK
D_QB = HB * 2 * DK
D_VB = HB * DV
CONV_W = 4
FF = 4 * D_MODEL
NUM_BUCKETS = 32
MAX_DISTANCE = 128
CHUNK = 64
MLSTM_CHUNK = 128
EPS = 1e-6
LAMBDA_INIT = 0.8 - 0.6 * math.exp(-0.3 * 0)
NEG = -1e30
LOG2E = math.log2(math.e)
QSCALE = DK ** -0.5 * LOG2E
GATE_ROWS = 16
ROW_TILE = 512
KEY_TILE = 512
ONES_ROWS = 16
BIAS_CORNER = 128
OUT_ROW_TILE = 512
MLSTM_ROW_TILE = 1024
FF_CHUNK = 1024
VMEM_LIMIT_BYTES = 56 * 1024 * 1024

NT_DIMS = (((1,), (1,)), ((), ()))
TN_DIMS = (((0,), (0,)), ((), ()))


def _const_spec(shape):
    zeros = (0,) * len(shape)
    return pl.BlockSpec(shape, lambda *_: zeros, pipeline_mode=pl.Buffered(1))


def _rms(x, g):
    return x * lax.rsqrt(jnp.mean(x * x, axis=-1, keepdims=True) + EPS) * g


def _log_sigmoid(x):
    return -(jnp.maximum(-x, 0.0) + jnp.log(1.0 + jnp.exp(-jnp.abs(x))))


def _split3(x):
    x1 = x.astype(BF16)
    r = x - x1.astype(F32)
    x2 = r.astype(BF16)
    r = r - x2.astype(F32)
    return x1, x2, r.astype(BF16)


def _mm(a, b):
    return jnp.dot(a, b, preferred_element_type=F32)


def _proj_kernel(x_ref, g_ref, wqk_ref, bqk_ref, wva_ref, bva_ref, wk_ref, bk_ref,
                 wv_ref, bv_ref, wqt_ref, bqt_ref,
                 qk_ref, va_ref, gate_ref, k_ref, k16_ref, v_ref, qt_ref, vt_ref):
    u = _rms(x_ref[...], g_ref[...]).astype(BF16)
    qk_ref[...] = _mm(u, wqk_ref[...]) + bqk_ref[...]
    va_ref[...] = _mm(u, wva_ref[...]) + bva_ref[...]
    k = _mm(u, wk_ref[...]) + bk_ref[...]
    k_ref[...] = k.reshape(k_ref.shape)
    k16 = k.astype(BF16)
    for h in range(HB):
        k16_ref[h] = k16[:, h * 2 * DK:(h + 1) * 2 * DK]
    v = _mm(u, wv_ref[...]) + bv_ref[...]
    v_ref[...] = v.reshape(v_ref.shape)
    qt = lax.dot_general(wqt_ref[...], u, NT_DIMS, preferred_element_type=F32) + bqt_ref[...]
    qt_ref[...] = (qt[0:D_QB, :] * QSCALE).astype(BF16)
    gate_ref[...] = qt[D_QB:D_QB + 2 * HA, :]
    vt = v.T.astype(BF16)
    tk = vt_ref.shape[-1]
    for h in range(HB):
        for c in range(vt_ref.shape[1]):
            vt_ref[h, c, 0:DV, :] = vt[h * DV:(h + 1) * DV, c * tk:(c + 1) * tk]
            vt_ref[h, c, DV:DV + ONES_ROWS, :] = jnp.ones((ONES_ROWS, tk), BF16)


def _proj(x2d, g_pre, w_in, b_in):
    rows = x2d.shape[0]
    tm = min(rows, ROW_TILE)
    assert rows % tm == 0
    nblk = rows // tm
    tk = min(tm, KEY_TILE)
    o_qk, o_va, o_oa, o_i, o_q, o_k, o_v, o_ga = 0, 2 * D_A, 3 * D_A, 4 * D_A, 4 * D_A + 2 * HA, \
        4 * D_A + 2 * HA + D_QB, 4 * D_A + 2 * HA + 2 * D_QB, 4 * D_A + 2 * HA + 2 * D_QB + D_VB
    del o_oa, o_ga

    def cols(lo, n):
        return w_in[:, lo:lo + n].astype(BF16), b_in[lo:lo + n].reshape(1, n)

    wqk, bqk = cols(o_qk, 2 * D_A)
    wva, bva = cols(o_va, D_A)
    wk, bk = cols(o_k, D_QB)
    wv, bv = cols(o_v, D_VB)
    t_cols = jnp.concatenate([w_in[:, o_q:o_q + D_QB], w_in[:, o_i:o_i + 2 * HA],
                              jnp.zeros((D_MODEL, GATE_ROWS - 2 * HA), F32)], axis=1)
    t_bias = jnp.concatenate([b_in[o_q:o_q + D_QB], b_in[o_i:o_i + 2 * HA], jnp.zeros((GATE_ROWS - 2 * HA,), F32)])
    wqt = t_cols.T.astype(BF16)
    bqt = t_bias.reshape(D_QB + GATE_ROWS, 1)
    consts = (g_pre.reshape(1, D_MODEL), wqk, bqk, wva, bva, wk, bk, wv, bv, wqt, bqt)

    def row_spec(n):
        return pl.BlockSpec((tm, n), lambda i: (i, 0))

    return pl.pallas_call(
        _proj_kernel,
        grid=(nblk,),
        in_specs=[row_spec(D_MODEL)] + [_const_spec(c.shape) for c in consts],
        out_specs=[row_spec(2 * D_A), row_spec(D_A), pl.BlockSpec((2 * HA, tm), lambda i: (0, i)),
                   pl.BlockSpec((tm, HB, 2 * DK), lambda i: (i, 0, 0)),
                   pl.BlockSpec((HB, tm, 2 * DK), lambda i: (0, i, 0)),
                   pl.BlockSpec((tm, HB, DV), lambda i: (i, 0, 0)),
                   pl.BlockSpec((None, D_QB, tm), lambda i: (i, 0, 0)),
                   pl.BlockSpec((HB, tm // tk, DV + ONES_ROWS, tk), lambda i: (0, i, 0, 0))],
        out_shape=[jax.ShapeDtypeStruct((rows, 2 * D_A), F32), jax.ShapeDtypeStruct((rows, D_A), F32),
                   jax.ShapeDtypeStruct((2 * HA, rows), F32), jax.ShapeDtypeStruct((rows, HB, 2 * DK), F32),
                   jax.ShapeDtypeStruct((HB, rows, 2 * DK), BF16), jax.ShapeDtypeStruct((rows, HB, DV), F32),
                   jax.ShapeDtypeStruct((nblk, D_QB, tm), BF16),
                   jax.ShapeDtypeStruct((HB, rows // tk, DV + ONES_ROWS, tk), BF16)],
        compiler_params=pltpu.CompilerParams(dimension_semantics=("parallel",),
                                             vmem_limit_bytes=VMEM_LIMIT_BYTES),
        name="proj",
    )(x2d, *consts)


def _mlstm_kernel(qk_ref, va_ref, gcol_ref, grow_ref, conv0_ref, c0_ref, n0_ref, m0_ref, cw_ref, cb_ref,
                  h_ref, c_ref, n_ref, m_ref, xp_sc, q_sc, k_sc, st_sc, m_sc, *, chunk, n_chunks):
    j = pl.program_id(1)
    tb = chunk * n_chunks

    @pl.when(j == 0)
    def _():
        xp_sc[0:8, :] = conv0_ref[...]
        for h in range(HA):
            st_sc[h, :, 0:DH] = c0_ref[h].T
            st_sc[h, :, DH:2 * DH] = jnp.broadcast_to(n0_ref[h:h + 1, :], (DH, DH)).T
        m_sc[...] = m0_ref[...]

    @pl.when(j > 0)
    def _():
        xp_sc[0:8, :] = xp_sc[tb:tb + 8, :]

    xp_sc[8:8 + tb, :] = qk_ref[...]
    cw = cw_ref[...]
    conv = cb_ref[...] + xp_sc[8:8 + tb, :] * cw[CONV_W - 1:CONV_W, :]
    for t in range(CONV_W - 1):
        conv = conv + xp_sc[8 - (CONV_W - 1) + t:8 - (CONV_W - 1) + t + tb, :] * cw[t:t + 1, :]
    half = 0.5 * conv
    act = half + half * jnp.tanh(half)
    q_sc[...] = act[:, :D_A].astype(BF16)
    k_sc[...] = (act[:, D_A:] * DH ** -0.5).astype(BF16)

    rr = lax.broadcasted_iota(jnp.int32, (chunk, chunk), 0)
    cc = lax.broadcasted_iota(jnp.int32, (chunk, chunk), 1)
    tril = cc <= rr
    ltri = tril.astype(BF16)
    utri = (rr <= cc).astype(BF16)
    ones = jnp.ones((chunk, DH), F32)

    for c in range(n_chunks):
        sl = slice(c * chunk, (c + 1) * chunk)
        gc = gcol_ref[sl, :]
        ig_c = gc[:, 0:HA]
        b_c = sum(_mm(ltri, p) for p in _split3(_log_sigmoid(gc[:, HA:2 * HA])))
        g = b_c[chunk - 1:chunk, :]
        ls_c = g - b_c + ig_c
        m_old = m_sc[0:1, 0:HA]
        m_new = jnp.maximum(g + m_old, jnp.max(ls_c, axis=0, keepdims=True))
        ws_c = jnp.exp(ls_c - m_new)
        decay = jnp.exp(g + m_old - m_new)
        a_c = b_c + m_old
        run = ig_c - b_c
        shift = 1
        while shift < chunk:
            run = jnp.maximum(run, jnp.concatenate([jnp.full((shift, HA), NEG, F32), run[:chunk - shift, :]], axis=0))
            shift *= 2
        mt_c = jnp.maximum(a_c, b_c + run)
        m_sc[0:1, 0:HA] = m_new
        gr = grow_ref[c]
        b_r = sum(_mm(p, utri) for p in _split3(_log_sigmoid(gr[HA:2 * HA, :])))
        c_r = gr[0:HA, :] - b_r
        for h in range(HA):
            hs = slice(h * DH, (h + 1) * DH)
            dlog = jnp.where(tril, b_c[:, h:h + 1] + c_r[h:h + 1, :], NEG)
            a_h = a_c[:, h:h + 1]
            m_t = mt_c[:, h:h + 1]
            p = jnp.exp(dlog - m_t)
            w_int = jnp.exp(a_h - m_t)
            q_h = q_sc[sl, hs]
            k_h = k_sc[sl, hs]
            vaug = jnp.concatenate([va_ref[sl, hs], ones], axis=1)
            s = (p * lax.dot_general(q_h, k_h, NT_DIMS, preferred_element_type=F32)).astype(BF16)
            st = st_sc[h]
            acc = w_int * _mm(q_h, st.astype(BF16)) + _mm(s, vaug.astype(BF16))
            hh = acc[:, :DH] / jnp.maximum(jnp.abs(acc[:, DH:]), jnp.exp(-m_t))
            h_ref[sl, hs] = hh
            wv = (ws_c[:, h:h + 1] * vaug).astype(BF16)
            st_sc[h] = decay[:, h:h + 1] * st + lax.dot_general(k_h, wv, TN_DIMS, preferred_element_type=F32)

    @pl.when(j == pl.num_programs(1) - 1)
    def _():
        for h in range(HA):
            st = st_sc[h]
            c_ref[h] = st[:, :DH].T
            n_ref[h:h + 1, :] = st[:, DH:].T[0:1, :]
        m_ref[...] = m_sc[...]


def _mlstm(qk_a, v_a, gates_t, conv0, c0, n0, m0, conv_w, conv_b):
    bsz, t, _ = qk_a.shape
    chunk = min(t, MLSTM_CHUNK)
    n_chunks = min(t // chunk, MLSTM_ROW_TILE // chunk)
    tb = chunk * n_chunks
    assert t % tb == 0
    nblk = t // tb
    grow = gates_t.reshape(2 * HA, bsz, t // chunk, chunk).transpose(1, 2, 0, 3)
    gates = gates_t.T.reshape(bsz, t, 2 * HA)
    conv0p = jnp.pad(conv0, ((0, 0), (8 - (CONV_W - 1), 0), (0, 0)))
    m0p = jnp.pad(m0, ((0, 0), (0, 128 - HA)))[:, None, :] * jnp.ones((1, 8, 1), F32)
    kern = functools.partial(_mlstm_kernel, chunk=chunk, n_chunks=n_chunks)
    return pl.pallas_call(
        kern,
        grid=(bsz, nblk),
        in_specs=[pl.BlockSpec((None, tb, 2 * D_A), lambda b, j: (b, j, 0)),
                  pl.BlockSpec((None, tb, D_A), lambda b, j: (b, j, 0)),
                  pl.BlockSpec((None, tb, 2 * HA), lambda b, j: (b, j, 0)),
                  pl.BlockSpec((None, n_chunks, 2 * HA, chunk), lambda b, j: (b, j, 0, 0)),
                  pl.BlockSpec((None, 8, 2 * D_A), lambda b, j: (b, 0, 0)),
                  pl.BlockSpec((None, HA, DH, DH), lambda b, j: (b, 0, 0, 0)),
                  pl.BlockSpec((None, HA, DH), lambda b, j: (b, 0, 0)),
                  pl.BlockSpec((None, 8, 128), lambda b, j: (b, 0, 0)),
                  _const_spec((CONV_W, 2 * D_A)), _const_spec((1, 2 * D_A))],
        out_specs=[pl.BlockSpec((None, tb, D_A), lambda b, j: (b, j, 0)),
                   pl.BlockSpec((None, HA, DH, DH), lambda b, j: (b, 0, 0, 0)),
                   pl.BlockSpec((None, HA, DH), lambda b, j: (b, 0, 0)),
                   pl.BlockSpec((None, 8, 128), lambda b, j: (b, 0, 0))],
        out_shape=[jax.ShapeDtypeStruct((bsz, t, D_A), F32), jax.ShapeDtypeStruct((bsz, HA, DH, DH), F32),
                   jax.ShapeDtypeStruct((bsz, HA, DH), F32), jax.ShapeDtypeStruct((bsz, 8, 128), F32)],
        scratch_shapes=[pltpu.VMEM((tb + 8, 2 * D_A), F32), pltpu.VMEM((tb, D_A), BF16),
                        pltpu.VMEM((tb, D_A), BF16), pltpu.VMEM((HA, DH, 2 * DH), F32),
                        pltpu.VMEM((8, 128), F32)],
        compiler_params=pltpu.CompilerParams(dimension_semantics=("parallel", "arbitrary"),
                                             vmem_limit_bytes=VMEM_LIMIT_BYTES),
        name="mlstm",
    )(qk_a, v_a, gates, grow, conv0p, c0, n0, m0p, conv_w, conv_b.reshape(1, 2 * D_A))


def _attn_kernel(qt_ref, k_ref, vt_ref, bd_ref, bp_ref, ghb_ref, lam_ref, o_ref,
                 qpad_sc, m_sc, acc_sc, s_sc, mx_sc, *, tq, tk, n_qblk):
    i = pl.program_id(1)
    lv = lam_ref[...]
    lam = (jnp.exp(jnp.sum(lv[0:1, :] * lv[1:2, :], axis=1, keepdims=True))
           - jnp.exp(jnp.sum(lv[2:3, :] * lv[3:4, :], axis=1, keepdims=True)) + LAMBDA_INIT)

    first_map = lax.broadcasted_iota(jnp.int32, (2 * DK, tq), 0) < DK
    for h in range(HB):
        qh = qt_ref[h * 2 * DK:(h + 1) * 2 * DK, :]
        zero = jnp.zeros_like(qh)
        qpad_sc[h, :, 0:tq] = jnp.where(first_map, qh, zero)
        qpad_sc[h, :, tq:2 * tq] = jnp.where(first_map, zero, qh)
    m_sc[...] = jnp.full(m_sc.shape, NEG, F32)
    acc_sc[...] = jnp.zeros(acc_sc.shape, F32)

    corner = bp_ref.shape[-1]

    def logits(jb, h):
        row0 = pl.multiple_of(jb * tk, tk)
        return _mm(k_ref[h, pl.ds(row0, tk), :], qpad_sc[h])

    def put(slot, s):
        s_sc[slot] = s
        mx_sc[slot] = jnp.max(s, axis=0, keepdims=True)

    def stage1(jb, h, slot, is_prev):
        put(slot, logits(jb, h))
        if is_prev is not None:
            bias = jnp.where(is_prev, bp_ref[h], 0.0)
            for c0 in (0, tq):
                sc = s_sc[slot, tk - corner:tk, c0:c0 + corner] + bias
                s_sc[slot, tk - corner:tk, c0:c0 + corner] = sc
                mx_sc[slot, :, c0:c0 + corner] = jnp.maximum(mx_sc[slot, :, c0:c0 + corner],
                                                             jnp.max(sc, axis=0, keepdims=True))

    def diag_bias(h):
        return jnp.concatenate([bd_ref[h], bd_ref[h]], axis=1)

    def stage2(jb, h, slot):
        m_old = m_sc[h]
        m_new = jnp.maximum(m_old, mx_sc[slot])
        p = jnp.exp2(s_sc[slot] - m_new).astype(BF16)
        acc_sc[h] = jnp.exp2(m_old - m_new) * acc_sc[h] + _mm(vt_ref[h, jb], p)
        m_sc[h] = m_new

    if n_qblk > 1:
        stage1(0, 0, 0, i == 1)

        def one_block(jb):
            is_prev = jb == i - 1
            stage1(jb, 1, 1, is_prev)
            stage2(jb, 0, 0)
            stage1(jb, 2, 0, is_prev)
            stage2(jb, 1, 1)
            stage1(jb, 3, 1, is_prev)
            stage2(jb, 2, 0)
            stage1(jb + 1, 0, 0, jb == i - 2)
            stage2(jb, 3, 1)

        def two_blocks(jj, carry):
            one_block(2 * jj)
            one_block(2 * jj + 1)
            return carry
        lax.fori_loop(0, i // 2, two_blocks, 0)

        @pl.when(i % 2 == 1)
        def _():
            one_block(i - 1)
    else:
        stage1(0, 0, 0, None)

    put(0, s_sc[0] + diag_bias(0))
    put(1, logits(i, 1) + diag_bias(1))
    stage2(i, 0, 0)
    put(0, logits(i, 2) + diag_bias(2))
    stage2(i, 1, 1)
    put(1, logits(i, 3) + diag_bias(3))
    stage2(i, 2, 0)
    stage2(i, 3, 1)

    for h in range(HB):
        on = acc_sc[h, 0:DV, :] / acc_sc[h, DV:DV + 1, :]
        o = on[:, 0:tq] - lam * on[:, tq:2 * tq]
        o = o * lax.rsqrt(jnp.mean(o * o, axis=0, keepdims=True) + EPS) * (ghb_ref[...] * (1.0 - LAMBDA_INIT))
        o_ref[:, h * DV:(h + 1) * DV] = o.T.astype(o_ref.dtype)


def _bucket(rel):
    half, max_exact = NUM_BUCKETS // 2, NUM_BUCKETS // 4
    n = np.abs(rel).astype(np.int64)
    sq = np.maximum(n * n // (max_exact * max_exact), 1)
    large = max_exact + np.floor(np.log2(sq.astype(np.float64)) + 1e-9).astype(np.int64)
    large = np.minimum(large, half - 1)
    return np.where(rel > 0, half, 0) + np.where(n < max_exact, n, large)


def _bias_tiles(rel_bias, rel0, nk, tq):
    table = (rel_bias - rel_bias[NUM_BUCKETS // 2 - 1:NUM_BUCKETS // 2, :]) * LOG2E
    half, max_exact = NUM_BUCKETS // 2, NUM_BUCKETS // 4
    dist = np.arange(4 * MAX_DISTANCE)
    buckets = _bucket(-dist)
    rel = rel0 + lax.broadcasted_iota(jnp.int32, (nk, tq), 0) - lax.broadcasted_iota(jnp.int32, (nk, tq), 1)
    n = jnp.abs(rel)
    large = jnp.full((nk, tq), max_exact, jnp.int32)
    for b in range(max_exact + 1, half):
        large = large + (n >= int(dist[buckets >= b][0])).astype(jnp.int32)
    bucket = jnp.where(rel > 0, half, 0) + jnp.where(n < max_exact, n, large)
    tile = jnp.zeros((HB, nk, tq), F32)
    for b in range(NUM_BUCKETS):
        tile = jnp.where((bucket == b)[None], table[b][:, None, None], tile)
    return tile


def _attn(qt_blk, k16, vt_blk, bias_diag, bias_prev, g_head_b, lam_vecs, bsz, t, tq, tk):
    n_qblk = t // tq
    nk = k16.shape[2]
    n_kblk = nk // tk
    assert t % tq == 0 and nk % tk == 0 and (n_qblk == 1 or tq == tk)
    kern = functools.partial(_attn_kernel, tq=tq, tk=tk, n_qblk=n_qblk)
    return pl.pallas_call(
        kern,
        grid=(bsz, n_qblk),
        in_specs=[pl.BlockSpec((None, D_QB, tq), lambda b, i: (b * n_qblk + i, 0, 0)),
                  pl.BlockSpec((HB, None, nk, 2 * DK), lambda b, i: (0, b, 0, 0), pipeline_mode=pl.Buffered(1)),
                  pl.BlockSpec((HB, n_kblk, DV + ONES_ROWS, tk), lambda b, i: (0, b, 0, 0),
                               pipeline_mode=pl.Buffered(1)),
                  _const_spec(bias_diag.shape), _const_spec(bias_prev.shape),
                  _const_spec((DV, 1)), _const_spec((4, DK))],
        out_specs=pl.BlockSpec((None, tq, D_VB), lambda b, i: (b, i, 0)),
        out_shape=jax.ShapeDtypeStruct((bsz, t, D_VB), BF16),
        scratch_shapes=[pltpu.VMEM((HB, 2 * DK, 2 * tq), BF16), pltpu.VMEM((HB, 1, 2 * tq), F32),
                        pltpu.VMEM((HB, DV + ONES_ROWS, 2 * tq), F32),
                        pltpu.VMEM((2, tk, 2 * tq), F32), pltpu.VMEM((2, 1, 2 * tq), F32)],
        compiler_params=pltpu.CompilerParams(dimension_semantics=("parallel", "arbitrary"),
                                             vmem_limit_bytes=VMEM_LIMIT_BYTES),
        name="attn",
    )(qt_blk, k16, vt_blk, bias_diag, bias_prev, g_head_b.reshape(DV, 1), lam_vecs)


def _attn_step_kernel(qt_ref, kp_ref, vp_ref, kn_ref, vn_ref, bpast_ref, bnew_ref, ghb_ref, lam_ref, o_ref):
    tq = qt_ref.shape[-1]
    lv = lam_ref[...]
    lam = (jnp.exp(jnp.sum(lv[0:1, :] * lv[1:2, :], axis=1, keepdims=True))
           - jnp.exp(jnp.sum(lv[2:3, :] * lv[3:4, :], axis=1, keepdims=True)) + LAMBDA_INIT)
    first_map = lax.broadcasted_iota(jnp.int32, (2 * DK, tq), 0) < DK
    kp = kp_ref[...].reshape(kp_ref.shape[0], D_QB).astype(BF16)
    vp = vp_ref[...].reshape(vp_ref.shape[0], D_VB).astype(BF16)
    vn = vn_ref[...].reshape(vn_ref.shape[0], D_VB).astype(BF16)
    for h in range(HB):
        qh = qt_ref[h * 2 * DK:(h + 1) * 2 * DK, :]
        zero = jnp.zeros_like(qh)
        qpad = jnp.concatenate([jnp.where(first_map, qh, zero), jnp.where(first_map, zero, qh)], axis=1)
        s_p = _mm(kp[:, h * 2 * DK:(h + 1) * 2 * DK], qpad) + jnp.concatenate([bpast_ref[h], bpast_ref[h]], axis=1)
        s_n = _mm(kn_ref[h], qpad) + jnp.concatenate([bnew_ref[h], bnew_ref[h]], axis=1)
        m = jnp.maximum(jnp.max(s_p, axis=0, keepdims=True), jnp.max(s_n, axis=0, keepdims=True))
        p_p = jnp.exp2(s_p - m)
        p_n = jnp.exp2(s_n - m)
        den = jnp.sum(p_p, axis=0, keepdims=True) + jnp.sum(p_n, axis=0, keepdims=True)
        num = (lax.dot_general(vp[:, h * DV:(h + 1) * DV], p_p.astype(BF16), TN_DIMS, preferred_element_type=F32)
               + lax.dot_general(vn[:, h * DV:(h + 1) * DV], p_n.astype(BF16), TN_DIMS,
                                 preferred_element_type=F32))
        on = num / den
        o = on[:, 0:tq] - lam * on[:, tq:2 * tq]
        o = o * lax.rsqrt(jnp.mean(o * o, axis=0, keepdims=True) + EPS) * (ghb_ref[...] * (1.0 - LAMBDA_INIT))
        o_ref[:, h * DV:(h + 1) * DV] = o.T.astype(o_ref.dtype)


def _attn_step(qt, k_past, v_past, k_new16, v_new, bias_past, bias_new, g_head_b, lam_vecs):
    bsz, _, t = qt.shape
    past = k_past.shape[1]
    return pl.pallas_call(
        _attn_step_kernel,
        grid=(bsz,),
        in_specs=[pl.BlockSpec((None, D_QB, t), lambda b: (b, 0, 0)),
                  pl.BlockSpec((None, past, HB, 2 * DK), lambda b: (b, 0, 0, 0)),
                  pl.BlockSpec((None, past, HB, DV), lambda b: (b, 0, 0, 0)),
                  pl.BlockSpec((HB, None, t, 2 * DK), lambda b: (0, b, 0, 0)),
                  pl.BlockSpec((None, t, HB, DV), lambda b: (b, 0, 0, 0)),
                  _const_spec(bias_past.shape), _const_spec(bias_new.shape),
                  _const_spec((DV, 1)), _const_spec((4, DK))],
        out_specs=pl.BlockSpec((None, t, D_VB), lambda b: (b, 0, 0)),
        out_shape=jax.ShapeDtypeStruct((bsz, t, D_VB), BF16),
        compiler_params=pltpu.CompilerParams(dimension_semantics=("parallel",),
                                             vmem_limit_bytes=VMEM_LIMIT_BYTES),
        name="attn_step",
    )(qt, k_past, v_past, k_new16, v_new, bias_past, bias_new, g_head_b.reshape(DV, 1), lam_vecs)


def _out_kernel(x_ref, h_ref, ob_ref, gha_ref, gpre_ref, gpost_ref, gpref_ref, gpostf_ref, woa_ref, boa_ref,
                wga_ref, bga_ref, wgb_ref, bgb_ref, wpa_ref, wpb_ref, wout_ref, wff1_ref, wff2_ref, o_ref):
    x = x_ref[...]
    u = _rms(x, gpre_ref[...]).astype(BF16)
    hn = jnp.concatenate([_rms(h_ref[:, h * DH:(h + 1) * DH], gha_ref[h:h + 1, :]) for h in range(HA)], axis=1)
    h_a = (hn * jax.nn.sigmoid(_mm(u, woa_ref[...]) + boa_ref[...])).astype(BF16)
    y_a = _mm(h_a, wpa_ref[...])
    y_b = _mm(ob_ref[...], wpb_ref[...])
    gate_a = jax.nn.sigmoid(_mm(u, wga_ref[...]) + bga_ref[...])
    gate_b = jax.nn.sigmoid(_mm(u, wgb_ref[...]) + bgb_ref[...])
    mix = _mm((gate_a * y_a + gate_b * y_b).astype(BF16), wout_ref[...])
    x1 = x + _rms(mix, gpost_ref[...])
    f = _rms(x1, gpref_ref[...]).astype(BF16)
    hf = jnp.zeros_like(x1)
    for c in range(FF // FF_CHUNK):
        mid = jnp.maximum(_mm(f, wff1_ref[:, c * FF_CHUNK:(c + 1) * FF_CHUNK]), 0.0)
        hf = hf + _mm((mid * mid).astype(BF16), wff2_ref[c * FF_CHUNK:(c + 1) * FF_CHUNK, :])
    o_ref[...] = x1 + _rms(hf, gpostf_ref[...])


def _out(x2d, h2d, ob2d, g_head_a, g_pre_mix, g_post_mix, g_pre_ffn, g_post_ffn, w_in, b_in, w_pa, w_pb, w_out,
         w_ff1, w_ff2):
    rows = x2d.shape[0]
    tm = min(rows, OUT_ROW_TILE)
    o_oa = 3 * D_A
    o_ga = 4 * D_A + 2 * HA + 2 * D_QB + D_VB
    o_gb = o_ga + D_MODEL

    def cols(lo, n):
        return w_in[:, lo:lo + n].astype(BF16), b_in[lo:lo + n].reshape(1, n)

    woa, boa = cols(o_oa, D_A)
    wga, bga = cols(o_ga, D_MODEL)
    wgb, bgb = cols(o_gb, D_MODEL)
    consts = (g_head_a, g_pre_mix.reshape(1, -1), g_post_mix.reshape(1, -1), g_pre_ffn.reshape(1, -1),
              g_post_ffn.reshape(1, -1), woa, boa, wga, bga, wgb, bgb, w_pa.astype(BF16), w_pb.astype(BF16),
              w_out.astype(BF16), w_ff1.astype(BF16), w_ff2.astype(BF16))

    assert rows % tm == 0

    def row_spec(n):
        return pl.BlockSpec((tm, n), lambda i: (i, 0))

    return pl.pallas_call(
        _out_kernel,
        grid=(rows // tm,),
        in_specs=[row_spec(D_MODEL), row_spec(D_A), row_spec(D_VB)] + [_const_spec(c.shape) for c in consts],
        out_specs=row_spec(D_MODEL),
        out_shape=jax.ShapeDtypeStruct((rows, D_MODEL), F32),
        compiler_params=pltpu.CompilerParams(dimension_semantics=("parallel",),
                                             vmem_limit_bytes=VMEM_LIMIT_BYTES),
        name="out",
    )(x2d, h2d, ob2d, *consts)


def _layer(x, k_past, v_past, c0, n0, m0, conv0, rel_bias, g_pre_mix, g_post_mix, g_pre_ffn, g_post_ffn,
           w_in, b_in, conv_w, conv_b, g_head_a, w_pa, lam_vecs, g_head_b, w_pb, w_out, w_ff1, w_ff2):
    bsz, t, _ = x.shape
    past = k_past.shape[1]
    x2d = x.reshape(bsz * t, D_MODEL)
    qk_a, v_a, gates, k_new, k16, v_new, qt_blk, vt_blk = _proj(x2d, g_pre_mix, w_in, b_in)

    h_a, c_new, n_new, m_new = _mlstm(qk_a.reshape(bsz, t, 2 * D_A), v_a.reshape(bsz, t, D_A),
                                      gates, conv0, c0, n0, m0, conv_w, conv_b)

    if past == 0:
        tq = tk = min(t, ROW_TILE, KEY_TILE)
        s_idx = lax.broadcasted_iota(jnp.int32, (tk, tq), 0)
        t_idx = lax.broadcasted_iota(jnp.int32, (tk, tq), 1)
        allowed = (s_idx // CHUNK) <= (t_idx // CHUNK)
        bias_diag = jnp.where(allowed[None], _bias_tiles(rel_bias, 0, tk, tq), NEG)
        corner = min(BIAS_CORNER, tk)
        bias_prev = _bias_tiles(rel_bias, -corner, corner, corner)
        o_b = _attn(qt_blk, k16.reshape(HB, bsz, t, 2 * DK), vt_blk, bias_diag, bias_prev, g_head_b, lam_vecs, bsz, t,
                    tq, tk)
    else:
        nk = past + t
        s_idx = lax.broadcasted_iota(jnp.int32, (nk, t), 0)
        t_idx = lax.broadcasted_iota(jnp.int32, (nk, t), 1)
        allowed = (s_idx // CHUNK) <= ((past + t_idx) // CHUNK)
        bias = jnp.where(allowed[None], _bias_tiles(rel_bias, -past, nk, t), NEG)
        o_b = _attn_step(jnp.transpose(qt_blk.reshape(D_QB, bsz, t), (1, 0, 2)), k_past, v_past,
                         k16.reshape(HB, bsz, t, 2 * DK), v_new.reshape(bsz, t, HB, DV),
                         bias[:, :past], bias[:, past:], g_head_b, lam_vecs)

    y = _out(x2d, h_a.reshape(bsz * t, D_A), o_b.reshape(bsz * t, D_VB), g_head_a, g_pre_mix, g_post_mix, g_pre_ffn,
             g_post_ffn, w_in, b_in, w_pa, w_pb, w_out, w_ff1, w_ff2)
    return (y.reshape(bsz, t, D_MODEL), k_new.reshape(bsz, t, HB, 2 * DK), v_new.reshape(bsz, t, HB, DV),
            c_new, n_new, m_new[:, 0, :HA], qk_a.reshape(bsz, t, 2 * D_A)[:, t - (CONV_W - 1):, :])


def kernel(x_prompt, x_sample, cache_k, cache_v, state_C, state_n, state_m, state_conv, rel_bias, g_pre_mix,
           g_post_mix, g_pre_ffn, g_post_ffn, w_in, b_in, conv_w, conv_b, g_head_a, w_pa, lambda_q1, lambda_k1,
           lambda_q2, lambda_k2, g_head_b, w_pb, w_out, w_ff1, w_ff2):
    lam_vecs = jnp.concatenate([lambda_q1, lambda_k1, lambda_q2, lambda_k2], axis=0)
    weights = (rel_bias, g_pre_mix[0], g_post_mix[0], g_pre_ffn[0], g_post_ffn[0], w_in[0], b_in[0], conv_w[0],
               conv_b[0], g_head_a[0], w_pa[0], lam_vecs, g_head_b[0], w_pb[0], w_out[0], w_ff1[0], w_ff2[0])
    bp = x_prompt.shape[0]
    prompt = _layer(x_prompt, jnp.zeros((bp, 0, HB, 2 * DK), F32), jnp.zeros((bp, 0, HB, DV), F32),
                    jnp.zeros((bp, HA, DH, DH), F32), jnp.zeros((bp, HA, DH), F32), jnp.zeros((bp, HA), F32),
                    jnp.zeros((bp, CONV_W - 1, 2 * D_A), F32), *weights)
    sample = _layer(x_sample, cache_k[0], cache_v[0], state_C[0], state_n[0], state_m[0], state_conv[0], *weights)
    yp, ys = prompt[0], sample[0]
    return (yp, ys) + tuple(a[None] for a in prompt[1:]) + tuple(a[None] for a in sample[1:])
```

```python
import functools
import math

import numpy as np
import jax
import jax.numpy as jnp
from jax import lax
from jax.experimental import pallas as pl
from jax.experimental.pallas import tpu as pltpu

F32 = jnp.float32
BF16 = jnp.bfloat16

D_MODEL = 1024
HA = 4
DH = 128
D_A = HA * DH
HB = 4
DK = 64
DV = 2 * DK
D_QB = HB * 2 * DK
D_VB = HB * DV
CONV_W = 4
FF = 4 * D_MODEL
NUM_BUCKETS = 32
MAX_DISTANCE = 128
CHUNK = 64
MLSTM_CHUNK = 128
EPS = 1e-6
LAMBDA_INIT = 0.8 - 0.6 * math.exp(-0.3 * 0)
NEG = -1e30
LOG2E = math.log2(math.e)
QSCALE = DK ** -0.5 * LOG2E
GATE_ROWS = 16
ROW_TILE = 512
KEY_TILE = 512
ONES_ROWS = 16
S_PAD_LANES = 128
BIAS_CORNER = 128
OUT_ROW_TILE = 512
MLSTM_ROW_TILE = 1024
FF_CHUNK = 1024
VMEM_LIMIT_BYTES = 56 * 1024 * 1024

NT_DIMS = (((1,), (1,)), ((), ()))
TN_DIMS = (((0,), (0,)), ((), ()))


def _const_spec(shape):
    zeros = (0,) * len(shape)
    return pl.BlockSpec(shape, lambda *_: zeros, pipeline_mode=pl.Buffered(1))


def _rms(x, g):
    return x * lax.rsqrt(jnp.mean(x * x, axis=-1, keepdims=True) + EPS) * g


def _log_sigmoid(x):
    return -(jnp.maximum(-x, 0.0) + jnp.log(1.0 + jnp.exp(-jnp.abs(x))))


def _split3(x):
    x1 = x.astype(BF16)
    r = x - x1.astype(F32)
    x2 = r.astype(BF16)
    r = r - x2.astype(F32)
    return x1, x2, r.astype(BF16)


def _mm(a, b):
    return jnp.dot(a, b, preferred_element_type=F32)


def _proj_kernel(x_ref, g_ref, wqk_ref, bqk_ref, wva_ref, bva_ref, wk_ref, bk_ref,
                 wv_ref, bv_ref, wqt_ref, bqt_ref,
                 qk_ref, va_ref, gate_ref, k_ref, k16_ref, v_ref, qt_ref, vt_ref):
    u = _rms(x_ref[...], g_ref[...]).astype(BF16)
    qk_ref[...] = _mm(u, wqk_ref[...]) + bqk_ref[...]
    va_ref[...] = _mm(u, wva_ref[...]) + bva_ref[...]
    k = _mm(u, wk_ref[...]) + bk_ref[...]
    k_ref[...] = k.reshape(k_ref.shape)
    k16 = k.astype(BF16)
    for h in range(HB):
        k16_ref[h] = k16[:, h * 2 * DK:(h + 1) * 2 * DK]
    v = _mm(u, wv_ref[...]) + bv_ref[...]
    v_ref[...] = v.reshape(v_ref.shape)
    qt = lax.dot_general(wqt_ref[...], u, NT_DIMS, preferred_element_type=F32) + bqt_ref[...]
    qt_ref[...] = (qt[0:D_QB, :] * QSCALE).astype(BF16)
    gate_ref[...] = qt[D_QB:D_QB + 2 * HA, :]
    vt = v.T.astype(BF16)
    tk = vt_ref.shape[-1]
    for h in range(HB):
        for c in range(vt_ref.shape[1]):
            vt_ref[h, c, 0:DV, :] = vt[h * DV:(h + 1) * DV, c * tk:(c + 1) * tk]
            vt_ref[h, c, DV:DV + ONES_ROWS, :] = jnp.ones((ONES_ROWS, tk), BF16)


def _proj(x2d, g_pre, w_in, b_in):
    rows = x2d.shape[0]
    tm = min(rows, ROW_TILE)
    assert rows % tm == 0
    nblk = rows // tm
    tk = min(tm, KEY_TILE)
    o_qk, o_va, o_oa, o_i, o_q, o_k, o_v, o_ga = 0, 2 * D_A, 3 * D_A, 4 * D_A, 4 * D_A + 2 * HA, \
        4 * D_A + 2 * HA + D_QB, 4 * D_A + 2 * HA + 2 * D_QB, 4 * D_A + 2 * HA + 2 * D_QB + D_VB
    del o_oa, o_ga

    def cols(lo, n):
        return w_in[:, lo:lo + n].astype(BF16), b_in[lo:lo + n].reshape(1, n)

    wqk, bqk = cols(o_qk, 2 * D_A)
    wva, bva = cols(o_va, D_A)
    wk, bk = cols(o_k, D_QB)
    wv, bv = cols(o_v, D_VB)
    t_cols = jnp.concatenate([w_in[:, o_q:o_q + D_QB], w_in[:, o_i:o_i + 2 * HA],
                              jnp.zeros((D_MODEL, GATE_ROWS - 2 * HA), F32)], axis=1)
    t_bias = jnp.concatenate([b_in[o_q:o_q + D_QB], b_in[o_i:o_i + 2 * HA], jnp.zeros((GATE_ROWS - 2 * HA,), F32)])
    wqt = t_cols.T.astype(BF16)
    bqt = t_bias.reshape(D_QB + GATE_ROWS, 1)
    consts = (g_pre.reshape(1, D_MODEL), wqk, bqk, wva, bva, wk, bk, wv, bv, wqt, bqt)

    def row_spec(n):
        return pl.BlockSpec((tm, n), lambda i: (i, 0))

    return pl.pallas_call(
        _proj_kernel,
        grid=(nblk,),
        in_specs=[row_spec(D_MODEL)] + [_const_spec(c.shape) for c in consts],
        out_specs=[row_spec(2 * D_A), row_spec(D_A), pl.BlockSpec((2 * HA, tm), lambda i: (0, i)),
                   pl.BlockSpec((tm, HB, 2 * DK), lambda i: (i, 0, 0)),
                   pl.BlockSpec((HB, tm, 2 * DK), lambda i: (0, i, 0)),
                   pl.BlockSpec((tm, HB, DV), lambda i: (i, 0, 0)),
                   pl.BlockSpec((None, D_QB, tm), lambda i: (i, 0, 0)),
                   pl.BlockSpec((HB, tm // tk, DV + ONES_ROWS, tk), lambda i: (0, i, 0, 0))],
        out_shape=[jax.ShapeDtypeStruct((rows, 2 * D_A), F32), jax.ShapeDtypeStruct((rows, D_A), F32),
                   jax.ShapeDtypeStruct((2 * HA, rows), F32), jax.ShapeDtypeStruct((rows, HB, 2 * DK), F32),
                   jax.ShapeDtypeStruct((HB, rows, 2 * DK), BF16), jax.ShapeDtypeStruct((rows, HB, DV), F32),
                   jax.ShapeDtypeStruct((nblk, D_QB, tm), BF16),
                   jax.ShapeDtypeStruct((HB, rows // tk, DV + ONES_ROWS, tk), BF16)],
        compiler_params=pltpu.CompilerParams(dimension_semantics=("parallel",),
                                             vmem_limit_bytes=VMEM_LIMIT_BYTES),
        name="proj",
    )(x2d, *consts)


def _mlstm_kernel(qk_ref, va_ref, gcol_ref, grow_ref, conv0_ref, c0_ref, n0_ref, m0_ref, cw_ref, cb_ref,
                  h_ref, c_ref, n_ref, m_ref, xp_sc, q_sc, k_sc, st_sc, m_sc, *, chunk, n_chunks):
    j = pl.program_id(1)
    tb = chunk * n_chunks

    @pl.when(j == 0)
    def _():
        xp_sc[0:8, :] = conv0_ref[...]
        for h in range(HA):
            st_sc[h, :, 0:DH] = c0_ref[h].T
            st_sc[h, :, DH:2 * DH] = jnp.broadcast_to(n0_ref[h:h + 1, :], (DH, DH)).T
        m_sc[...] = m0_ref[...]

    @pl.when(j > 0)
    def _():
        xp_sc[0:8, :] = xp_sc[tb:tb + 8, :]

    xp_sc[8:8 + tb, :] = qk_ref[...]
    cw = cw_ref[...]
    conv = cb_ref[...] + xp_sc[8:8 + tb, :] * cw[CONV_W - 1:CONV_W, :]
    for t in range(CONV_W - 1):
        conv = conv + xp_sc[8 - (CONV_W - 1) + t:8 - (CONV_W - 1) + t + tb, :] * cw[t:t + 1, :]
    half = 0.5 * conv
    act = half + half * jnp.tanh(half)
    q_sc[...] = act[:, :D_A].astype(BF16)
    k_sc[...] = (act[:, D_A:] * DH ** -0.5).astype(BF16)

    rr = lax.broadcasted_iota(jnp.int32, (chunk, chunk), 0)
    cc = lax.broadcasted_iota(jnp.int32, (chunk, chunk), 1)
    tril = cc <= rr
    ltri = tril.astype(BF16)
    utri = (rr <= cc).astype(BF16)
    ones = jnp.ones((chunk, DH), F32)

    for c in range(n_chunks):
        sl = slice(c * chunk, (c + 1) * chunk)
        gc = gcol_ref[sl, :]
        ig_c = gc[:, 0:HA]
        b_c = sum(_mm(ltri, p) for p in _split3(_log_sigmoid(gc[:, HA:2 * HA])))
        g = b_c[chunk - 1:chunk, :]
        ls_c = g - b_c + ig_c
        m_old = m_sc[0:1, 0:HA]
        m_new = jnp.maximum(g + m_old, jnp.max(ls_c, axis=0, keepdims=True))
        ws_c = jnp.exp(ls_c - m_new)
        decay = jnp.exp(g + m_old - m_new)
        a_c = b_c + m_old
        run = ig_c - b_c
        shift = 1
        while shift < chunk:
            run = jnp.maximum(run, jnp.concatenate([jnp.full((shift, HA), NEG, F32), run[:chunk - shift, :]], axis=0))
            shift *= 2
        mt_c = jnp.maximum(a_c, b_c + run)
        m_sc[0:1, 0:HA] = m_new
        gr = grow_ref[c]
        b_r = sum(_mm(p, utri) for p in _split3(_log_sigmoid(gr[HA:2 * HA, :])))
        c_r = gr[0:HA, :] - b_r
        for h in range(HA):
            hs = slice(h * DH, (h + 1) * DH)
            dlog = jnp.where(tril, b_c[:, h:h + 1] + c_r[h:h + 1, :], NEG)
            a_h = a_c[:, h:h + 1]
            m_t = mt_c[:, h:h + 1]
            p = jnp.exp(dlog - m_t)
            w_int = jnp.exp(a_h - m_t)
            q_h = q_sc[sl, hs]
            k_h = k_sc[sl, hs]
            vaug = jnp.concatenate([va_ref[sl, hs], ones], axis=1)
            s = (p * lax.dot_general(q_h, k_h, NT_DIMS, preferred_element_type=F32)).astype(BF16)
            st = st_sc[h]
            acc = w_int * _mm(q_h, st.astype(BF16)) + _mm(s, vaug.astype(BF16))
            hh = acc[:, :DH] / jnp.maximum(jnp.abs(acc[:, DH:]), jnp.exp(-m_t))
            h_ref[sl, hs] = hh
            wv = (ws_c[:, h:h + 1] * vaug).astype(BF16)
            st_sc[h] = decay[:, h:h + 1] * st + lax.dot_general(k_h, wv, TN_DIMS, preferred_element_type=F32)

    @pl.when(j == pl.num_programs(1) - 1)
    def _():
        for h in range(HA):
            st = st_sc[h]
            c_ref[h] = st[:, :DH].T
            n_ref[h:h + 1, :] = st[:, DH:].T[0:1, :]
        m_ref[...] = m_sc[...]


def _mlstm(qk_a, v_a, gates_t, conv0, c0, n0, m0, conv_w, conv_b):
    bsz, t, _ = qk_a.shape
    chunk = min(t, MLSTM_CHUNK)
    n_chunks = min(t // chunk, MLSTM_ROW_TILE // chunk)
    tb = chunk * n_chunks
    assert t % tb == 0
    nblk = t // tb
    grow = gates_t.reshape(2 * HA, bsz, t // chunk, chunk).transpose(1, 2, 0, 3)
    gates = gates_t.T.reshape(bsz, t, 2 * HA)
    conv0p = jnp.pad(conv0, ((0, 0), (8 - (CONV_W - 1), 0), (0, 0)))
    m0p = jnp.pad(m0, ((0, 0), (0, 128 - HA)))[:, None, :] * jnp.ones((1, 8, 1), F32)
    kern = functools.partial(_mlstm_kernel, chunk=chunk, n_chunks=n_chunks)
    return pl.pallas_call(
        kern,
        grid=(bsz, nblk),
        in_specs=[pl.BlockSpec((None, tb, 2 * D_A), lambda b, j: (b, j, 0)),
                  pl.BlockSpec((None, tb, D_A), lambda b, j: (b, j, 0)),
                  pl.BlockSpec((None, tb, 2 * HA), lambda b, j: (b, j, 0)),
                  pl.BlockSpec((None, n_chunks, 2 * HA, chunk), lambda b, j: (b, j, 0, 0)),
                  pl.BlockSpec((None, 8, 2 * D_A), lambda b, j: (b, 0, 0)),
                  pl.BlockSpec((None, HA, DH, DH), lambda b, j: (b, 0, 0, 0)),
                  pl.BlockSpec((None, HA, DH), lambda b, j: (b, 0, 0)),
                  pl.BlockSpec((None, 8, 128), lambda b, j: (b, 0, 0)),
                  _const_spec((CONV_W, 2 * D_A)), _const_spec((1, 2 * D_A))],
        out_specs=[pl.BlockSpec((None, tb, D_A), lambda b, j: (b, j, 0)),
                   pl.BlockSpec((None, HA, DH, DH), lambda b, j: (b, 0, 0, 0)),
                   pl.BlockSpec((None, HA, DH), lambda b, j: (b, 0, 0)),
                   pl.BlockSpec((None, 8, 128), lambda b, j: (b, 0, 0))],
        out_shape=[jax.ShapeDtypeStruct((bsz, t, D_A), F32), jax.ShapeDtypeStruct((bsz, HA, DH, DH), F32),
                   jax.ShapeDtypeStruct((bsz, HA, DH), F32), jax.ShapeDtypeStruct((bsz, 8, 128), F32)],
        scratch_shapes=[pltpu.VMEM((tb + 8, 2 * D_A), F32), pltpu.VMEM((tb, D_A), BF16),
                        pltpu.VMEM((tb, D_A), BF16), pltpu.VMEM((HA, DH, 2 * DH), F32),
                        pltpu.VMEM((8, 128), F32)],
        compiler_params=pltpu.CompilerParams(dimension_semantics=("parallel", "arbitrary"),
                                             vmem_limit_bytes=VMEM_LIMIT_BYTES),
        name="mlstm",
    )(qk_a, v_a, gates, grow, conv0p, c0, n0, m0p, conv_w, conv_b.reshape(1, 2 * D_A))


def _attn_kernel(qt_ref, k_ref, vt_ref, bd_ref, bp_ref, ghb_ref, lam_ref, o_ref,
                 qpad_sc, m_sc, acc_sc, s_sc, mx_sc, *, tq, tk, n_qblk):
    i = pl.program_id(1)
    lv = lam_ref[...]
    lam = (jnp.exp(jnp.sum(lv[0:1, :] * lv[1:2, :], axis=1, keepdims=True))
           - jnp.exp(jnp.sum(lv[2:3, :] * lv[3:4, :], axis=1, keepdims=True)) + LAMBDA_INIT)

    first_map = lax.broadcasted_iota(jnp.int32, (2 * DK, tq), 0) < DK
    for h in range(HB):
        qh = qt_ref[h * 2 * DK:(h + 1) * 2 * DK, :]
        zero = jnp.zeros_like(qh)
        qpad_sc[h, :, 0:tq] = jnp.where(first_map, qh, zero)
        qpad_sc[h, :, tq:2 * tq] = jnp.where(first_map, zero, qh)
    m_sc[...] = jnp.full(m_sc.shape, NEG, F32)
    acc_sc[...] = jnp.zeros(acc_sc.shape, F32)

    corner = bp_ref.shape[-1]

    def logits(jb, h):
        row0 = pl.multiple_of(jb * tk, tk)
        return _mm(k_ref[h, pl.ds(row0, tk), :], qpad_sc[h])

    def put(slot, s):
        s_sc[slot, :, 0:2 * tq] = s
        mx_sc[slot] = jnp.max(s, axis=0, keepdims=True)

    def stage1(jb, h, slot, is_prev):
        put(slot, logits(jb, h))
        if is_prev is not None:
            bias = jnp.where(is_prev, bp_ref[h], 0.0)
            for c0 in (0, tq):
                sc = s_sc[slot, tk - corner:tk, c0:c0 + corner] + bias
                s_sc[slot, tk - corner:tk, c0:c0 + corner] = sc
                mx_sc[slot, :, c0:c0 + corner] = jnp.maximum(mx_sc[slot, :, c0:c0 + corner],
                                                             jnp.max(sc, axis=0, keepdims=True))

    def diag_bias(h):
        return jnp.concatenate([bd_ref[h], bd_ref[h]], axis=1)

    def stage2(jb, h, slot):
        m_old = m_sc[h]
        m_new = jnp.maximum(m_old, mx_sc[slot])
        p = jnp.exp2(s_sc[slot, :, 0:2 * tq] - m_new).astype(BF16)
        acc_sc[h] = jnp.exp2(m_old - m_new) * acc_sc[h] + _mm(vt_ref[h, jb], p)
        m_sc[h] = m_new

    if n_qblk > 1:
        stage1(0, 0, 0, i == 1)

        def one_block(jb):
            is_prev = jb == i - 1
            stage1(jb, 1, 1, is_prev)
            stage2(jb, 0, 0)
            stage1(jb, 2, 0, is_prev)
            stage2(jb, 1, 1)
            stage1(jb, 3, 1, is_prev)
            stage2(jb, 2, 0)
            stage1(jb + 1, 0, 0, jb == i - 2)
            stage2(jb, 3, 1)

        def two_blocks(jj, carry):
            one_block(2 * jj)
            one_block(2 * jj + 1)
            return carry
        lax.fori_loop(0, i // 2, two_blocks, 0)

        @pl.when(i % 2 == 1)
        def _():
            one_block(i - 1)
    else:
        stage1(0, 0, 0, None)

    put(0, s_sc[0, :, 0:2 * tq] + diag_bias(0))
    put(1, logits(i, 1) + diag_bias(1))
    stage2(i, 0, 0)
    put(0, logits(i, 2) + diag_bias(2))
    stage2(i, 1, 1)
    put(1, logits(i, 3) + diag_bias(3))
    stage2(i, 2, 0)
    stage2(i, 3, 1)

    for h in range(HB):
        on = acc_sc[h, 0:DV, :] / acc_sc[h, DV:DV + 1, :]
        o = on[:, 0:tq] - lam * on[:, tq:2 * tq]
        o = o * lax.rsqrt(jnp.mean(o * o, axis=0, keepdims=True) + EPS) * (ghb_ref[...] * (1.0 - LAMBDA_INIT))
        o_ref[:, h * DV:(h + 1) * DV] = o.T.astype(o_ref.dtype)


def _bucket(rel):
    half, max_exact = NUM_BUCKETS // 2, NUM_BUCKETS // 4
    n = np.abs(rel).astype(np.int64)
    sq = np.maximum(n * n // (max_exact * max_exact), 1)
    large = max_exact + np.floor(np.log2(sq.astype(np.float64)) + 1e-9).astype(np.int64)
    large = np.minimum(large, half - 1)
    return np.where(rel > 0, half, 0) + np.where(n < max_exact, n, large)


def _bias_tiles(rel_bias, rel0, nk, tq):
    table = (rel_bias - rel_bias[NUM_BUCKETS // 2 - 1:NUM_BUCKETS // 2, :]) * LOG2E
    half, max_exact = NUM_BUCKETS // 2, NUM_BUCKETS // 4
    dist = np.arange(4 * MAX_DISTANCE)
    buckets = _bucket(-dist)
    rel = rel0 + lax.broadcasted_iota(jnp.int32, (nk, tq), 0) - lax.broadcasted_iota(jnp.int32, (nk, tq), 1)
    n = jnp.abs(rel)
    large = jnp.full((nk, tq), max_exact, jnp.int32)
    for b in range(max_exact + 1, half):
        large = large + (n >= int(dist[buckets >= b][0])).astype(jnp.int32)
    bucket = jnp.where(rel > 0, half, 0) + jnp.where(n < max_exact, n, large)
    tile = jnp.zeros((HB, nk, tq), F32)
    for b in range(NUM_BUCKETS):
        tile = jnp.where((bucket == b)[None], table[b][:, None, None], tile)
    return tile


def _attn(qt_blk, k16, vt_blk, bias_diag, bias_prev, g_head_b, lam_vecs, bsz, t, tq, tk):
    n_qblk = t // tq
    nk = k16.shape[2]
    n_kblk = nk // tk
    assert t % tq == 0 and nk % tk == 0 and (n_qblk == 1 or tq == tk)
    kern = functools.partial(_attn_kernel, tq=tq, tk=tk, n_qblk=n_qblk)
    return pl.pallas_call(
        kern,
        grid=(bsz, n_qblk),
        in_specs=[pl.BlockSpec((None, D_QB, tq), lambda b, i: (b * n_qblk + i, 0, 0)),
                  pl.BlockSpec((HB, None, nk, 2 * DK), lambda b, i: (0, b, 0, 0), pipeline_mode=pl.Buffered(1)),
                  pl.BlockSpec((HB, n_kblk, DV + ONES_ROWS, tk), lambda b, i: (0, b, 0, 0),
                               pipeline_mode=pl.Buffered(1)),
                  _const_spec(bias_diag.shape), _const_spec(bias_prev.shape),
                  _const_spec((DV, 1)), _const_spec((4, DK))],
        out_specs=pl.BlockSpec((None, tq, D_VB), lambda b, i: (b, i, 0)),
        out_shape=jax.ShapeDtypeStruct((bsz, t, D_VB), BF16),
        scratch_shapes=[pltpu.VMEM((HB, 2 * DK, 2 * tq), BF16), pltpu.VMEM((HB, 1, 2 * tq), F32),
                        pltpu.VMEM((HB, DV + ONES_ROWS, 2 * tq), F32),
                        pltpu.VMEM((2, tk, 2 * tq + S_PAD_LANES), F32), pltpu.VMEM((2, 1, 2 * tq), F32)],
        compiler_params=pltpu.CompilerParams(dimension_semantics=("parallel", "arbitrary"),
                                             vmem_limit_bytes=VMEM_LIMIT_BYTES),
        name="attn",
    )(qt_blk, k16, vt_blk, bias_diag, bias_prev, g_head_b.reshape(DV, 1), lam_vecs)


def _attn_step_kernel(qt_ref, kp_ref, vp_ref, kn_ref, vn_ref, bpast_ref, bnew_ref, ghb_ref, lam_ref, o_ref):
    tq = qt_ref.shape[-1]
    lv = lam_ref[...]
    lam = (jnp.exp(jnp.sum(lv[0:1, :] * lv[1:2, :], axis=1, keepdims=True))
           - jnp.exp(jnp.sum(lv[2:3, :] * lv[3:4, :], axis=1, keepdims=True)) + LAMBDA_INIT)
    first_map = lax.broadcasted_iota(jnp.int32, (2 * DK, tq), 0) < DK
    kp = kp_ref[...].reshape(kp_ref.shape[0], D_QB).astype(BF16)
    vp = vp_ref[...].reshape(vp_ref.shape[0], D_VB).astype(BF16)
    vn = vn_ref[...].reshape(vn_ref.shape[0], D_VB).astype(BF16)
    for h in range(HB):
        qh = qt_ref[h * 2 * DK:(h + 1) * 2 * DK, :]
        zero = jnp.zeros_like(qh)
        qpad = jnp.concatenate([jnp.where(first_map, qh, zero), jnp.where(first_map, zero, qh)], axis=1)
        s_p = _mm(kp[:, h * 2 * DK:(h + 1) * 2 * DK], qpad) + jnp.concatenate([bpast_ref[h], bpast_ref[h]], axis=1)
        s_n = _mm(kn_ref[h], qpad) + jnp.concatenate([bnew_ref[h], bnew_ref[h]], axis=1)
        m = jnp.maximum(jnp.max(s_p, axis=0, keepdims=True), jnp.max(s_n, axis=0, keepdims=True))
        p_p = jnp.exp2(s_p - m)
        p_n = jnp.exp2(s_n - m)
        den = jnp.sum(p_p, axis=0, keepdims=True) + jnp.sum(p_n, axis=0, keepdims=True)
        num = (lax.dot_general(vp[:, h * DV:(h + 1) * DV], p_p.astype(BF16), TN_DIMS, preferred_element_type=F32)
               + lax.dot_general(vn[:, h * DV:(h + 1) * DV], p_n.astype(BF16), TN_DIMS,
                                 preferred_element_type=F32))
        on = num / den
        o = on[:, 0:tq] - lam * on[:, tq:2 * tq]
        o = o * lax.rsqrt(jnp.mean(o * o, axis=0, keepdims=True) + EPS) * (ghb_ref[...] * (1.0 - LAMBDA_INIT))
        o_ref[:, h * DV:(h + 1) * DV] = o.T.astype(o_ref.dtype)


def _attn_step(qt, k_past, v_past, k_new16, v_new, bias_past, bias_new, g_head_b, lam_vecs):
    bsz, _, t = qt.shape
    past = k_past.shape[1]
    return pl.pallas_call(
        _attn_step_kernel,
        grid=(bsz,),
        in_specs=[pl.BlockSpec((None, D_QB, t), lambda b: (b, 0, 0)),
                  pl.BlockSpec((None, past, HB, 2 * DK), lambda b: (b, 0, 0, 0)),
                  pl.BlockSpec((None, past, HB, DV), lambda b: (b, 0, 0, 0)),
                  pl.BlockSpec((HB, None, t, 2 * DK), lambda b: (0, b, 0, 0)),
                  pl.BlockSpec((None, t, HB, DV), lambda b: (b, 0, 0, 0)),
                  _const_spec(bias_past.shape), _const_spec(bias_new.shape),
                  _const_spec((DV, 1)), _const_spec((4, DK))],
        out_specs=pl.BlockSpec((None, t, D_VB), lambda b: (b, 0, 0)),
        out_shape=jax.ShapeDtypeStruct((bsz, t, D_VB), BF16),
        compiler_params=pltpu.CompilerParams(dimension_semantics=("parallel",),
                                             vmem_limit_bytes=VMEM_LIMIT_BYTES),
        name="attn_step",
    )(qt, k_past, v_past, k_new16, v_new, bias_past, bias_new, g_head_b.reshape(DV, 1), lam_vecs)


def _out_kernel(x_ref, h_ref, ob_ref, gha_ref, gpre_ref, gpost_ref, gpref_ref, gpostf_ref, woa_ref, boa_ref,
                wga_ref, bga_ref, wgb_ref, bgb_ref, wpa_ref, wpb_ref, wout_ref, wff1_ref, wff2_ref, o_ref):
    x = x_ref[...]
    u = _rms(x, gpre_ref[...]).astype(BF16)
    hn = jnp.concatenate([_rms(h_ref[:, h * DH:(h + 1) * DH], gha_ref[h:h + 1, :]) for h in range(HA)], axis=1)
    h_a = (hn * jax.nn.sigmoid(_mm(u, woa_ref[...]) + boa_ref[...])).astype(BF16)
    y_a = _mm(h_a, wpa_ref[...])
    y_b = _mm(ob_ref[...], wpb_ref[...])
    gate_a = jax.nn.sigmoid(_mm(u, wga_ref[...]) + bga_ref[...])
    gate_b = jax.nn.sigmoid(_mm(u, wgb_ref[...]) + bgb_ref[...])
    mix = _mm((gate_a * y_a + gate_b * y_b).astype(BF16), wout_ref[...])
    x1 = x + _rms(mix, gpost_ref[...])
    f = _rms(x1, gpref_ref[...]).astype(BF16)
    hf = jnp.zeros_like(x1)
    for c in range(FF // FF_CHUNK):
        mid = jnp.maximum(_mm(f, wff1_ref[:, c * FF_CHUNK:(c + 1) * FF_CHUNK]), 0.0)
        hf = hf + _mm((mid * mid).astype(BF16), wff2_ref[c * FF_CHUNK:(c + 1) * FF_CHUNK, :])
    o_ref[...] = x1 + _rms(hf, gpostf_ref[...])


def _out(x2d, h2d, ob2d, g_head_a, g_pre_mix, g_post_mix, g_pre_ffn, g_post_ffn, w_in, b_in, w_pa, w_pb, w_out,
         w_ff1, w_ff2):
    rows = x2d.shape[0]
    tm = min(rows, OUT_ROW_TILE)
    o_oa = 3 * D_A
    o_ga = 4 * D_A + 2 * HA + 2 * D_QB + D_VB
    o_gb = o_ga + D_MODEL

    def cols(lo, n):
        return w_in[:, lo:lo + n].astype(BF16), b_in[lo:lo + n].reshape(1, n)

    woa, boa = cols(o_oa, D_A)
    wga, bga = cols(o_ga, D_MODEL)
    wgb, bgb = cols(o_gb, D_MODEL)
    consts = (g_head_a, g_pre_mix.reshape(1, -1), g_post_mix.reshape(1, -1), g_pre_ffn.reshape(1, -1),
              g_post_ffn.reshape(1, -1), woa, boa, wga, bga, wgb, bgb, w_pa.astype(BF16), w_pb.astype(BF16),
              w_out.astype(BF16), w_ff1.astype(BF16), w_ff2.astype(BF16))

    assert rows % tm == 0

    def row_spec(n):
        return pl.BlockSpec((tm, n), lambda i: (i, 0))

    return pl.pallas_call(
        _out_kernel,
        grid=(rows // tm,),
        in_specs=[row_spec(D_MODEL), row_spec(D_A), row_spec(D_VB)] + [_const_spec(c.shape) for c in consts],
        out_specs=row_spec(D_MODEL),
        out_shape=jax.ShapeDtypeStruct((rows, D_MODEL), F32),
        compiler_params=pltpu.CompilerParams(dimension_semantics=("parallel",),
                                             vmem_limit_bytes=VMEM_LIMIT_BYTES),
        name="out",
    )(x2d, h2d, ob2d, *consts)


def _layer(x, k_past, v_past, c0, n0, m0, conv0, rel_bias, g_pre_mix, g_post_mix, g_pre_ffn, g_post_ffn,
           w_in, b_in, conv_w, conv_b, g_head_a, w_pa, lam_vecs, g_head_b, w_pb, w_out, w_ff1, w_ff2):
    bsz, t, _ = x.shape
    past = k_past.shape[1]
    x2d = x.reshape(bsz * t, D_MODEL)
    qk_a, v_a, gates, k_new, k16, v_new, qt_blk, vt_blk = _proj(x2d, g_pre_mix, w_in, b_in)

    h_a, c_new, n_new, m_new = _mlstm(qk_a.reshape(bsz, t, 2 * D_A), v_a.reshape(bsz, t, D_A),
                                      gates, conv0, c0, n0, m0, conv_w, conv_b)

    if past == 0:
        tq = tk = min(t, ROW_TILE, KEY_TILE)
        s_idx = lax.broadcasted_iota(jnp.int32, (tk, tq), 0)
        t_idx = lax.broadcasted_iota(jnp.int32, (tk, tq), 1)
        allowed = (s_idx // CHUNK) <= (t_idx // CHUNK)
        bias_diag = jnp.where(allowed[None], _bias_tiles(rel_bias, 0, tk, tq), NEG)
        corner = min(BIAS_CORNER, tk)
        bias_prev = _bias_tiles(rel_bias, -corner, corner, corner)
        o_b = _attn(qt_blk, k16.reshape(HB, bsz, t, 2 * DK), vt_blk, bias_diag, bias_prev, g_head_b, lam_vecs, bsz, t,
                    tq, tk)
    else:
        nk = past + t
        s_idx = lax.broadcasted_iota(jnp.int32, (nk, t), 0)
        t_idx = lax.broadcasted_iota(jnp.int32, (nk, t), 1)
        allowed = (s_idx // CHUNK) <= ((past + t_idx) // CHUNK)
        bias = jnp.where(allowed[None], _bias_tiles(rel_bias, -past, nk, t), NEG)
        o_b = _attn_step(jnp.transpose(qt_blk.reshape(D_QB, bsz, t), (1, 0, 2)), k_past, v_past,
                         k16.reshape(HB, bsz, t, 2 * DK), v_new.reshape(bsz, t, HB, DV),
                         bias[:, :past], bias[:, past:], g_head_b, lam_vecs)

    y = _out(x2d, h_a.reshape(bsz * t, D_A), o_b.reshape(bsz * t, D_VB), g_head_a, g_pre_mix, g_post_mix, g_pre_ffn,
             g_post_ffn, w_in, b_in, w_pa, w_pb, w_out, w_ff1, w_ff2)
    return (y.reshape(bsz, t, D_MODEL), k_new.reshape(bsz, t, HB, 2 * DK), v_new.reshape(bsz, t, HB, DV),
            c_new, n_new, m_new[:, 0, :HA], qk_a.reshape(bsz, t, 2 * D_A)[:, t - (CONV_W - 1):, :])


def kernel(x_prompt, x_sample, cache_k, cache_v, state_C, state_n, state_m, state_conv, rel_bias, g_pre_mix,
           g_post_mix, g_pre_ffn, g_post_ffn, w_in, b_in, conv_w, conv_b, g_head_a, w_pa, lambda_q1, lambda_k1,
           lambda_q2, lambda_k2, g_head_b, w_pb, w_out, w_ff1, w_ff2):
    lam_vecs = jnp.concatenate([lambda_q1, lambda_k1, lambda_q2, lambda_k2], axis=0)
    weights = (rel_bias, g_pre_mix[0], g_post_mix[0], g_pre_ffn[0], g_post_ffn[0], w_in[0], b_in[0], conv_w[0],
               conv_b[0], g_head_a[0], w_pa[0], lam_vecs, g_head_b[0], w_pb[0], w_out[0], w_ff1[0], w_ff2[0])
    bp = x_prompt.shape[0]
    prompt = _layer(x_prompt, jnp.zeros((bp, 0, HB, 2 * DK), F32), jnp.zeros((bp, 0, HB, DV), F32),
                    jnp.zeros((bp, HA, DH, DH), F32), jnp.zeros((bp, HA, DH), F32), jnp.zeros((bp, HA), F32),
                    jnp.zeros((bp, CONV_W - 1, 2 * D_A), F32), *weights)
    sample = _layer(x_sample, cache_k[0], cache_v[0], state_C[0], state_n[0], state_m[0], state_conv[0], *weights)
    yp, ys = prompt[0], sample[0]
    return (yp, ys) + tuple(a[None] for a in prompt[1:]) + tuple(a[None] for a in sample[1:])
```

```python
import functools
import math

import numpy as np
import jax
import jax.numpy as jnp
from jax import lax
from jax.experimental import pallas as pl
from jax.experimental.pallas import tpu as pltpu

F32 = jnp.float32
BF16 = jnp.bfloat16

D_MODEL = 1024
HA = 4
DH = 128
D_A = HA * DH
HB = 4
DK = 64
DV = 2 * DK
D_QB = HB * 2 * DK
D_VB = HB * DV
CONV_W = 4
FF = 4 * D_MODEL
NUM_BUCKETS = 32
MAX_DISTANCE = 128
CHUNK = 64
MLSTM_CHUNK = 128
EPS = 1e-6
LAMBDA_INIT = 0.8 - 0.6 * math.exp(-0.3 * 0)
NEG = -1e30
LOG2E = math.log2(math.e)
QSCALE = DK ** -0.5 * LOG2E
GATE_ROWS = 16
ROW_TILE = 512
KEY_TILE = 512
ONES_ROWS = 16
BIAS_CORNER = 128
OUT_ROW_TILE = 512
MLSTM_ROW_TILE = 1024
FF_CHUNK = 1024
VMEM_LIMIT_BYTES = 56 * 1024 * 1024

NT_DIMS = (((1,), (1,)), ((), ()))
TN_DIMS = (((0,), (0,)), ((), ()))


def _const_spec(shape):
    zeros = (0,) * len(shape)
    return pl.BlockSpec(shape, lambda *_: zeros, pipeline_mode=pl.Buffered(1))


def _rms(x, g):
    return x * lax.rsqrt(jnp.mean(x * x, axis=-1, keepdims=True) + EPS) * g


def _log_sigmoid(x):
    return -(jnp.maximum(-x, 0.0) + jnp.log(1.0 + jnp.exp(-jnp.abs(x))))


def _split3(x):
    x1 = x.astype(BF16)
    r = x - x1.astype(F32)
    x2 = r.astype(BF16)
    r = r - x2.astype(F32)
    return x1, x2, r.astype(BF16)


def _mm(a, b):
    return jnp.dot(a, b, preferred_element_type=F32)


def _proj_kernel(x_ref, g_ref, wqk_ref, bqk_ref, wva_ref, bva_ref, wk_ref, bk_ref,
                 wv_ref, bv_ref, wqt_ref, bqt_ref,
                 qk_ref, va_ref, gate_ref, k_ref, k16_ref, v_ref, qt_ref, vt_ref):
    u = _rms(x_ref[...], g_ref[...]).astype(BF16)
    qk_ref[...] = _mm(u, wqk_ref[...]) + bqk_ref[...]
    va_ref[...] = _mm(u, wva_ref[...]) + bva_ref[...]
    k = _mm(u, wk_ref[...]) + bk_ref[...]
    k_ref[...] = k.reshape(k_ref.shape)
    k16 = k.astype(BF16)
    for h in range(HB):
        k16_ref[h] = k16[:, h * 2 * DK:(h + 1) * 2 * DK]
    v = _mm(u, wv_ref[...]) + bv_ref[...]
    v_ref[...] = v.reshape(v_ref.shape)
    qt = lax.dot_general(wqt_ref[...], u, NT_DIMS, preferred_element_type=F32) + bqt_ref[...]
    qt_ref[...] = (qt[0:D_QB, :] * QSCALE).astype(BF16)
    gate_ref[...] = qt[D_QB:D_QB + 2 * HA, :]
    vt = v.T.astype(BF16)
    tk = vt_ref.shape[-1]
    for h in range(HB):
        for c in range(vt_ref.shape[1]):
            vt_ref[h, c, 0:DV, :] = vt[h * DV:(h + 1) * DV, c * tk:(c + 1) * tk]
            vt_ref[h, c, DV:DV + ONES_ROWS, :] = jnp.ones((ONES_ROWS, tk), BF16)


def _proj(x2d, g_pre, w_in, b_in):
    rows = x2d.shape[0]
    tm = min(rows, ROW_TILE)
    assert rows % tm == 0
    nblk = rows // tm
    tk = min(tm, KEY_TILE)
    o_qk, o_va, o_oa, o_i, o_q, o_k, o_v, o_ga = 0, 2 * D_A, 3 * D_A, 4 * D_A, 4 * D_A + 2 * HA, \
        4 * D_A + 2 * HA + D_QB, 4 * D_A + 2 * HA + 2 * D_QB, 4 * D_A + 2 * HA + 2 * D_QB + D_VB
    del o_oa, o_ga

    def cols(lo, n):
        return w_in[:, lo:lo + n].astype(BF16), b_in[lo:lo + n].reshape(1, n)

    wqk, bqk = cols(o_qk, 2 * D_A)
    wva, bva = cols(o_va, D_A)
    wk, bk = cols(o_k, D_QB)
    wv, bv = cols(o_v, D_VB)
    t_cols = jnp.concatenate([w_in[:, o_q:o_q + D_QB], w_in[:, o_i:o_i + 2 * HA],
                              jnp.zeros((D_MODEL, GATE_ROWS - 2 * HA), F32)], axis=1)
    t_bias = jnp.concatenate([b_in[o_q:o_q + D_QB], b_in[o_i:o_i + 2 * HA], jnp.zeros((GATE_ROWS - 2 * HA,), F32)])
    wqt = t_cols.T.astype(BF16)
    bqt = t_bias.reshape(D_QB + GATE_ROWS, 1)
    consts = (g_pre.reshape(1, D_MODEL), wqk, bqk, wva, bva, wk, bk, wv, bv, wqt, bqt)

    def row_spec(n):
        return pl.BlockSpec((tm, n), lambda i: (i, 0))

    return pl.pallas_call(
        _proj_kernel,
        grid=(nblk,),
        in_specs=[row_spec(D_MODEL)] + [_const_spec(c.shape) for c in consts],
        out_specs=[row_spec(2 * D_A), row_spec(D_A), pl.BlockSpec((2 * HA, tm), lambda i: (0, i)),
                   pl.BlockSpec((tm, HB, 2 * DK), lambda i: (i, 0, 0)),
                   pl.BlockSpec((HB, tm, 2 * DK), lambda i: (0, i, 0)),
                   pl.BlockSpec((tm, HB, DV), lambda i: (i, 0, 0)),
                   pl.BlockSpec((None, D_QB, tm), lambda i: (i, 0, 0)),
                   pl.BlockSpec((HB, tm // tk, DV + ONES_ROWS, tk), lambda i: (0, i, 0, 0))],
        out_shape=[jax.ShapeDtypeStruct((rows, 2 * D_A), F32), jax.ShapeDtypeStruct((rows, D_A), F32),
                   jax.ShapeDtypeStruct((2 * HA, rows), F32), jax.ShapeDtypeStruct((rows, HB, 2 * DK), F32),
                   jax.ShapeDtypeStruct((HB, rows, 2 * DK), BF16), jax.ShapeDtypeStruct((rows, HB, DV), F32),
                   jax.ShapeDtypeStruct((nblk, D_QB, tm), BF16),
                   jax.ShapeDtypeStruct((HB, rows // tk, DV + ONES_ROWS, tk), BF16)],
        compiler_params=pltpu.CompilerParams(dimension_semantics=("parallel",),
                                             vmem_limit_bytes=VMEM_LIMIT_BYTES),
        name="proj",
    )(x2d, *consts)


def _mlstm_kernel(qk_ref, va_ref, gcol_ref, grow_ref, conv0_ref, c0_ref, n0_ref, m0_ref, cw_ref, cb_ref,
                  h_ref, c_ref, n_ref, m_ref, xp_sc, q_sc, k_sc, st_sc, m_sc, *, chunk, n_chunks):
    j = pl.program_id(1)
    tb = chunk * n_chunks

    @pl.when(j == 0)
    def _():
        xp_sc[0:8, :] = conv0_ref[...]
        for h in range(HA):
            st_sc[h, :, 0:DH] = c0_ref[h].T
            st_sc[h, :, DH:2 * DH] = jnp.broadcast_to(n0_ref[h:h + 1, :], (DH, DH)).T
        m_sc[...] = m0_ref[...]

    @pl.when(j > 0)
    def _():
        xp_sc[0:8, :] = xp_sc[tb:tb + 8, :]

    xp_sc[8:8 + tb, :] = qk_ref[...]
    cw = cw_ref[...]
    conv = cb_ref[...] + xp_sc[8:8 + tb, :] * cw[CONV_W - 1:CONV_W, :]
    for t in range(CONV_W - 1):
        conv = conv + xp_sc[8 - (CONV_W - 1) + t:8 - (CONV_W - 1) + t + tb, :] * cw[t:t + 1, :]
    half = 0.5 * conv
    act = half + half * jnp.tanh(half)
    q_sc[...] = act[:, :D_A].astype(BF16)
    k_sc[...] = (act[:, D_A:] * DH ** -0.5).astype(BF16)

    rr = lax.broadcasted_iota(jnp.int32, (chunk, chunk), 0)
    cc = lax.broadcasted_iota(jnp.int32, (chunk, chunk), 1)
    tril = cc <= rr
    ltri = tril.astype(BF16)
    utri = (rr <= cc).astype(BF16)
    ones = jnp.ones((chunk, DH), F32)

    for c in range(n_chunks):
        sl = slice(c * chunk, (c + 1) * chunk)
        gc = gcol_ref[sl, :]
        ig_c = gc[:, 0:HA]
        b_c = sum(_mm(ltri, p) for p in _split3(_log_sigmoid(gc[:, HA:2 * HA])))
        g = b_c[chunk - 1:chunk, :]
        ls_c = g - b_c + ig_c
        m_old = m_sc[0:1, 0:HA]
        m_new = jnp.maximum(g + m_old, jnp.max(ls_c, axis=0, keepdims=True))
        ws_c = jnp.exp(ls_c - m_new)
        decay = jnp.exp(g + m_old - m_new)
        a_c = b_c + m_old
        run = ig_c - b_c
        shift = 1
        while shift < chunk:
            run = jnp.maximum(run, jnp.concatenate([jnp.full((shift, HA), NEG, F32), run[:chunk - shift, :]], axis=0))
            shift *= 2
        mt_c = jnp.maximum(a_c, b_c + run)
        m_sc[0:1, 0:HA] = m_new
        gr = grow_ref[c]
        b_r = sum(_mm(p, utri) for p in _split3(_log_sigmoid(gr[HA:2 * HA, :])))
        c_r = gr[0:HA, :] - b_r
        for h in range(HA):
            hs = slice(h * DH, (h + 1) * DH)
            dlog = jnp.where(tril, b_c[:, h:h + 1] + c_r[h:h + 1, :], NEG)
            a_h = a_c[:, h:h + 1]
            m_t = mt_c[:, h:h + 1]
            p = jnp.exp(dlog - m_t)
            w_int = jnp.exp(a_h - m_t)
            q_h = q_sc[sl, hs]
            k_h = k_sc[sl, hs]
            vaug = jnp.concatenate([va_ref[sl, hs], ones], axis=1)
            s = (p * lax.dot_general(q_h, k_h, NT_DIMS, preferred_element_type=F32)).astype(BF16)
            st = st_sc[h]
            acc = w_int * _mm(q_h, st.astype(BF16)) + _mm(s, vaug.astype(BF16))
            hh = acc[:, :DH] / jnp.maximum(jnp.abs(acc[:, DH:]), jnp.exp(-m_t))
            h_ref[sl, hs] = hh
            wv = (ws_c[:, h:h + 1] * vaug).astype(BF16)
            st_sc[h] = decay[:, h:h + 1] * st + lax.dot_general(k_h, wv, TN_DIMS, preferred_element_type=F32)

    @pl.when(j == pl.num_programs(1) - 1)
    def _():
        for h in range(HA):
            st = st_sc[h]
            c_ref[h] = st[:, :DH].T
            n_ref[h:h + 1, :] = st[:, DH:].T[0:1, :]
        m_ref[...] = m_sc[...]


def _mlstm(qk_a, v_a, gates_t, conv0, c0, n0, m0, conv_w, conv_b):
    bsz, t, _ = qk_a.shape
    chunk = min(t, MLSTM_CHUNK)
    n_chunks = min(t // chunk, MLSTM_ROW_TILE // chunk)
    tb = chunk * n_chunks
    assert t % tb == 0
    nblk = t // tb
    grow = gates_t.reshape(2 * HA, bsz, t // chunk, chunk).transpose(1, 2, 0, 3)
    gates = gates_t.T.reshape(bsz, t, 2 * HA)
    conv0p = jnp.pad(conv0, ((0, 0), (8 - (CONV_W - 1), 0), (0, 0)))
    m0p = jnp.pad(m0, ((0, 0), (0, 128 - HA)))[:, None, :] * jnp.ones((1, 8, 1), F32)
    kern = functools.partial(_mlstm_kernel, chunk=chunk, n_chunks=n_chunks)
    return pl.pallas_call(
        kern,
        grid=(bsz, nblk),
        in_specs=[pl.BlockSpec((None, tb, 2 * D_A), lambda b, j: (b, j, 0)),
                  pl.BlockSpec((None, tb, D_A), lambda b, j: (b, j, 0)),
                  pl.BlockSpec((None, tb, 2 * HA), lambda b, j: (b, j, 0)),
                  pl.BlockSpec((None, n_chunks, 2 * HA, chunk), lambda b, j: (b, j, 0, 0)),
                  pl.BlockSpec((None, 8, 2 * D_A), lambda b, j: (b, 0, 0)),
                  pl.BlockSpec((None, HA, DH, DH), lambda b, j: (b, 0, 0, 0)),
                  pl.BlockSpec((None, HA, DH), lambda b, j: (b, 0, 0)),
                  pl.BlockSpec((None, 8, 128), lambda b, j: (b, 0, 0)),
                  _const_spec((CONV_W, 2 * D_A)), _const_spec((1, 2 * D_A))],
        out_specs=[pl.BlockSpec((None, tb, D_A), lambda b, j: (b, j, 0)),
                   pl.BlockSpec((None, HA, DH, DH), lambda b, j: (b, 0, 0, 0)),
                   pl.BlockSpec((None, HA, DH), lambda b, j: (b, 0, 0)),
                   pl.BlockSpec((None, 8, 128), lambda b, j: (b, 0, 0))],
        out_shape=[jax.ShapeDtypeStruct((bsz, t, D_A), F32), jax.ShapeDtypeStruct((bsz, HA, DH, DH), F32),
                   jax.ShapeDtypeStruct((bsz, HA, DH), F32), jax.ShapeDtypeStruct((bsz, 8, 128), F32)],
        scratch_shapes=[pltpu.VMEM((tb + 8, 2 * D_A), F32), pltpu.VMEM((tb, D_A), BF16),
                        pltpu.VMEM((tb, D_A), BF16), pltpu.VMEM((HA, DH, 2 * DH), F32),
                        pltpu.VMEM((8, 128), F32)],
        compiler_params=pltpu.CompilerParams(dimension_semantics=("parallel", "arbitrary"),
                                             vmem_limit_bytes=VMEM_LIMIT_BYTES),
        name="mlstm",
    )(qk_a, v_a, gates, grow, conv0p, c0, n0, m0p, conv_w, conv_b.reshape(1, 2 * D_A))


def _attn_kernel(qt_ref, k_ref, vt_ref, bd_ref, bp_ref, ghb_ref, lam_ref, o_ref,
                 qpad_sc, m_sc, acc_sc, s_sc, mx_sc, *, tq, tk, n_qblk):
    i = pl.program_id(1)
    lv = lam_ref[...]
    lam = (jnp.exp(jnp.sum(lv[0:1, :] * lv[1:2, :], axis=1, keepdims=True))
           - jnp.exp(jnp.sum(lv[2:3, :] * lv[3:4, :], axis=1, keepdims=True)) + LAMBDA_INIT)

    first_map = lax.broadcasted_iota(jnp.int32, (2 * DK, tq), 0) < DK
    for h in range(HB):
        qh = qt_ref[h * 2 * DK:(h + 1) * 2 * DK, :]
        zero = jnp.zeros_like(qh)
        qpad_sc[h, :, 0:tq] = jnp.where(first_map, qh, zero)
        qpad_sc[h, :, tq:2 * tq] = jnp.where(first_map, zero, qh)
    m_sc[...] = jnp.full(m_sc.shape, NEG, F32)
    acc_sc[...] = jnp.zeros(acc_sc.shape, F32)

    corner = bp_ref.shape[-1]

    def logits(jb, h):
        row0 = pl.multiple_of(jb * tk, tk)
        return _mm(k_ref[h, pl.ds(row0, tk), :], qpad_sc[h])

    def put(slot, s):
        s_sc[slot] = s
        mx_sc[slot] = jnp.max(s, axis=0, keepdims=True)

    def stage1(jb, h, slot, is_prev):
        put(slot, logits(jb, h))
        if is_prev is not None:
            bias = jnp.where(is_prev, bp_ref[h], 0.0)
            for c0 in (0, tq):
                sc = s_sc[slot, tk - corner:tk, c0:c0 + corner] + bias
                s_sc[slot, tk - corner:tk, c0:c0 + corner] = sc
                mx_sc[slot, :, c0:c0 + corner] = jnp.maximum(mx_sc[slot, :, c0:c0 + corner],
                                                             jnp.max(sc, axis=0, keepdims=True))

    def diag_bias(h):
        return jnp.concatenate([bd_ref[h], bd_ref[h]], axis=1)

    def stage2(jb, h, slot):
        m_old = m_sc[h]
        m_new = jnp.maximum(m_old, mx_sc[slot])
        p = jnp.exp2(s_sc[slot] - m_new).astype(BF16)
        acc_sc[h] = jnp.exp2(m_old - m_new) * acc_sc[h] + _mm(vt_ref[h, jb], p)
        m_sc[h] = m_new

    if n_qblk > 1:
        stage1(0, 0, 0, i == 1)

        def one_block(jb):
            is_prev = jb == i - 1
            stage1(jb, 1, 1, is_prev)
            stage2(jb, 0, 0)
            stage1(jb, 2, 0, is_prev)
            stage2(jb, 1, 1)
            stage1(jb, 3, 1, is_prev)
            stage2(jb, 2, 0)
            stage1(jb + 1, 0, 0, jb == i - 2)
            stage2(jb, 3, 1)

        def four_blocks(jj, carry):
            for d in range(4):
                one_block(4 * jj + d)
            return carry
        lax.fori_loop(0, i // 4, four_blocks, 0)

        def rest(jb, carry):
            one_block(jb)
            return carry
        lax.fori_loop(4 * (i // 4), i, rest, 0)
    else:
        stage1(0, 0, 0, None)

    put(0, s_sc[0] + diag_bias(0))
    put(1, logits(i, 1) + diag_bias(1))
    stage2(i, 0, 0)
    put(0, logits(i, 2) + diag_bias(2))
    stage2(i, 1, 1)
    put(1, logits(i, 3) + diag_bias(3))
    stage2(i, 2, 0)
    stage2(i, 3, 1)

    for h in range(HB):
        on = acc_sc[h, 0:DV, :] / acc_sc[h, DV:DV + 1, :]
        o = on[:, 0:tq] - lam * on[:, tq:2 * tq]
        o = o * lax.rsqrt(jnp.mean(o * o, axis=0, keepdims=True) + EPS) * (ghb_ref[...] * (1.0 - LAMBDA_INIT))
        o_ref[:, h * DV:(h + 1) * DV] = o.T.astype(o_ref.dtype)


def _bucket(rel):
    half, max_exact = NUM_BUCKETS // 2, NUM_BUCKETS // 4
    n = np.abs(rel).astype(np.int64)
    sq = np.maximum(n * n // (max_exact * max_exact), 1)
    large = max_exact + np.floor(np.log2(sq.astype(np.float64)) + 1e-9).astype(np.int64)
    large = np.minimum(large, half - 1)
    return np.where(rel > 0, half, 0) + np.where(n < max_exact, n, large)


def _bias_tiles(rel_bias, rel0, nk, tq):
    table = (rel_bias - rel_bias[NUM_BUCKETS // 2 - 1:NUM_BUCKETS // 2, :]) * LOG2E
    half, max_exact = NUM_BUCKETS // 2, NUM_BUCKETS // 4
    dist = np.arange(4 * MAX_DISTANCE)
    buckets = _bucket(-dist)
    rel = rel0 + lax.broadcasted_iota(jnp.int32, (nk, tq), 0) - lax.broadcasted_iota(jnp.int32, (nk, tq), 1)
    n = jnp.abs(rel)
    large = jnp.full((nk, tq), max_exact, jnp.int32)
    for b in range(max_exact + 1, half):
        large = large + (n >= int(dist[buckets >= b][0])).astype(jnp.int32)
    bucket = jnp.where(rel > 0, half, 0) + jnp.where(n < max_exact, n, large)
    tile = jnp.zeros((HB, nk, tq), F32)
    for b in range(NUM_BUCKETS):
        tile = jnp.where((bucket == b)[None], table[b][:, None, None], tile)
    return tile


def _attn(qt_blk, k16, vt_blk, bias_diag, bias_prev, g_head_b, lam_vecs, bsz, t, tq, tk):
    n_qblk = t // tq
    nk = k16.shape[2]
    n_kblk = nk // tk
    assert t % tq == 0 and nk % tk == 0 and (n_qblk == 1 or tq == tk)
    kern = functools.partial(_attn_kernel, tq=tq, tk=tk, n_qblk=n_qblk)
    return pl.pallas_call(
        kern,
        grid=(bsz, n_qblk),
        in_specs=[pl.BlockSpec((None, D_QB, tq), lambda b, i: (b * n_qblk + i, 0, 0)),
                  pl.BlockSpec((HB, None, nk, 2 * DK), lambda b, i: (0, b, 0, 0), pipeline_mode=pl.Buffered(1)),
                  pl.BlockSpec((HB, n_kblk, DV + ONES_ROWS, tk), lambda b, i: (0, b, 0, 0),
                               pipeline_mode=pl.Buffered(1)),
                  _const_spec(bias_diag.shape), _const_spec(bias_prev.shape),
                  _const_spec((DV, 1)), _const_spec((4, DK))],
        out_specs=pl.BlockSpec((None, tq, D_VB), lambda b, i: (b, i, 0)),
        out_shape=jax.ShapeDtypeStruct((bsz, t, D_VB), BF16),
        scratch_shapes=[pltpu.VMEM((HB, 2 * DK, 2 * tq), BF16), pltpu.VMEM((HB, 1, 2 * tq), F32),
                        pltpu.VMEM((HB, DV + ONES_ROWS, 2 * tq), F32),
                        pltpu.VMEM((2, tk, 2 * tq), F32), pltpu.VMEM((2, 1, 2 * tq), F32)],
        compiler_params=pltpu.CompilerParams(dimension_semantics=("parallel", "arbitrary"),
                                             vmem_limit_bytes=VMEM_LIMIT_BYTES),
        name="attn",
    )(qt_blk, k16, vt_blk, bias_diag, bias_prev, g_head_b.reshape(DV, 1), lam_vecs)


def _attn_step_kernel(qt_ref, kp_ref, vp_ref, kn_ref, vn_ref, bpast_ref, bnew_ref, ghb_ref, lam_ref, o_ref):
    tq = qt_ref.shape[-1]
    lv = lam_ref[...]
    lam = (jnp.exp(jnp.sum(lv[0:1, :] * lv[1:2, :], axis=1, keepdims=True))
           - jnp.exp(jnp.sum(lv[2:3, :] * lv[3:4, :], axis=1, keepdims=True)) + LAMBDA_INIT)
    first_map = lax.broadcasted_iota(jnp.int32, (2 * DK, tq), 0) < DK
    kp = kp_ref[...].reshape(kp_ref.shape[0], D_QB).astype(BF16)
    vp = vp_ref[...].reshape(vp_ref.shape[0], D_VB).astype(BF16)
    vn = vn_ref[...].reshape(vn_ref.shape[0], D_VB).astype(BF16)
    for h in range(HB):
        qh = qt_ref[h * 2 * DK:(h + 1) * 2 * DK, :]
        zero = jnp.zeros_like(qh)
        qpad = jnp.concatenate([jnp.where(first_map, qh, zero), jnp.where(first_map, zero, qh)], axis=1)
        s_p = _mm(kp[:, h * 2 * DK:(h + 1) * 2 * DK], qpad) + jnp.concatenate([bpast_ref[h], bpast_ref[h]], axis=1)
        s_n = _mm(kn_ref[h], qpad) + jnp.concatenate([bnew_ref[h], bnew_ref[h]], axis=1)
        m = jnp.maximum(jnp.max(s_p, axis=0, keepdims=True), jnp.max(s_n, axis=0, keepdims=True))
        p_p = jnp.exp2(s_p - m)
        p_n = jnp.exp2(s_n - m)
        den = jnp.sum(p_p, axis=0, keepdims=True) + jnp.sum(p_n, axis=0, keepdims=True)
        num = (lax.dot_general(vp[:, h * DV:(h + 1) * DV], p_p.astype(BF16), TN_DIMS, preferred_element_type=F32)
               + lax.dot_general(vn[:, h * DV:(h + 1) * DV], p_n.astype(BF16), TN_DIMS,
                                 preferred_element_type=F32))
        on = num / den
        o = on[:, 0:tq] - lam * on[:, tq:2 * tq]
        o = o * lax.rsqrt(jnp.mean(o * o, axis=0, keepdims=True) + EPS) * (ghb_ref[...] * (1.0 - LAMBDA_INIT))
        o_ref[:, h * DV:(h + 1) * DV] = o.T.astype(o_ref.dtype)


def _attn_step(qt, k_past, v_past, k_new16, v_new, bias_past, bias_new, g_head_b, lam_vecs):
    bsz, _, t = qt.shape
    past = k_past.shape[1]
    return pl.pallas_call(
        _attn_step_kernel,
        grid=(bsz,),
        in_specs=[pl.BlockSpec((None, D_QB, t), lambda b: (b, 0, 0)),
                  pl.BlockSpec((None, past, HB, 2 * DK), lambda b: (b, 0, 0, 0)),
                  pl.BlockSpec((None, past, HB, DV), lambda b: (b, 0, 0, 0)),
                  pl.BlockSpec((HB, None, t, 2 * DK), lambda b: (0, b, 0, 0)),
                  pl.BlockSpec((None, t, HB, DV), lambda b: (b, 0, 0, 0)),
                  _const_spec(bias_past.shape), _const_spec(bias_new.shape),
                  _const_spec((DV, 1)), _const_spec((4, DK))],
        out_specs=pl.BlockSpec((None, t, D_VB), lambda b: (b, 0, 0)),
        out_shape=jax.ShapeDtypeStruct((bsz, t, D_VB), BF16),
        compiler_params=pltpu.CompilerParams(dimension_semantics=("parallel",),
                                             vmem_limit_bytes=VMEM_LIMIT_BYTES),
        name="attn_step",
    )(qt, k_past, v_past, k_new16, v_new, bias_past, bias_new, g_head_b.reshape(DV, 1), lam_vecs)


def _out_kernel(x_ref, h_ref, ob_ref, gha_ref, gpre_ref, gpost_ref, gpref_ref, gpostf_ref, wg_ref, bg_ref,
                wpa_ref, wpb_ref, wout_ref, wff1_ref, wff2_ref, o_ref):
    x = x_ref[...]
    u = _rms(x, gpre_ref[...]).astype(BF16)
    hn = jnp.concatenate([_rms(h_ref[:, h * DH:(h + 1) * DH], gha_ref[h:h + 1, :]) for h in range(HA)], axis=1)
    gates = jax.nn.sigmoid(_mm(u, wg_ref[...]) + bg_ref[...])
    h_a = (hn * gates[:, 0:D_A]).astype(BF16)
    y_a = _mm(h_a, wpa_ref[...])
    y_b = _mm(ob_ref[...], wpb_ref[...])
    gate_a = gates[:, D_A:D_A + D_MODEL]
    gate_b = gates[:, D_A + D_MODEL:]
    mix = _mm((gate_a * y_a + gate_b * y_b).astype(BF16), wout_ref[...])
    x1 = x + _rms(mix, gpost_ref[...])
    f = _rms(x1, gpref_ref[...]).astype(BF16)
    hf = jnp.zeros_like(x1)
    for c in range(FF // FF_CHUNK):
        mid = jnp.maximum(_mm(f, wff1_ref[:, c * FF_CHUNK:(c + 1) * FF_CHUNK]), 0.0)
        hf = hf + _mm((mid * mid).astype(BF16), wff2_ref[c * FF_CHUNK:(c + 1) * FF_CHUNK, :])
    o_ref[...] = x1 + _rms(hf, gpostf_ref[...])


def _out(x2d, h2d, ob2d, g_head_a, g_pre_mix, g_post_mix, g_pre_ffn, g_post_ffn, w_in, b_in, w_pa, w_pb, w_out,
         w_ff1, w_ff2):
    rows = x2d.shape[0]
    tm = min(rows, OUT_ROW_TILE)
    o_oa = 3 * D_A
    o_ga = 4 * D_A + 2 * HA + 2 * D_QB + D_VB

    def cols(lo, n):
        return w_in[:, lo:lo + n].astype(BF16), b_in[lo:lo + n].reshape(1, n)

    woa, boa = cols(o_oa, D_A)
    wgg, bgg = cols(o_ga, 2 * D_MODEL)
    wg = jnp.concatenate([woa, wgg], axis=1)
    bg = jnp.concatenate([boa, bgg], axis=1)
    consts = (g_head_a, g_pre_mix.reshape(1, -1), g_post_mix.reshape(1, -1), g_pre_ffn.reshape(1, -1),
              g_post_ffn.reshape(1, -1), wg, bg, w_pa.astype(BF16), w_pb.astype(BF16),
              w_out.astype(BF16), w_ff1.astype(BF16), w_ff2.astype(BF16))

    assert rows % tm == 0

    def row_spec(n):
        return pl.BlockSpec((tm, n), lambda i: (i, 0))

    return pl.pallas_call(
        _out_kernel,
        grid=(rows // tm,),
        in_specs=[row_spec(D_MODEL), row_spec(D_A), row_spec(D_VB)] + [_const_spec(c.shape) for c in consts],
        out_specs=row_spec(D_MODEL),
        out_shape=jax.ShapeDtypeStruct((rows, D_MODEL), F32),
        compiler_params=pltpu.CompilerParams(dimension_semantics=("parallel",),
                                             vmem_limit_bytes=VMEM_LIMIT_BYTES),
        name="out",
    )(x2d, h2d, ob2d, *consts)


def _layer(x, k_past, v_past, c0, n0, m0, conv0, rel_bias, g_pre_mix, g_post_mix, g_pre_ffn, g_post_ffn,
           w_in, b_in, conv_w, conv_b, g_head_a, w_pa, lam_vecs, g_head_b, w_pb, w_out, w_ff1, w_ff2):
    bsz, t, _ = x.shape
    past = k_past.shape[1]
    x2d = x.reshape(bsz * t, D_MODEL)
    qk_a, v_a, gates, k_new, k16, v_new, qt_blk, vt_blk = _proj(x2d, g_pre_mix, w_in, b_in)

    h_a, c_new, n_new, m_new = _mlstm(qk_a.reshape(bsz, t, 2 * D_A), v_a.reshape(bsz, t, D_A),
                                      gates, conv0, c0, n0, m0, conv_w, conv_b)

    if past == 0:
        tq = tk = min(t, ROW_TILE, KEY_TILE)
        s_idx = lax.broadcasted_iota(jnp.int32, (tk, tq), 0)
        t_idx = lax.broadcasted_iota(jnp.int32, (tk, tq), 1)
        allowed = (s_idx // CHUNK) <= (t_idx // CHUNK)
        bias_diag = jnp.where(allowed[None], _bias_tiles(rel_bias, 0, tk, tq), NEG)
        corner = min(BIAS_CORNER, tk)
        bias_prev = _bias_tiles(rel_bias, -corner, corner, corner)
        o_b = _attn(qt_blk, k16.reshape(HB, bsz, t, 2 * DK), vt_blk, bias_diag, bias_prev, g_head_b, lam_vecs, bsz, t,
                    tq, tk)
    else:
        nk = past + t
        s_idx = lax.broadcasted_iota(jnp.int32, (nk, t), 0)
        t_idx = lax.broadcasted_iota(jnp.int32, (nk, t), 1)
        allowed = (s_idx // CHUNK) <= ((past + t_idx) // CHUNK)
        bias = jnp.where(allowed[None], _bias_tiles(rel_bias, -past, nk, t), NEG)
        o_b = _attn_step(jnp.transpose(qt_blk.reshape(D_QB, bsz, t), (1, 0, 2)), k_past, v_past,
                         k16.reshape(HB, bsz, t, 2 * DK), v_new.reshape(bsz, t, HB, DV),
                         bias[:, :past], bias[:, past:], g_head_b, lam_vecs)

    y = _out(x2d, h_a.reshape(bsz * t, D_A), o_b.reshape(bsz * t, D_VB), g_head_a, g_pre_mix, g_post_mix, g_pre_ffn,
             g_post_ffn, w_in, b_in, w_pa, w_pb, w_out, w_ff1, w_ff2)
    return (y.reshape(bsz, t, D_MODEL), k_new.reshape(bsz, t, HB, 2 * DK), v_new.reshape(bsz, t, HB, DV),
            c_new, n_new, m_new[:, 0, :HA], qk_a.reshape(bsz, t, 2 * D_A)[:, t - (CONV_W - 1):, :])


def kernel(x_prompt, x_sample, cache_k, cache_v, state_C, state_n, state_m, state_conv, rel_bias, g_pre_mix,
           g_post_mix, g_pre_ffn, g_post_ffn, w_in, b_in, conv_w, conv_b, g_head_a, w_pa, lambda_q1, lambda_k1,
           lambda_q2, lambda_k2, g_head_b, w_pb, w_out, w_ff1, w_ff2):
    lam_vecs = jnp.concatenate([lambda_q1, lambda_k1, lambda_q2, lambda_k2], axis=0)
    weights = (rel_bias, g_pre_mix[0], g_post_mix[0], g_pre_ffn[0], g_post_ffn[0], w_in[0], b_in[0], conv_w[0],
               conv_b[0], g_head_a[0], w_pa[0], lam_vecs, g_head_b[0], w_pb[0], w_out[0], w_ff1[0], w_ff2[0])
    bp = x_prompt.shape[0]
    prompt = _layer(x_prompt, jnp.zeros((bp, 0, HB, 2 * DK), F32), jnp.zeros((bp, 0, HB, DV), F32),
                    jnp.zeros((bp, HA, DH, DH), F32), jnp.zeros((bp, HA, DH), F32), jnp.zeros((bp, HA), F32),
                    jnp.zeros((bp, CONV_W - 1, 2 * D_A), F32), *weights)
    sample = _layer(x_sample, cache_k[0], cache_v[0], state_C[0], state_n[0], state_m[0], state_conv[0], *weights)
    yp, ys = prompt[0], sample[0]
    return (yp, ys) + tuple(a[None] for a in prompt[1:]) + tuple(a[None] for a in sample[1:])
```

```python
import functools
import math

import numpy as np
import jax
import jax.numpy as jnp
from jax import lax
from jax.experimental import pallas as pl
from jax.experimental.pallas import tpu as pltpu

F32 = jnp.float32
BF16 = jnp.bfloat16

D_MODEL = 1024
HA = 4
DH = 128
D_A = HA * DH
HB = 4
DK = 64
DV = 2 * DK
D_QB = HB * 2 * DK
D_VB = HB * DV
CONV_W = 4
FF = 4 * D_MODEL
NUM_BUCKETS = 32
MAX_DISTANCE = 128
CHUNK = 64
MLSTM_CHUNK = 128
EPS = 1e-6
LAMBDA_INIT = 0.8 - 0.6 * math.exp(-0.3 * 0)
NEG = -1e30
LOG2E = math.log2(math.e)
QSCALE = DK ** -0.5 * LOG2E
GATE_ROWS = 16
ROW_TILE = 512
KEY_TILE = 512
ONES_ROWS = 16
BIAS_CORNER = 128
OUT_ROW_TILE = 512
MLSTM_ROW_TILE = 1024
FF_CHUNK = 1024
VMEM_LIMIT_BYTES = 56 * 1024 * 1024

NT_DIMS = (((1,), (1,)), ((), ()))
TN_DIMS = (((0,), (0,)), ((), ()))


def _const_spec(shape):
    zeros = (0,) * len(shape)
    return pl.BlockSpec(shape, lambda *_: zeros, pipeline_mode=pl.Buffered(1))


def _rms(x, g):
    return x * lax.rsqrt(jnp.mean(x * x, axis=-1, keepdims=True) + EPS) * g


def _log_sigmoid(x):
    return -(jnp.maximum(-x, 0.0) + jnp.log(1.0 + jnp.exp(-jnp.abs(x))))


def _split3(x):
    x1 = x.astype(BF16)
    r = x - x1.astype(F32)
    x2 = r.astype(BF16)
    r = r - x2.astype(F32)
    return x1, x2, r.astype(BF16)


def _mm(a, b):
    return jnp.dot(a, b, preferred_element_type=F32)


def _proj_kernel(x_ref, g_ref, wqk_ref, bqk_ref, wva_ref, bva_ref, wk_ref, bk_ref,
                 wv_ref, bv_ref, wqt_ref, bqt_ref,
                 qk_ref, va_ref, gate_ref, k_ref, k16_ref, v_ref, qt_ref, vt_ref):
    u = _rms(x_ref[...], g_ref[...]).astype(BF16)
    qk_ref[...] = _mm(u, wqk_ref[...]) + bqk_ref[...]
    va_ref[...] = _mm(u, wva_ref[...]) + bva_ref[...]
    k = _mm(u, wk_ref[...]) + bk_ref[...]
    k_ref[...] = k.reshape(k_ref.shape)
    k16 = k.astype(BF16)
    for h in range(HB):
        k16_ref[h] = k16[:, h * 2 * DK:(h + 1) * 2 * DK]
    v = _mm(u, wv_ref[...]) + bv_ref[...]
    v_ref[...] = v.reshape(v_ref.shape)
    qt = lax.dot_general(wqt_ref[...], u, NT_DIMS, preferred_element_type=F32) + bqt_ref[...]
    qt_ref[...] = (qt[0:D_QB, :] * QSCALE).astype(BF16)
    gate_ref[...] = qt[D_QB:D_QB + 2 * HA, :]
    vt = v.T.astype(BF16)
    tk = vt_ref.shape[-1]
    for h in range(HB):
        for c in range(vt_ref.shape[1]):
            vt_ref[h, c, 0:DV, :] = vt[h * DV:(h + 1) * DV, c * tk:(c + 1) * tk]
            vt_ref[h, c, DV:DV + ONES_ROWS, :] = jnp.ones((ONES_ROWS, tk), BF16)


def _proj(x2d, g_pre, w_in, b_in):
    rows = x2d.shape[0]
    tm = min(rows, ROW_TILE)
    assert rows % tm == 0
    nblk = rows // tm
    tk = min(tm, KEY_TILE)
    o_qk, o_va, o_oa, o_i, o_q, o_k, o_v, o_ga = 0, 2 * D_A, 3 * D_A, 4 * D_A, 4 * D_A + 2 * HA, \
        4 * D_A + 2 * HA + D_QB, 4 * D_A + 2 * HA + 2 * D_QB, 4 * D_A + 2 * HA + 2 * D_QB + D_VB
    del o_oa, o_ga

    def cols(lo, n):
        return w_in[:, lo:lo + n].astype(BF16), b_in[lo:lo + n].reshape(1, n)

    wqk, bqk = cols(o_qk, 2 * D_A)
    wva, bva = cols(o_va, D_A)
    wk, bk = cols(o_k, D_QB)
    wv, bv = cols(o_v, D_VB)
    t_cols = jnp.concatenate([w_in[:, o_q:o_q + D_QB], w_in[:, o_i:o_i + 2 * HA],
                              jnp.zeros((D_MODEL, GATE_ROWS - 2 * HA), F32)], axis=1)
    t_bias = jnp.concatenate([b_in[o_q:o_q + D_QB], b_in[o_i:o_i + 2 * HA], jnp.zeros((GATE_ROWS - 2 * HA,), F32)])
    wqt = t_cols.T.astype(BF16)
    bqt = t_bias.reshape(D_QB + GATE_ROWS, 1)
    consts = (g_pre.reshape(1, D_MODEL), wqk, bqk, wva, bva, wk, bk, wv, bv, wqt, bqt)

    def row_spec(n):
        return pl.BlockSpec((tm, n), lambda i: (i, 0))

    return pl.pallas_call(
        _proj_kernel,
        grid=(nblk,),
        in_specs=[row_spec(D_MODEL)] + [_const_spec(c.shape) for c in consts],
        out_specs=[row_spec(2 * D_A), row_spec(D_A), pl.BlockSpec((2 * HA, tm), lambda i: (0, i)),
                   pl.BlockSpec((tm, HB, 2 * DK), lambda i: (i, 0, 0)),
                   pl.BlockSpec((HB, tm, 2 * DK), lambda i: (0, i, 0)),
                   pl.BlockSpec((tm, HB, DV), lambda i: (i, 0, 0)),
                   pl.BlockSpec((None, D_QB, tm), lambda i: (i, 0, 0)),
                   pl.BlockSpec((HB, tm // tk, DV + ONES_ROWS, tk), lambda i: (0, i, 0, 0))],
        out_shape=[jax.ShapeDtypeStruct((rows, 2 * D_A), F32), jax.ShapeDtypeStruct((rows, D_A), F32),
                   jax.ShapeDtypeStruct((2 * HA, rows), F32), jax.ShapeDtypeStruct((rows, HB, 2 * DK), F32),
                   jax.ShapeDtypeStruct((HB, rows, 2 * DK), BF16), jax.ShapeDtypeStruct((rows, HB, DV), F32),
                   jax.ShapeDtypeStruct((nblk, D_QB, tm), BF16),
                   jax.ShapeDtypeStruct((HB, rows // tk, DV + ONES_ROWS, tk), BF16)],
        compiler_params=pltpu.CompilerParams(dimension_semantics=("parallel",),
                                             vmem_limit_bytes=VMEM_LIMIT_BYTES),
        name="proj",
    )(x2d, *consts)


def _mlstm_kernel(qk_ref, va_ref, gcol_ref, grow_ref, conv0_ref, c0_ref, n0_ref, m0_ref, cw_ref, cb_ref,
                  h_ref, c_ref, n_ref, m_ref, xp_sc, q_sc, k_sc, st_sc, m_sc, *, chunk, n_chunks):
    j = pl.program_id(1)
    tb = chunk * n_chunks

    @pl.when(j == 0)
    def _():
        xp_sc[0:8, :] = conv0_ref[...]
        for h in range(HA):
            st_sc[h, :, 0:DH] = c0_ref[h].T
            st_sc[h, :, DH:2 * DH] = jnp.broadcast_to(n0_ref[h:h + 1, :], (DH, DH)).T
        m_sc[...] = m0_ref[...]

    @pl.when(j > 0)
    def _():
        xp_sc[0:8, :] = xp_sc[tb:tb + 8, :]

    xp_sc[8:8 + tb, :] = qk_ref[...]
    cw = cw_ref[...]
    conv = cb_ref[...] + xp_sc[8:8 + tb, :] * cw[CONV_W - 1:CONV_W, :]
    for t in range(CONV_W - 1):
        conv = conv + xp_sc[8 - (CONV_W - 1) + t:8 - (CONV_W - 1) + t + tb, :] * cw[t:t + 1, :]
    half = 0.5 * conv
    act = half + half * jnp.tanh(half)
    q_sc[...] = act[:, :D_A].astype(BF16)
    k_sc[...] = (act[:, D_A:] * DH ** -0.5).astype(BF16)

    rr = lax.broadcasted_iota(jnp.int32, (chunk, chunk), 0)
    cc = lax.broadcasted_iota(jnp.int32, (chunk, chunk), 1)
    tril = cc <= rr
    ltri = tril.astype(BF16)
    utri = (rr <= cc).astype(BF16)
    ones = jnp.ones((chunk, DH), F32)

    for c in range(n_chunks):
        sl = slice(c * chunk, (c + 1) * chunk)
        gc = gcol_ref[sl, :]
        ig_c = gc[:, 0:HA]
        b_c = sum(_mm(ltri, p) for p in _split3(_log_sigmoid(gc[:, HA:2 * HA])))
        g = b_c[chunk - 1:chunk, :]
        ls_c = g - b_c + ig_c
        m_old = m_sc[0:1, 0:HA]
        m_new = jnp.maximum(g + m_old, jnp.max(ls_c, axis=0, keepdims=True))
        ws_c = jnp.exp(ls_c - m_new)
        decay = jnp.exp(g + m_old - m_new)
        a_c = b_c + m_old
        run = ig_c - b_c
        shift = 1
        while shift < chunk:
            run = jnp.maximum(run, jnp.concatenate([jnp.full((shift, HA), NEG, F32), run[:chunk - shift, :]], axis=0))
            shift *= 2
        mt_c = jnp.maximum(a_c, b_c + run)
        m_sc[0:1, 0:HA] = m_new
        gr = grow_ref[c]
        b_r = sum(_mm(p, utri) for p in _split3(_log_sigmoid(gr[HA:2 * HA, :])))
        c_r = gr[0:HA, :] - b_r
        for h in range(HA):
            hs = slice(h * DH, (h + 1) * DH)
            dlog = jnp.where(tril, b_c[:, h:h + 1] + c_r[h:h + 1, :], NEG)
            a_h = a_c[:, h:h + 1]
            m_t = mt_c[:, h:h + 1]
            p = jnp.exp(dlog - m_t)
            w_int = jnp.exp(a_h - m_t)
            q_h = q_sc[sl, hs]
            k_h = k_sc[sl, hs]
            vaug = jnp.concatenate([va_ref[sl, hs], ones], axis=1)
            s = (p * lax.dot_general(q_h, k_h, NT_DIMS, preferred_element_type=F32)).astype(BF16)
            st = st_sc[h]
            acc = w_int * _mm(q_h, st.astype(BF16)) + _mm(s, vaug.astype(BF16))
            hh = acc[:, :DH] / jnp.maximum(jnp.abs(acc[:, DH:]), jnp.exp(-m_t))
            h_ref[sl, hs] = hh
            wv = (ws_c[:, h:h + 1] * vaug).astype(BF16)
            st_sc[h] = decay[:, h:h + 1] * st + lax.dot_general(k_h, wv, TN_DIMS, preferred_element_type=F32)

    @pl.when(j == pl.num_programs(1) - 1)
    def _():
        for h in range(HA):
            st = st_sc[h]
            c_ref[h] = st[:, :DH].T
            n_ref[h:h + 1, :] = st[:, DH:].T[0:1, :]
        m_ref[...] = m_sc[...]


def _mlstm(qk_a, v_a, gates_t, conv0, c0, n0, m0, conv_w, conv_b):
    bsz, t, _ = qk_a.shape
    chunk = min(t, MLSTM_CHUNK)
    n_chunks = min(t // chunk, MLSTM_ROW_TILE // chunk)
    tb = chunk * n_chunks
    assert t % tb == 0
    nblk = t // tb
    grow = gates_t.reshape(2 * HA, bsz, t // chunk, chunk).transpose(1, 2, 0, 3)
    gates = gates_t.T.reshape(bsz, t, 2 * HA)
    conv0p = jnp.pad(conv0, ((0, 0), (8 - (CONV_W - 1), 0), (0, 0)))
    m0p = jnp.pad(m0, ((0, 0), (0, 128 - HA)))[:, None, :] * jnp.ones((1, 8, 1), F32)
    kern = functools.partial(_mlstm_kernel, chunk=chunk, n_chunks=n_chunks)
    return pl.pallas_call(
        kern,
        grid=(bsz, nblk),
        in_specs=[pl.BlockSpec((None, tb, 2 * D_A), lambda b, j: (b, j, 0)),
                  pl.BlockSpec((None, tb, D_A), lambda b, j: (b, j, 0)),
                  pl.BlockSpec((None, tb, 2 * HA), lambda b, j: (b, j, 0)),
                  pl.BlockSpec((None, n_chunks, 2 * HA, chunk), lambda b, j: (b, j, 0, 0)),
                  pl.BlockSpec((None, 8, 2 * D_A), lambda b, j: (b, 0, 0)),
                  pl.BlockSpec((None, HA, DH, DH), lambda b, j: (b, 0, 0, 0)),
                  pl.BlockSpec((None, HA, DH), lambda b, j: (b, 0, 0)),
                  pl.BlockSpec((None, 8, 128), lambda b, j: (b, 0, 0)),
                  _const_spec((CONV_W, 2 * D_A)), _const_spec((1, 2 * D_A))],
        out_specs=[pl.BlockSpec((None, tb, D_A), lambda b, j: (b, j, 0)),
                   pl.BlockSpec((None, HA, DH, DH), lambda b, j: (b, 0, 0, 0)),
                   pl.BlockSpec((None, HA, DH), lambda b, j: (b, 0, 0)),
                   pl.BlockSpec((None, 8, 128), lambda b, j: (b, 0, 0))],
        out_shape=[jax.ShapeDtypeStruct((bsz, t, D_A), F32), jax.ShapeDtypeStruct((bsz, HA, DH, DH), F32),
                   jax.ShapeDtypeStruct((bsz, HA, DH), F32), jax.ShapeDtypeStruct((bsz, 8, 128), F32)],
        scratch_shapes=[pltpu.VMEM((tb + 8, 2 * D_A), F32), pltpu.VMEM((tb, D_A), BF16),
                        pltpu.VMEM((tb, D_A), BF16), pltpu.VMEM((HA, DH, 2 * DH), F32),
                        pltpu.VMEM((8, 128), F32)],
        compiler_params=pltpu.CompilerParams(dimension_semantics=("parallel", "arbitrary"),
                                             vmem_limit_bytes=VMEM_LIMIT_BYTES),
        name="mlstm",
    )(qk_a, v_a, gates, grow, conv0p, c0, n0, m0p, conv_w, conv_b.reshape(1, 2 * D_A))


def _attn_kernel(qt_ref, k_ref, vt_ref, bd_ref, bp_ref, ghb_ref, lam_ref, o_ref,
                 qpad_sc, m_sc, acc_sc, s_sc, mx_sc, *, tq, tk, n_qblk):
    i = pl.program_id(1)
    lv = lam_ref[...]
    lam = (jnp.exp(jnp.sum(lv[0:1, :] * lv[1:2, :], axis=1, keepdims=True))
           - jnp.exp(jnp.sum(lv[2:3, :] * lv[3:4, :], axis=1, keepdims=True)) + LAMBDA_INIT)

    first_map = lax.broadcasted_iota(jnp.int32, (2 * DK, tq), 0) < DK
    for h in range(HB):
        qh = qt_ref[h * 2 * DK:(h + 1) * 2 * DK, :]
        zero = jnp.zeros_like(qh)
        qpad_sc[h, :, 0:tq] = jnp.where(first_map, qh, zero)
        qpad_sc[h, :, tq:2 * tq] = jnp.where(first_map, zero, qh)
    m_sc[...] = jnp.full(m_sc.shape, NEG, F32)
    acc_sc[...] = jnp.zeros(acc_sc.shape, F32)

    corner = bp_ref.shape[-1]

    def logits(jb, h):
        row0 = pl.multiple_of(jb * tk, tk)
        return _mm(k_ref[h, pl.ds(row0, tk), :], qpad_sc[h])

    def put(slot, s):
        s_sc[slot] = s
        mx_sc[slot] = jnp.max(s, axis=0, keepdims=True)

    def stage1(jb, h, slot, is_prev):
        s = logits(jb, h)
        if is_prev is None:
            put(slot, s)
            return
        rest = tk - corner
        s_sc[slot] = s
        top = (jnp.max(s[0:rest, :], axis=0, keepdims=True) if rest > 0
               else jnp.full((1, 2 * tq), NEG, F32))
        mx_sc[slot] = jnp.maximum(top, jnp.max(s[rest:tk, :], axis=0, keepdims=True))
        bias = jnp.where(is_prev, bp_ref[h], 0.0)
        for c0 in (0, tq):
            sc = s_sc[slot, rest:tk, c0:c0 + corner] + bias
            s_sc[slot, rest:tk, c0:c0 + corner] = sc
            mx_sc[slot, :, c0:c0 + corner] = jnp.maximum(top[:, c0:c0 + corner], jnp.max(sc, axis=0, keepdims=True))

    def diag_bias(h):
        return jnp.concatenate([bd_ref[h], bd_ref[h]], axis=1)

    def stage2(jb, h, slot):
        m_old = m_sc[h]
        m_new = jnp.maximum(m_old, mx_sc[slot])
        p = jnp.exp2(s_sc[slot] - m_new).astype(BF16)
        acc_sc[h] = jnp.exp2(m_old - m_new) * acc_sc[h] + _mm(vt_ref[h, jb], p)
        m_sc[h] = m_new

    if n_qblk > 1:
        stage1(0, 0, 0, i == 1)

        def one_block(jb):
            is_prev = jb == i - 1
            stage1(jb, 1, 1, is_prev)
            stage2(jb, 0, 0)
            stage1(jb, 2, 0, is_prev)
            stage2(jb, 1, 1)
            stage1(jb, 3, 1, is_prev)
            stage2(jb, 2, 0)
            stage1(jb + 1, 0, 0, jb == i - 2)
            stage2(jb, 3, 1)

        def four_blocks(jj, carry):
            for d in range(4):
                one_block(4 * jj + d)
            return carry
        lax.fori_loop(0, i // 4, four_blocks, 0)

        def rest(jb, carry):
            one_block(jb)
            return carry
        lax.fori_loop(4 * (i // 4), i, rest, 0)
    else:
        stage1(0, 0, 0, None)

    put(0, s_sc[0] + diag_bias(0))
    put(1, logits(i, 1) + diag_bias(1))
    stage2(i, 0, 0)
    put(0, logits(i, 2) + diag_bias(2))
    stage2(i, 1, 1)
    put(1, logits(i, 3) + diag_bias(3))
    stage2(i, 2, 0)
    stage2(i, 3, 1)

    for h in range(HB):
        on = acc_sc[h, 0:DV, :] / acc_sc[h, DV:DV + 1, :]
        o = on[:, 0:tq] - lam * on[:, tq:2 * tq]
        o = o * lax.rsqrt(jnp.mean(o * o, axis=0, keepdims=True) + EPS) * (ghb_ref[...] * (1.0 - LAMBDA_INIT))
        o_ref[:, h * DV:(h + 1) * DV] = o.T.astype(o_ref.dtype)


def _bucket(rel):
    half, max_exact = NUM_BUCKETS // 2, NUM_BUCKETS // 4
    n = np.abs(rel).astype(np.int64)
    sq = np.maximum(n * n // (max_exact * max_exact), 1)
    large = max_exact + np.floor(np.log2(sq.astype(np.float64)) + 1e-9).astype(np.int64)
    large = np.minimum(large, half - 1)
    return np.where(rel > 0, half, 0) + np.where(n < max_exact, n, large)


def _bias_tiles(rel_bias, rel0, nk, tq):
    table = (rel_bias - rel_bias[NUM_BUCKETS // 2 - 1:NUM_BUCKETS // 2, :]) * LOG2E
    half, max_exact = NUM_BUCKETS // 2, NUM_BUCKETS // 4
    dist = np.arange(4 * MAX_DISTANCE)
    buckets = _bucket(-dist)
    rel = rel0 + lax.broadcasted_iota(jnp.int32, (nk, tq), 0) - lax.broadcasted_iota(jnp.int32, (nk, tq), 1)
    n = jnp.abs(rel)
    large = jnp.full((nk, tq), max_exact, jnp.int32)
    for b in range(max_exact + 1, half):
        large = large + (n >= int(dist[buckets >= b][0])).astype(jnp.int32)
    bucket = jnp.where(rel > 0, half, 0) + jnp.where(n < max_exact, n, large)
    tile = jnp.zeros((HB, nk, tq), F32)
    for b in range(NUM_BUCKETS):
        tile = jnp.where((bucket == b)[None], table[b][:, None, None], tile)
    return tile


def _attn(qt_blk, k16, vt_blk, bias_diag, bias_prev, g_head_b, lam_vecs, bsz, t, tq, tk):
    n_qblk = t // tq
    nk = k16.shape[2]
    n_kblk = nk // tk
    assert t % tq == 0 and nk % tk == 0 and (n_qblk == 1 or tq == tk)
    kern = functools.partial(_attn_kernel, tq=tq, tk=tk, n_qblk=n_qblk)
    return pl.pallas_call(
        kern,
        grid=(bsz, n_qblk),
        in_specs=[pl.BlockSpec((None, D_QB, tq), lambda b, i: (b * n_qblk + i, 0, 0)),
                  pl.BlockSpec((HB, None, nk, 2 * DK), lambda b, i: (0, b, 0, 0), pipeline_mode=pl.Buffered(1)),
                  pl.BlockSpec((HB, n_kblk, DV + ONES_ROWS, tk), lambda b, i: (0, b, 0, 0),
                               pipeline_mode=pl.Buffered(1)),
                  _const_spec(bias_diag.shape), _const_spec(bias_prev.shape),
                  _const_spec((DV, 1)), _const_spec((4, DK))],
        out_specs=pl.BlockSpec((None, tq, D_VB), lambda b, i: (b, i, 0)),
        out_shape=jax.ShapeDtypeStruct((bsz, t, D_VB), BF16),
        scratch_shapes=[pltpu.VMEM((HB, 2 * DK, 2 * tq), BF16), pltpu.VMEM((HB, 1, 2 * tq), F32),
                        pltpu.VMEM((HB, DV + ONES_ROWS, 2 * tq), F32),
                        pltpu.VMEM((2, tk, 2 * tq), F32), pltpu.VMEM((2, 1, 2 * tq), F32)],
        compiler_params=pltpu.CompilerParams(dimension_semantics=("parallel", "arbitrary"),
                                             vmem_limit_bytes=VMEM_LIMIT_BYTES),
        name="attn",
    )(qt_blk, k16, vt_blk, bias_diag, bias_prev, g_head_b.reshape(DV, 1), lam_vecs)


def _attn_step_kernel(qt_ref, kp_ref, vp_ref, kn_ref, vn_ref, bpast_ref, bnew_ref, ghb_ref, lam_ref, o_ref):
    tq = qt_ref.shape[-1]
    lv = lam_ref[...]
    lam = (jnp.exp(jnp.sum(lv[0:1, :] * lv[1:2, :], axis=1, keepdims=True))
           - jnp.exp(jnp.sum(lv[2:3, :] * lv[3:4, :], axis=1, keepdims=True)) + LAMBDA_INIT)
    first_map = lax.broadcasted_iota(jnp.int32, (2 * DK, tq), 0) < DK
    kp = kp_ref[...].reshape(kp_ref.shape[0], D_QB).astype(BF16)
    vp = vp_ref[...].reshape(vp_ref.shape[0], D_VB).astype(BF16)
    vn = vn_ref[...].reshape(vn_ref.shape[0], D_VB).astype(BF16)
    for h in range(HB):
        qh = qt_ref[h * 2 * DK:(h + 1) * 2 * DK, :]
        zero = jnp.zeros_like(qh)
        qpad = jnp.concatenate([jnp.where(first_map, qh, zero), jnp.where(first_map, zero, qh)], axis=1)
        s_p = _mm(kp[:, h * 2 * DK:(h + 1) * 2 * DK], qpad) + jnp.concatenate([bpast_ref[h], bpast_ref[h]], axis=1)
        s_n = _mm(kn_ref[h], qpad) + jnp.concatenate([bnew_ref[h], bnew_ref[h]], axis=1)
        m = jnp.maximum(jnp.max(s_p, axis=0, keepdims=True), jnp.max(s_n, axis=0, keepdims=True))
        p_p = jnp.exp2(s_p - m)
        p_n = jnp.exp2(s_n - m)
        den = jnp.sum(p_p, axis=0, keepdims=True) + jnp.sum(p_n, axis=0, keepdims=True)
        num = (lax.dot_general(vp[:, h * DV:(h + 1) * DV], p_p.astype(BF16), TN_DIMS, preferred_element_type=F32)
               + lax.dot_general(vn[:, h * DV:(h + 1) * DV], p_n.astype(BF16), TN_DIMS,
                                 preferred_element_type=F32))
        on = num / den
        o = on[:, 0:tq] - lam * on[:, tq:2 * tq]
        o = o * lax.rsqrt(jnp.mean(o * o, axis=0, keepdims=True) + EPS) * (ghb_ref[...] * (1.0 - LAMBDA_INIT))
        o_ref[:, h * DV:(h + 1) * DV] = o.T.astype(o_ref.dtype)


def _attn_step(qt, k_past, v_past, k_new16, v_new, bias_past, bias_new, g_head_b, lam_vecs):
    bsz, _, t = qt.shape
    past = k_past.shape[1]
    return pl.pallas_call(
        _attn_step_kernel,
        grid=(bsz,),
        in_specs=[pl.BlockSpec((None, D_QB, t), lambda b: (b, 0, 0)),
                  pl.BlockSpec((None, past, HB, 2 * DK), lambda b: (b, 0, 0, 0)),
                  pl.BlockSpec((None, past, HB, DV), lambda b: (b, 0, 0, 0)),
                  pl.BlockSpec((HB, None, t, 2 * DK), lambda b: (0, b, 0, 0)),
                  pl.BlockSpec((None, t, HB, DV), lambda b: (b, 0, 0, 0)),
                  _const_spec(bias_past.shape), _const_spec(bias_new.shape),
                  _const_spec((DV, 1)), _const_spec((4, DK))],
        out_specs=pl.BlockSpec((None, t, D_VB), lambda b: (b, 0, 0)),
        out_shape=jax.ShapeDtypeStruct((bsz, t, D_VB), BF16),
        compiler_params=pltpu.CompilerParams(dimension_semantics=("parallel",),
                                             vmem_limit_bytes=VMEM_LIMIT_BYTES),
        name="attn_step",
    )(qt, k_past, v_past, k_new16, v_new, bias_past, bias_new, g_head_b.reshape(DV, 1), lam_vecs)


def _out_kernel(x_ref, h_ref, ob_ref, gha_ref, gpre_ref, gpost_ref, gpref_ref, gpostf_ref, wg_ref, bg_ref,
                wpa_ref, wpb_ref, wout_ref, wff1_ref, wff2_ref, o_ref):
    x = x_ref[...]
    u = _rms(x, gpre_ref[...]).astype(BF16)
    hn = jnp.concatenate([_rms(h_ref[:, h * DH:(h + 1) * DH], gha_ref[h:h + 1, :]) for h in range(HA)], axis=1)
    gates = jax.nn.sigmoid(_mm(u, wg_ref[...]) + bg_ref[...])
    h_a = (hn * gates[:, 0:D_A]).astype(BF16)
    y_a = _mm(h_a, wpa_ref[...])
    y_b = _mm(ob_ref[...], wpb_ref[...])
    gate_a = gates[:, D_A:D_A + D_MODEL]
    gate_b = gates[:, D_A + D_MODEL:]
    mix = _mm((gate_a * y_a + gate_b * y_b).astype(BF16), wout_ref[...])
    x1 = x + _rms(mix, gpost_ref[...])
    f = _rms(x1, gpref_ref[...]).astype(BF16)
    hf = jnp.zeros_like(x1)
    for c in range(FF // FF_CHUNK):
        mid = jnp.maximum(_mm(f, wff1_ref[:, c * FF_CHUNK:(c + 1) * FF_CHUNK]), 0.0)
        hf = hf + _mm((mid * mid).astype(BF16), wff2_ref[c * FF_CHUNK:(c + 1) * FF_CHUNK, :])
    o_ref[...] = x1 + _rms(hf, gpostf_ref[...])


def _out(x2d, h2d, ob2d, g_head_a, g_pre_mix, g_post_mix, g_pre_ffn, g_post_ffn, w_in, b_in, w_pa, w_pb, w_out,
         w_ff1, w_ff2):
    rows = x2d.shape[0]
    tm = min(rows, OUT_ROW_TILE)
    o_oa = 3 * D_A
    o_ga = 4 * D_A + 2 * HA + 2 * D_QB + D_VB

    def cols(lo, n):
        return w_in[:, lo:lo + n].astype(BF16), b_in[lo:lo + n].reshape(1, n)

    woa, boa = cols(o_oa, D_A)
    wgg, bgg = cols(o_ga, 2 * D_MODEL)
    wg = jnp.concatenate([woa, wgg], axis=1)
    bg = jnp.concatenate([boa, bgg], axis=1)
    consts = (g_head_a, g_pre_mix.reshape(1, -1), g_post_mix.reshape(1, -1), g_pre_ffn.reshape(1, -1),
              g_post_ffn.reshape(1, -1), wg, bg, w_pa.astype(BF16), w_pb.astype(BF16),
              w_out.astype(BF16), w_ff1.astype(BF16), w_ff2.astype(BF16))

    assert rows % tm == 0

    def row_spec(n):
        return pl.BlockSpec((tm, n), lambda i: (i, 0))

    return pl.pallas_call(
        _out_kernel,
        grid=(rows // tm,),
        in_specs=[row_spec(D_MODEL), row_spec(D_A), row_spec(D_VB)] + [_const_spec(c.shape) for c in consts],
        out_specs=row_spec(D_MODEL),
        out_shape=jax.ShapeDtypeStruct((rows, D_MODEL), F32),
        compiler_params=pltpu.CompilerParams(dimension_semantics=("parallel",),
                                             vmem_limit_bytes=VMEM_LIMIT_BYTES),
        name="out",
    )(x2d, h2d, ob2d, *consts)


def _layer(x, k_past, v_past, c0, n0, m0, conv0, rel_bias, g_pre_mix, g_post_mix, g_pre_ffn, g_post_ffn,
           w_in, b_in, conv_w, conv_b, g_head_a, w_pa, lam_vecs, g_head_b, w_pb, w_out, w_ff1, w_ff2):
    bsz, t, _ = x.shape
    past = k_past.shape[1]
    x2d = x.reshape(bsz * t, D_MODEL)
    qk_a, v_a, gates, k_new, k16, v_new, qt_blk, vt_blk = _proj(x2d, g_pre_mix, w_in, b_in)

    h_a, c_new, n_new, m_new = _mlstm(qk_a.reshape(bsz, t, 2 * D_A), v_a.reshape(bsz, t, D_A),
                                      gates, conv0, c0, n0, m0, conv_w, conv_b)

    if past == 0:
        tq = tk = min(t, ROW_TILE, KEY_TILE)
        s_idx = lax.broadcasted_iota(jnp.int32, (tk, tq), 0)
        t_idx = lax.broadcasted_iota(jnp.int32, (tk, tq), 1)
        allowed = (s_idx // CHUNK) <= (t_idx // CHUNK)
        bias_diag = jnp.where(allowed[None], _bias_tiles(rel_bias, 0, tk, tq), NEG)
        corner = min(BIAS_CORNER, tk)
        bias_prev = _bias_tiles(rel_bias, -corner, corner, corner)
        o_b = _attn(qt_blk, k16.reshape(HB, bsz, t, 2 * DK), vt_blk, bias_diag, bias_prev, g_head_b, lam_vecs, bsz, t,
                    tq, tk)
    else:
        nk = past + t
        s_idx = lax.broadcasted_iota(jnp.int32, (nk, t), 0)
        t_idx = lax.broadcasted_iota(jnp.int32, (nk, t), 1)
        allowed = (s_idx // CHUNK) <= ((past + t_idx) // CHUNK)
        bias = jnp.where(allowed[None], _bias_tiles(rel_bias, -past, nk, t), NEG)
        o_b = _attn_step(jnp.transpose(qt_blk.reshape(D_QB, bsz, t), (1, 0, 2)), k_past, v_past,
                         k16.reshape(HB, bsz, t, 2 * DK), v_new.reshape(bsz, t, HB, DV),
                         bias[:, :past], bias[:, past:], g_head_b, lam_vecs)

    y = _out(x2d, h_a.reshape(bsz * t, D_A), o_b.reshape(bsz * t, D_VB), g_head_a, g_pre_mix, g_post_mix, g_pre_ffn,
             g_post_ffn, w_in, b_in, w_pa, w_pb, w_out, w_ff1, w_ff2)
    return (y.reshape(bsz, t, D_MODEL), k_new.reshape(bsz, t, HB, 2 * DK), v_new.reshape(bsz, t, HB, DV),
            c_new, n_new, m_new[:, 0, :HA], qk_a.reshape(bsz, t, 2 * D_A)[:, t - (CONV_W - 1):, :])


def kernel(x_prompt, x_sample, cache_k, cache_v, state_C, state_n, state_m, state_conv, rel_bias, g_pre_mix,
           g_post_mix, g_pre_ffn, g_post_ffn, w_in, b_in, conv_w, conv_b, g_head_a, w_pa, lambda_q1, lambda_k1,
           lambda_q2, lambda_k2, g_head_b, w_pb, w_out, w_ff1, w_ff2):
    lam_vecs = jnp.concatenate([lambda_q1, lambda_k1, lambda_q2, lambda_k2], axis=0)
    weights = (rel_bias, g_pre_mix[0], g_post_mix[0], g_pre_ffn[0], g_post_ffn[0], w_in[0], b_in[0], conv_w[0],
               conv_b[0], g_head_a[0], w_pa[0], lam_vecs, g_head_b[0], w_pb[0], w_out[0], w_ff1[0], w_ff2[0])
    bp = x_prompt.shape[0]
    prompt = _layer(x_prompt, jnp.zeros((bp, 0, HB, 2 * DK), F32), jnp.zeros((bp, 0, HB, DV), F32),
                    jnp.zeros((bp, HA, DH, DH), F32), jnp.zeros((bp, HA, DH), F32), jnp.zeros((bp, HA), F32),
                    jnp.zeros((bp, CONV_W - 1, 2 * D_A), F32), *weights)
    sample = _layer(x_sample, cache_k[0], cache_v[0], state_C[0], state_n[0], state_m[0], state_conv[0], *weights)
    yp, ys = prompt[0], sample[0]
    return (yp, ys) + tuple(a[None] for a in prompt[1:]) + tuple(a[None] for a in sample[1:])
```

```python
import functools
import math

import numpy as np
import jax
import jax.numpy as jnp
from jax import lax
from jax.experimental import pallas as pl
from jax.experimental.pallas import tpu as pltpu

F32 = jnp.float32
BF16 = jnp.bfloat16

D_MODEL = 1024
HA = 4
DH = 128
D_A = HA * DH
HB = 4
DK = 64
DV = 2 * DK
D_QB = HB * 2 * DK
D_VB = HB * DV
CONV_W = 4
FF = 4 * D_MODEL
NUM_BUCKETS = 32
MAX_DISTANCE = 128
CHUNK = 64
MLSTM_CHUNK = 128
EPS = 1e-6
LAMBDA_INIT = 0.8 - 0.6 * math.exp(-0.3 * 0)
NEG = -1e30
LOG2E = math.log2(math.e)
QSCALE = DK ** -0.5 * LOG2E
GATE_ROWS = 16
ROW_TILE = 512
KEY_TILE = 512
ONES_ROWS = 16
BIAS_CORNER = 128
OUT_ROW_TILE = 512
MLSTM_ROW_TILE = 2048
FF_CHUNK = 1024
VMEM_LIMIT_BYTES = 56 * 1024 * 1024
SUBLANES = 8
LANES = 128

NT_DIMS = (((1,), (1,)), ((), ()))
TN_DIMS = (((0,), (0,)), ((), ()))


def _const_spec(shape):
    zeros = (0,) * len(shape)
    return pl.BlockSpec(shape, lambda *_: zeros, pipeline_mode=pl.Buffered(1))


def _rms(x, g):
    return x * lax.rsqrt(jnp.mean(x * x, axis=-1, keepdims=True) + EPS) * g


def _log_sigmoid(x):
    return -(jnp.maximum(-x, 0.0) + jnp.log(1.0 + jnp.exp(-jnp.abs(x))))


def _split3(x):
    x1 = x.astype(BF16)
    r = x - x1.astype(F32)
    x2 = r.astype(BF16)
    r = r - x2.astype(F32)
    return x1, x2, r.astype(BF16)


def _mm(a, b):
    return jnp.dot(a, b, preferred_element_type=F32)


def _proj_kernel(x_ref, g_ref, wqk_ref, bqk_ref, wva_ref, bva_ref, wk_ref, bk_ref,
                 wv_ref, bv_ref, wqt_ref, bqt_ref,
                 qk_ref, va_ref, gate_ref, k_ref, k16_ref, v_ref, qt_ref, vt_ref):
    u = _rms(x_ref[...], g_ref[...]).astype(BF16)
    qk_ref[...] = _mm(u, wqk_ref[...]) + bqk_ref[...]
    va_ref[...] = _mm(u, wva_ref[...]) + bva_ref[...]
    k = _mm(u, wk_ref[...]) + bk_ref[...]
    k_ref[...] = k.reshape(k_ref.shape)
    k16 = k.astype(BF16)
    for h in range(HB):
        k16_ref[h] = k16[:, h * 2 * DK:(h + 1) * 2 * DK]
    v = _mm(u, wv_ref[...]) + bv_ref[...]
    v_ref[...] = v.reshape(v_ref.shape)
    qt = lax.dot_general(wqt_ref[...], u, NT_DIMS, preferred_element_type=F32) + bqt_ref[...]
    qt_ref[...] = (qt[0:D_QB, :] * QSCALE).astype(BF16)
    gate_ref[...] = qt[D_QB:D_QB + 2 * HA, :]
    vt = v.T.astype(BF16)
    tk = vt_ref.shape[-1]
    for h in range(HB):
        for c in range(vt_ref.shape[1]):
            vt_ref[h, c, 0:DV, :] = vt[h * DV:(h + 1) * DV, c * tk:(c + 1) * tk]
            vt_ref[h, c, DV:DV + ONES_ROWS, :] = jnp.ones((ONES_ROWS, tk), BF16)


def _proj(x2d, g_pre, w_in, b_in):
    rows = x2d.shape[0]
    tm = min(rows, ROW_TILE)
    assert rows % tm == 0
    nblk = rows // tm
    tk = min(tm, KEY_TILE)
    o_qk, o_va, o_oa, o_i, o_q, o_k, o_v, o_ga = 0, 2 * D_A, 3 * D_A, 4 * D_A, 4 * D_A + 2 * HA, \
        4 * D_A + 2 * HA + D_QB, 4 * D_A + 2 * HA + 2 * D_QB, 4 * D_A + 2 * HA + 2 * D_QB + D_VB
    del o_oa, o_ga

    def cols(lo, n):
        return w_in[:, lo:lo + n].astype(BF16), b_in[lo:lo + n].reshape(1, n)

    wqk, bqk = cols(o_qk, 2 * D_A)
    wva, bva = cols(o_va, D_A)
    wk, bk = cols(o_k, D_QB)
    wv, bv = cols(o_v, D_VB)
    t_cols = jnp.concatenate([w_in[:, o_q:o_q + D_QB], w_in[:, o_i:o_i + 2 * HA],
                              jnp.zeros((D_MODEL, GATE_ROWS - 2 * HA), F32)], axis=1)
    t_bias = jnp.concatenate([b_in[o_q:o_q + D_QB], b_in[o_i:o_i + 2 * HA], jnp.zeros((GATE_ROWS - 2 * HA,), F32)])
    wqt = t_cols.T.astype(BF16)
    bqt = t_bias.reshape(D_QB + GATE_ROWS, 1)
    consts = (g_pre.reshape(1, D_MODEL), wqk, bqk, wva, bva, wk, bk, wv, bv, wqt, bqt)

    def row_spec(n):
        return pl.BlockSpec((tm, n), lambda i: (i, 0))

    return pl.pallas_call(
        _proj_kernel,
        grid=(nblk,),
        in_specs=[row_spec(D_MODEL)] + [_const_spec(c.shape) for c in consts],
        out_specs=[row_spec(2 * D_A), row_spec(D_A), pl.BlockSpec((2 * HA, tm), lambda i: (0, i)),
                   pl.BlockSpec((tm, HB, 2 * DK), lambda i: (i, 0, 0)),
                   pl.BlockSpec((HB, tm, 2 * DK), lambda i: (0, i, 0)),
                   pl.BlockSpec((tm, HB, DV), lambda i: (i, 0, 0)),
                   pl.BlockSpec((None, D_QB, tm), lambda i: (i, 0, 0)),
                   pl.BlockSpec((HB, tm // tk, DV + ONES_ROWS, tk), lambda i: (0, i, 0, 0))],
        out_shape=[jax.ShapeDtypeStruct((rows, 2 * D_A), F32), jax.ShapeDtypeStruct((rows, D_A), F32),
                   jax.ShapeDtypeStruct((2 * HA, rows), F32), jax.ShapeDtypeStruct((rows, HB, 2 * DK), F32),
                   jax.ShapeDtypeStruct((HB, rows, 2 * DK), BF16), jax.ShapeDtypeStruct((rows, HB, DV), F32),
                   jax.ShapeDtypeStruct((nblk, D_QB, tm), BF16),
                   jax.ShapeDtypeStruct((HB, rows // tk, DV + ONES_ROWS, tk), BF16)],
        compiler_params=pltpu.CompilerParams(dimension_semantics=("parallel",),
                                             vmem_limit_bytes=VMEM_LIMIT_BYTES),
        name="proj",
    )(x2d, *consts)


def _mlstm_kernel(qk_ref, va_ref, gcol_ref, grow_ref, conv0_ref, c0_ref, n0_ref, m0_ref, cw_ref, cb_ref,
                  h_ref, c_ref, n_ref, m_ref, xp_sc, q_sc, k_sc, st_sc, m_sc, *, chunk, n_chunks):
    j = pl.program_id(1)
    tb = chunk * n_chunks

    @pl.when(j == 0)
    def _():
        xp_sc[0:SUBLANES, :] = conv0_ref[...]
        for h in range(HA):
            st_sc[h, :, 0:DH] = c0_ref[h].T
            st_sc[h, :, DH:2 * DH] = jnp.broadcast_to(n0_ref[h:h + 1, :], (DH, DH)).T
        m_sc[...] = m0_ref[...]

    @pl.when(j > 0)
    def _():
        xp_sc[0:SUBLANES, :] = xp_sc[tb:tb + SUBLANES, :]

    xp_sc[SUBLANES:SUBLANES + tb, :] = qk_ref[...]
    cw = cw_ref[...]
    conv = cb_ref[...] + xp_sc[SUBLANES:SUBLANES + tb, :] * cw[CONV_W - 1:CONV_W, :]
    for t in range(CONV_W - 1):
        lo = SUBLANES - (CONV_W - 1) + t
        conv = conv + xp_sc[lo:lo + tb, :] * cw[t:t + 1, :]
    half = 0.5 * conv
    act = half + half * jnp.tanh(half)
    q_sc[...] = act[:, :D_A].astype(BF16)
    k_sc[...] = (act[:, D_A:] * DH ** -0.5).astype(BF16)

    rr = lax.broadcasted_iota(jnp.int32, (chunk, chunk), 0)
    cc = lax.broadcasted_iota(jnp.int32, (chunk, chunk), 1)
    tril = cc <= rr
    ltri = tril.astype(BF16)
    utri = (rr <= cc).astype(BF16)
    ones = jnp.ones((chunk, DH), F32)

    for c in range(n_chunks):
        sl = slice(c * chunk, (c + 1) * chunk)
        gc = gcol_ref[sl, :]
        ig_c = gc[:, 0:HA]
        b_c = sum(_mm(ltri, p) for p in _split3(_log_sigmoid(gc[:, HA:2 * HA])))
        g = b_c[chunk - 1:chunk, :]
        ls_c = g - b_c + ig_c
        m_old = m_sc[0:1, 0:HA]
        m_new = jnp.maximum(g + m_old, jnp.max(ls_c, axis=0, keepdims=True))
        ws_c = jnp.exp(ls_c - m_new)
        decay = jnp.exp(g + m_old - m_new)
        a_c = b_c + m_old
        run = ig_c - b_c
        shift = 1
        while shift < chunk:
            run = jnp.maximum(run, jnp.concatenate([jnp.full((shift, HA), NEG, F32), run[:chunk - shift, :]], axis=0))
            shift *= 2
        mt_c = jnp.maximum(a_c, b_c + run)
        m_sc[0:1, 0:HA] = m_new
        gr = grow_ref[c]
        b_r = sum(_mm(p, utri) for p in _split3(_log_sigmoid(gr[HA:2 * HA, :])))
        c_r = gr[0:HA, :] - b_r
        for h in range(HA):
            hs = slice(h * DH, (h + 1) * DH)
            dlog = jnp.where(tril, b_c[:, h:h + 1] + c_r[h:h + 1, :], NEG)
            a_h = a_c[:, h:h + 1]
            m_t = mt_c[:, h:h + 1]
            p = jnp.exp(dlog - m_t)
            w_int = jnp.exp(a_h - m_t)
            q_h = q_sc[sl, hs]
            k_h = k_sc[sl, hs]
            vaug = jnp.concatenate([va_ref[sl, hs], ones], axis=1)
            s = (p * lax.dot_general(q_h, k_h, NT_DIMS, preferred_element_type=F32)).astype(BF16)
            st = st_sc[h]
            acc = w_int * _mm(q_h, st.astype(BF16)) + _mm(s, vaug.astype(BF16))
            hh = acc[:, :DH] / jnp.maximum(jnp.abs(acc[:, DH:]), jnp.exp(-m_t))
            h_ref[sl, hs] = hh
            wv = (ws_c[:, h:h + 1] * vaug).astype(BF16)
            st_sc[h] = decay[:, h:h + 1] * st + lax.dot_general(k_h, wv, TN_DIMS, preferred_element_type=F32)

    @pl.when(j == pl.num_programs(1) - 1)
    def _():
        for h in range(HA):
            st = st_sc[h]
            c_ref[h] = st[:, :DH].T
            n_ref[h:h + 1, :] = st[:, DH:].T[0:1, :]
        m_ref[...] = m_sc[...]


def _mlstm(qk_a, v_a, gates_t, conv0, c0, n0, m0, conv_w, conv_b):
    bsz, t, _ = qk_a.shape
    chunk = min(t, MLSTM_CHUNK)
    n_chunks = min(t // chunk, MLSTM_ROW_TILE // chunk)
    tb = chunk * n_chunks
    assert t % tb == 0
    nblk = t // tb
    grow = gates_t.reshape(2 * HA, bsz, t // chunk, chunk).transpose(1, 2, 0, 3)
    gates = gates_t.T.reshape(bsz, t, 2 * HA)
    conv0p = jnp.pad(conv0, ((0, 0), (SUBLANES - (CONV_W - 1), 0), (0, 0)))
    m0p = jnp.pad(m0, ((0, 0), (0, LANES - HA)))[:, None, :] * jnp.ones((1, SUBLANES, 1), F32)
    kern = functools.partial(_mlstm_kernel, chunk=chunk, n_chunks=n_chunks)
    return pl.pallas_call(
        kern,
        grid=(bsz, nblk),
        in_specs=[pl.BlockSpec((None, tb, 2 * D_A), lambda b, j: (b, j, 0)),
                  pl.BlockSpec((None, tb, D_A), lambda b, j: (b, j, 0)),
                  pl.BlockSpec((None, tb, 2 * HA), lambda b, j: (b, j, 0)),
                  pl.BlockSpec((None, n_chunks, 2 * HA, chunk), lambda b, j: (b, j, 0, 0)),
                  pl.BlockSpec((None, SUBLANES, 2 * D_A), lambda b, j: (b, 0, 0)),
                  pl.BlockSpec((None, HA, DH, DH), lambda b, j: (b, 0, 0, 0)),
                  pl.BlockSpec((None, HA, DH), lambda b, j: (b, 0, 0)),
                  pl.BlockSpec((None, SUBLANES, LANES), lambda b, j: (b, 0, 0)),
                  _const_spec((CONV_W, 2 * D_A)), _const_spec((1, 2 * D_A))],
        out_specs=[pl.BlockSpec((None, tb, D_A), lambda b, j: (b, j, 0)),
                   pl.BlockSpec((None, HA, DH, DH), lambda b, j: (b, 0, 0, 0)),
                   pl.BlockSpec((None, HA, DH), lambda b, j: (b, 0, 0)),
                   pl.BlockSpec((None, SUBLANES, LANES), lambda b, j: (b, 0, 0))],
        out_shape=[jax.ShapeDtypeStruct((bsz, t, D_A), F32), jax.ShapeDtypeStruct((bsz, HA, DH, DH), F32),
                   jax.ShapeDtypeStruct((bsz, HA, DH), F32), jax.ShapeDtypeStruct((bsz, SUBLANES, LANES), F32)],
        scratch_shapes=[pltpu.VMEM((tb + SUBLANES, 2 * D_A), F32), pltpu.VMEM((tb, D_A), BF16),
                        pltpu.VMEM((tb, D_A), BF16), pltpu.VMEM((HA, DH, 2 * DH), F32),
                        pltpu.VMEM((SUBLANES, LANES), F32)],
        compiler_params=pltpu.CompilerParams(dimension_semantics=("parallel", "arbitrary"),
                                             vmem_limit_bytes=VMEM_LIMIT_BYTES),
        name="mlstm",
    )(qk_a, v_a, gates, grow, conv0p, c0, n0, m0p, conv_w, conv_b.reshape(1, 2 * D_A))


def _attn_kernel(qt_ref, k_ref, vt_ref, bd_ref, bp_ref, ghb_ref, lam_ref, o_ref,
                 qpad_sc, m_sc, acc_sc, s_sc, mx_sc, *, tq, tk, n_qblk):
    i = pl.program_id(1)
    lv = lam_ref[...]
    lam = (jnp.exp(jnp.sum(lv[0:1, :] * lv[1:2, :], axis=1, keepdims=True))
           - jnp.exp(jnp.sum(lv[2:3, :] * lv[3:4, :], axis=1, keepdims=True)) + LAMBDA_INIT)

    first_map = lax.broadcasted_iota(jnp.int32, (2 * DK, tq), 0) < DK
    for h in range(HB):
        qh = qt_ref[h * 2 * DK:(h + 1) * 2 * DK, :]
        zero = jnp.zeros_like(qh)
        qpad_sc[h, :, 0:tq] = jnp.where(first_map, qh, zero)
        qpad_sc[h, :, tq:2 * tq] = jnp.where(first_map, zero, qh)
    m_sc[...] = jnp.full(m_sc.shape, NEG, F32)
    acc_sc[...] = jnp.zeros(acc_sc.shape, F32)

    corner = bp_ref.shape[-1]

    def logits(jb, h):
        row0 = pl.multiple_of(jb * tk, tk)
        return _mm(k_ref[h, pl.ds(row0, tk), :], qpad_sc[h])

    def put(slot, s):
        s_sc[slot] = s
        mx_sc[slot] = jnp.max(s, axis=0, keepdims=True)

    def stage1(jb, h, slot):
        put(slot, logits(jb, h))

    def add_corner(slot, h):
        for c0 in (0, tq):
            s_sc[slot, tk - corner:tk, c0:c0 + corner] = s_sc[slot, tk - corner:tk, c0:c0 + corner] + bp_ref[h]
            mx_sc[slot, :, c0:c0 + corner] = jnp.max(s_sc[slot, :, c0:c0 + corner], axis=0, keepdims=True)

    def diag_bias(h):
        return jnp.concatenate([bd_ref[h], bd_ref[h]], axis=1)

    def stage2(jb, h, slot):
        m_old = m_sc[h]
        m_new = jnp.maximum(m_old, mx_sc[slot])
        p = jnp.exp2(s_sc[slot] - m_new).astype(BF16)
        acc_sc[h] = jnp.exp2(m_old - m_new) * acc_sc[h] + _mm(vt_ref[h, jb], p)
        m_sc[h] = m_new

    stage1(0, 0, 0)
    if n_qblk > 1:
        def far_block(jb):
            stage1(jb, 1, 1)
            stage2(jb, 0, 0)
            stage1(jb, 2, 0)
            stage2(jb, 1, 1)
            stage1(jb, 3, 1)
            stage2(jb, 2, 0)
            stage1(jb + 1, 0, 0)
            stage2(jb, 3, 1)

        n_far = jnp.maximum(i - 1, 0)

        def four_blocks(jj, carry):
            for d in range(4):
                far_block(4 * jj + d)
            return carry
        lax.fori_loop(0, n_far // 4, four_blocks, 0)

        def rest(jb, carry):
            far_block(jb)
            return carry
        lax.fori_loop(4 * (n_far // 4), n_far, rest, 0)

        @pl.when(i >= 1)
        def _():
            jb = i - 1
            add_corner(0, 0)
            stage1(jb, 1, 1)
            add_corner(1, 1)
            stage2(jb, 0, 0)
            stage1(jb, 2, 0)
            add_corner(0, 2)
            stage2(jb, 1, 1)
            stage1(jb, 3, 1)
            add_corner(1, 3)
            stage2(jb, 2, 0)
            stage1(i, 0, 0)
            stage2(jb, 3, 1)

    put(0, s_sc[0] + diag_bias(0))
    put(1, logits(i, 1) + diag_bias(1))
    stage2(i, 0, 0)
    put(0, logits(i, 2) + diag_bias(2))
    stage2(i, 1, 1)
    put(1, logits(i, 3) + diag_bias(3))
    stage2(i, 2, 0)
    stage2(i, 3, 1)

    for h in range(HB):
        on = acc_sc[h, 0:DV, :] * (1.0 / acc_sc[h, DV:DV + 1, :])
        o = on[:, 0:tq] - lam * on[:, tq:2 * tq]
        o = o * lax.rsqrt(jnp.mean(o * o, axis=0, keepdims=True) + EPS) * (ghb_ref[...] * (1.0 - LAMBDA_INIT))
        o_ref[:, h * DV:(h + 1) * DV] = o.T.astype(o_ref.dtype)


def _bucket(rel):
    half, max_exact = NUM_BUCKETS // 2, NUM_BUCKETS // 4
    n = np.abs(rel).astype(np.int64)
    sq = np.maximum(n * n // (max_exact * max_exact), 1)
    large = max_exact + np.floor(np.log2(sq.astype(np.float64)) + 1e-9).astype(np.int64)
    large = np.minimum(large, half - 1)
    return np.where(rel > 0, half, 0) + np.where(n < max_exact, n, large)


def _bias_tiles(rel_bias, rel0, nk, tq):
    table = (rel_bias - rel_bias[NUM_BUCKETS // 2 - 1:NUM_BUCKETS // 2, :]) * LOG2E
    half, max_exact = NUM_BUCKETS // 2, NUM_BUCKETS // 4
    dist = np.arange(4 * MAX_DISTANCE)
    buckets = _bucket(-dist)
    rel = rel0 + lax.broadcasted_iota(jnp.int32, (nk, tq), 0) - lax.broadcasted_iota(jnp.int32, (nk, tq), 1)
    n = jnp.abs(rel)
    large = jnp.full((nk, tq), max_exact, jnp.int32)
    for b in range(max_exact + 1, half):
        large = large + (n >= int(dist[buckets >= b][0])).astype(jnp.int32)
    bucket = jnp.where(rel > 0, half, 0) + jnp.where(n < max_exact, n, large)
    tile = jnp.zeros((HB, nk, tq), F32)
    for b in range(NUM_BUCKETS):
        tile = jnp.where((bucket == b)[None], table[b][:, None, None], tile)
    return tile


def _attn(qt_blk, k16, vt_blk, bias_diag, bias_prev, g_head_b, lam_vecs, bsz, t, tq, tk):
    n_qblk = t // tq
    nk = k16.shape[2]
    n_kblk = nk // tk
    assert t % tq == 0 and nk % tk == 0 and (n_qblk == 1 or tq == tk)
    kern = functools.partial(_attn_kernel, tq=tq, tk=tk, n_qblk=n_qblk)
    return pl.pallas_call(
        kern,
        grid=(bsz, n_qblk),
        in_specs=[pl.BlockSpec((None, D_QB, tq), lambda b, i: (b * n_qblk + i, 0, 0)),
                  pl.BlockSpec((HB, None, nk, 2 * DK), lambda b, i: (0, b, 0, 0), pipeline_mode=pl.Buffered(1)),
                  pl.BlockSpec((HB, n_kblk, DV + ONES_ROWS, tk), lambda b, i: (0, b, 0, 0),
                               pipeline_mode=pl.Buffered(1)),
                  _const_spec(bias_diag.shape), _const_spec(bias_prev.shape),
                  _const_spec((DV, 1)), _const_spec((4, DK))],
        out_specs=pl.BlockSpec((None, tq, D_VB), lambda b, i: (b, i, 0)),
        out_shape=jax.ShapeDtypeStruct((bsz, t, D_VB), BF16),
        scratch_shapes=[pltpu.VMEM((HB, 2 * DK, 2 * tq), BF16), pltpu.VMEM((HB, 1, 2 * tq), F32),
                        pltpu.VMEM((HB, DV + ONES_ROWS, 2 * tq), F32),
                        pltpu.VMEM((2, tk, 2 * tq), F32), pltpu.VMEM((2, 1, 2 * tq), F32)],
        compiler_params=pltpu.CompilerParams(dimension_semantics=("parallel", "arbitrary"),
                                             vmem_limit_bytes=VMEM_LIMIT_BYTES),
        name="attn",
    )(qt_blk, k16, vt_blk, bias_diag, bias_prev, g_head_b.reshape(DV, 1), lam_vecs)


def _attn_step_kernel(qt_ref, kp_ref, vp_ref, kn_ref, vn_ref, bpast_ref, bnew_ref, ghb_ref, lam_ref, o_ref):
    tq = qt_ref.shape[-1]
    lv = lam_ref[...]
    lam = (jnp.exp(jnp.sum(lv[0:1, :] * lv[1:2, :], axis=1, keepdims=True))
           - jnp.exp(jnp.sum(lv[2:3, :] * lv[3:4, :], axis=1, keepdims=True)) + LAMBDA_INIT)
    first_map = lax.broadcasted_iota(jnp.int32, (2 * DK, tq), 0) < DK
    kp = kp_ref[...].reshape(kp_ref.shape[0], D_QB).astype(BF16)
    vp = vp_ref[...].reshape(vp_ref.shape[0], D_VB).astype(BF16)
    vn = vn_ref[...].reshape(vn_ref.shape[0], D_VB).astype(BF16)
    for h in range(HB):
        qh = qt_ref[h * 2 * DK:(h + 1) * 2 * DK, :]
        zero = jnp.zeros_like(qh)
        qpad = jnp.concatenate([jnp.where(first_map, qh, zero), jnp.where(first_map, zero, qh)], axis=1)
        s_p = _mm(kp[:, h * 2 * DK:(h + 1) * 2 * DK], qpad) + jnp.concatenate([bpast_ref[h], bpast_ref[h]], axis=1)
        s_n = _mm(kn_ref[h], qpad) + jnp.concatenate([bnew_ref[h], bnew_ref[h]], axis=1)
        m = jnp.maximum(jnp.max(s_p, axis=0, keepdims=True), jnp.max(s_n, axis=0, keepdims=True))
        p_p = jnp.exp2(s_p - m)
        p_n = jnp.exp2(s_n - m)
        den = jnp.sum(p_p, axis=0, keepdims=True) + jnp.sum(p_n, axis=0, keepdims=True)
        num = (lax.dot_general(vp[:, h * DV:(h + 1) * DV], p_p.astype(BF16), TN_DIMS, preferred_element_type=F32)
               + lax.dot_general(vn[:, h * DV:(h + 1) * DV], p_n.astype(BF16), TN_DIMS,
                                 preferred_element_type=F32))
        on = num * (1.0 / den)
        o = on[:, 0:tq] - lam * on[:, tq:2 * tq]
        o = o * lax.rsqrt(jnp.mean(o * o, axis=0, keepdims=True) + EPS) * (ghb_ref[...] * (1.0 - LAMBDA_INIT))
        o_ref[:, h * DV:(h + 1) * DV] = o.T.astype(o_ref.dtype)


def _attn_step(qt, k_past, v_past, k_new16, v_new, bias_past, bias_new, g_head_b, lam_vecs):
    bsz, _, t = qt.shape
    past = k_past.shape[1]
    return pl.pallas_call(
        _attn_step_kernel,
        grid=(bsz,),
        in_specs=[pl.BlockSpec((None, D_QB, t), lambda b: (b, 0, 0)),
                  pl.BlockSpec((None, past, HB, 2 * DK), lambda b: (b, 0, 0, 0)),
                  pl.BlockSpec((None, past, HB, DV), lambda b: (b, 0, 0, 0)),
                  pl.BlockSpec((HB, None, t, 2 * DK), lambda b: (0, b, 0, 0)),
                  pl.BlockSpec((None, t, HB, DV), lambda b: (b, 0, 0, 0)),
                  _const_spec(bias_past.shape), _const_spec(bias_new.shape),
                  _const_spec((DV, 1)), _const_spec((4, DK))],
        out_specs=pl.BlockSpec((None, t, D_VB), lambda b: (b, 0, 0)),
        out_shape=jax.ShapeDtypeStruct((bsz, t, D_VB), BF16),
        compiler_params=pltpu.CompilerParams(dimension_semantics=("parallel",),
                                             vmem_limit_bytes=VMEM_LIMIT_BYTES),
        name="attn_step",
    )(qt, k_past, v_past, k_new16, v_new, bias_past, bias_new, g_head_b.reshape(DV, 1), lam_vecs)


def _out_kernel(x_ref, h_ref, ob_ref, gha_ref, gpre_ref, gpost_ref, gpref_ref, gpostf_ref, wg_ref, bg_ref,
                wpa_ref, wpb_ref, wout_ref, wff1_ref, wff2_ref, o_ref):
    x = x_ref[...]
    u = _rms(x, gpre_ref[...]).astype(BF16)
    hn = jnp.concatenate([_rms(h_ref[:, h * DH:(h + 1) * DH], gha_ref[h:h + 1, :]) for h in range(HA)], axis=1)
    gates = jax.nn.sigmoid(_mm(u, wg_ref[...]) + bg_ref[...])
    h_a = (hn * gates[:, 0:D_A]).astype(BF16)
    y_a = _mm(h_a, wpa_ref[...])
    y_b = _mm(ob_ref[...], wpb_ref[...])
    gate_a = gates[:, D_A:D_A + D_MODEL]
    gate_b = gates[:, D_A + D_MODEL:]
    mix = _mm((gate_a * y_a + gate_b * y_b).astype(BF16), wout_ref[...])
    x1 = x + _rms(mix, gpost_ref[...])
    f = _rms(x1, gpref_ref[...]).astype(BF16)
    hf = jnp.zeros_like(x1)
    for c in range(FF // FF_CHUNK):
        mid = jnp.maximum(_mm(f, wff1_ref[:, c * FF_CHUNK:(c + 1) * FF_CHUNK]), 0.0)
        hf = hf + _mm((mid * mid).astype(BF16), wff2_ref[c * FF_CHUNK:(c + 1) * FF_CHUNK, :])
    o_ref[...] = x1 + _rms(hf, gpostf_ref[...])


def _out(x2d, h2d, ob2d, g_head_a, g_pre_mix, g_post_mix, g_pre_ffn, g_post_ffn, w_in, b_in, w_pa, w_pb, w_out,
         w_ff1, w_ff2):
    rows = x2d.shape[0]
    tm = min(rows, OUT_ROW_TILE)
    o_oa = 3 * D_A
    o_ga = 4 * D_A + 2 * HA + 2 * D_QB + D_VB

    def cols(lo, n):
        return w_in[:, lo:lo + n].astype(BF16), b_in[lo:lo + n].reshape(1, n)

    woa, boa = cols(o_oa, D_A)
    wgg, bgg = cols(o_ga, 2 * D_MODEL)
    wg = jnp.concatenate([woa, wgg], axis=1)
    bg = jnp.concatenate([boa, bgg], axis=1)
    consts = (g_head_a, g_pre_mix.reshape(1, -1), g_post_mix.reshape(1, -1), g_pre_ffn.reshape(1, -1),
              g_post_ffn.reshape(1, -1), wg, bg, w_pa.astype(BF16), w_pb.astype(BF16),
              w_out.astype(BF16), w_ff1.astype(BF16), w_ff2.astype(BF16))

    assert rows % tm == 0

    def row_spec(n):
        return pl.BlockSpec((tm, n), lambda i: (i, 0))

    return pl.pallas_call(
        _out_kernel,
        grid=(rows // tm,),
        in_specs=[row_spec(D_MODEL), row_spec(D_A), row_spec(D_VB)] + [_const_spec(c.shape) for c in consts],
        out_specs=row_spec(D_MODEL),
        out_shape=jax.ShapeDtypeStruct((rows, D_MODEL), F32),
        compiler_params=pltpu.CompilerParams(dimension_semantics=("parallel",),
                                             vmem_limit_bytes=VMEM_LIMIT_BYTES),
        name="out",
    )(x2d, h2d, ob2d, *consts)


def _layer(x, k_past, v_past, c0, n0, m0, conv0, rel_bias, g_pre_mix, g_post_mix, g_pre_ffn, g_post_ffn,
           w_in, b_in, conv_w, conv_b, g_head_a, w_pa, lam_vecs, g_head_b, w_pb, w_out, w_ff1, w_ff2):
    bsz, t, _ = x.shape
    past = k_past.shape[1]
    x2d = x.reshape(bsz * t, D_MODEL)
    qk_a, v_a, gates, k_new, k16, v_new, qt_blk, vt_blk = _proj(x2d, g_pre_mix, w_in, b_in)

    h_a, c_new, n_new, m_new = _mlstm(qk_a.reshape(bsz, t, 2 * D_A), v_a.reshape(bsz, t, D_A),
                                      gates, conv0, c0, n0, m0, conv_w, conv_b)

    if past == 0:
        tq = tk = min(t, ROW_TILE, KEY_TILE)
        s_idx = lax.broadcasted_iota(jnp.int32, (tk, tq), 0)
        t_idx = lax.broadcasted_iota(jnp.int32, (tk, tq), 1)
        allowed = (s_idx // CHUNK) <= (t_idx // CHUNK)
        bias_diag = jnp.where(allowed[None], _bias_tiles(rel_bias, 0, tk, tq), NEG)
        corner = min(BIAS_CORNER, tk)
        bias_prev = _bias_tiles(rel_bias, -corner, corner, corner)
        o_b = _attn(qt_blk, k16.reshape(HB, bsz, t, 2 * DK), vt_blk, bias_diag, bias_prev, g_head_b, lam_vecs, bsz, t,
                    tq, tk)
    else:
        nk = past + t
        s_idx = lax.broadcasted_iota(jnp.int32, (nk, t), 0)
        t_idx = lax.broadcasted_iota(jnp.int32, (nk, t), 1)
        allowed = (s_idx // CHUNK) <= ((past + t_idx) // CHUNK)
        bias = jnp.where(allowed[None], _bias_tiles(rel_bias, -past, nk, t), NEG)
        o_b = _attn_step(jnp.transpose(qt_blk.reshape(D_QB, bsz, t), (1, 0, 2)), k_past, v_past,
                         k16.reshape(HB, bsz, t, 2 * DK), v_new.reshape(bsz, t, HB, DV),
                         bias[:, :past], bias[:, past:], g_head_b, lam_vecs)

    y = _out(x2d, h_a.reshape(bsz * t, D_A), o_b.reshape(bsz * t, D_VB), g_head_a, g_pre_mix, g_post_mix, g_pre_ffn,
             g_post_ffn, w_in, b_in, w_pa, w_pb, w_out, w_ff1, w_ff2)
    return (y.reshape(bsz, t, D_MODEL), k_new.reshape(bsz, t, HB, 2 * DK), v_new.reshape(bsz, t, HB, DV),
            c_new, n_new, m_new[:, 0, :HA], qk_a.reshape(bsz, t, 2 * D_A)[:, t - (CONV_W - 1):, :])


def kernel(x_prompt, x_sample, cache_k, cache_v, state_C, state_n, state_m, state_conv, rel_bias, g_pre_mix,
           g_post_mix, g_pre_ffn, g_post_ffn, w_in, b_in, conv_w, conv_b, g_head_a, w_pa, lambda_q1, lambda_k1,
           lambda_q2, lambda_k2, g_head_b, w_pb, w_out, w_ff1, w_ff2):
    lam_vecs = jnp.concatenate([lambda_q1, lambda_k1, lambda_q2, lambda_k2], axis=0)
    weights = (rel_bias, g_pre_mix[0], g_post_mix[0], g_pre_ffn[0], g_post_ffn[0], w_in[0], b_in[0], conv_w[0],
               conv_b[0], g_head_a[0], w_pa[0], lam_vecs, g_head_b[0], w_pb[0], w_out[0], w_ff1[0], w_ff2[0])
    bp = x_prompt.shape[0]
    prompt = _layer(x_prompt, jnp.zeros((bp, 0, HB, 2 * DK), F32), jnp.zeros((bp, 0, HB, DV), F32),
                    jnp.zeros((bp, HA, DH, DH), F32), jnp.zeros((bp, HA, DH), F32), jnp.zeros((bp, HA), F32),
                    jnp.zeros((bp, CONV_W - 1, 2 * D_A), F32), *weights)
    sample = _layer(x_sample, cache_k[0], cache_v[0], state_C[0], state_n[0], state_m[0], state_conv[0], *weights)
    yp, ys = prompt[0], sample[0]
    return (yp, ys) + tuple(a[None] for a in prompt[1:]) + tuple(a[None] for a in sample[1:])
```
